```python
import math
import jax, jax.numpy as jnp
from jax import lax
import numpy as np

D_MODEL = 4096
BATCH = 2
SEQ = 8192
DEPTH = 1

HEAD_DIM = 128
N_HEADS_MIX = D_MODEL // HEAD_DIM
N_HEADS_DIL = N_HEADS_MIX // 2
N_HEADS_SB = N_HEADS_MIX - N_HEADS_DIL
W_DIL = N_HEADS_DIL * HEAD_DIM
W_SB = N_HEADS_SB * HEAD_DIM
W_MIX = W_DIL + W_SB
PROJ_WIDTH = 3 * W_MIX
SPLITS = (W_DIL, 2 * W_DIL, 3 * W_DIL, 3 * W_DIL + W_SB, 3 * W_DIL + 2 * W_SB)
DIL_PAIRS = ((128, 1), (512, 4), (2048, 16))
BLOCK = 128
ALIBI_MAX_BIAS = 8.0
N_MEM = 256
N_HEADS_CROSS = 4
CROSS_HEAD_DIM = 128
W_CROSS = N_HEADS_CROSS * CROSS_HEAD_DIM
N_GROUPS = 8
EXPERTS_PER_GROUP = 8
N_EXPERTS = N_GROUPS * EXPERTS_PER_GROUP
TOP_K_IN_GROUP = 2
D_EXPERT = D_MODEL * 3 // 16
MOE_CHUNK = 128
EPS = 1e-6

kernel_name = "hybrid_dilated_stickbreak_hmoe_layer"


def rms_norm(x, g):
    xf = x.astype(jnp.float32)
    y = xf * lax.rsqrt(jnp.mean(xf * xf, axis=-1, keepdims=True) + EPS)
    return (y * g.astype(jnp.float32)).astype(x.dtype)


def alibi_slopes(n):
    return jnp.exp2(-ALIBI_MAX_BIAS * jnp.arange(1, n + 1, dtype=jnp.float32) / n)


def dilated_branch(q, k, v, slopes, window, dilation):
    B, S, H, Dh = q.shape
    J = window // dilation
    unit = dilation * J
    Sp = -(-S // unit) * unit
    n = Sp // dilation
    nb = n // J

    def to_blocks(t):
        t = jnp.pad(t, ((0, 0), (0, Sp - S), (0, 0), (0, 0)))
        t = t.reshape(B, n, dilation, H, Dh).transpose(0, 2, 1, 3, 4)
        return t.reshape(B, dilation, nb, J, H, Dh)

    def with_prev(t):
        prev = jnp.pad(t, ((0, 0), (0, 0), (1, 0), (0, 0), (0, 0), (0, 0)))[:, :, :-1]
        return jnp.concatenate([prev, t], axis=3)

    qb = to_blocks(q)
    kw = with_prev(to_blocks(k))
    vw = with_prev(to_blocks(v))
    s = jnp.einsum('brnqhc,brnkhc->brnhqk', qb, kw) / math.sqrt(Dh)
    qi = jnp.arange(J)[:, None]
    ki = jnp.arange(2 * J)[None, :]
    dist = J + qi - ki
    valid = ((dist >= 0) & (dist <= J))[None] & ((jnp.arange(nb)[:, None, None] > 0) | (ki >= J)[None])
    bias = -slopes[:, None, None] * (dilation * dist).astype(jnp.float32)[None]
    s = jnp.where(valid[None, None, :, None], s + bias[None, None, None], -jnp.inf)
    m = jnp.max(s, axis=-1, keepdims=True)
    p = jnp.exp(s - m)
    l = jnp.sum(p, axis=-1)
    o = jnp.einsum('brnhqk,brnkhc->brnqhc', p, vw)
    l_t = jnp.swapaxes(l, 3, 4)
    o = o / l_t[..., None]
    lse = jnp.swapaxes(m[..., 0], 3, 4) + jnp.log(l_t)

    def from_blocks(t):
        t = t.reshape((B, dilation, n) + t.shape[4:])
        t = jnp.swapaxes(t, 1, 2).reshape((B, Sp) + t.shape[3:])
        return t[:, :S]

    return from_blocks(o), from_blocks(lse)


def dilated_mixture(q, k, v, slopes):
    outs, lses = [], []
    for window, dilation in DIL_PAIRS:
        o, lse = dilated_branch(q, k, v, slopes, window, dilation)
        outs.append(o)
        lses.append(lse)
    wts = jax.nn.softmax(jnp.stack(lses), axis=0)
    return jnp.sum(wts[..., None] * jnp.stack(outs), axis=0)


def stick_breaking(q, k, v):
    B, S, H, Dh = q.shape
    kpos = jnp.arange(S)

    def one_block(b):
        qb = lax.dynamic_slice_in_dim(q, b * BLOCK, BLOCK, axis=1)
        z = jnp.einsum('bqhc,bkhc->bhqk', qb, k) / math.sqrt(Dh)
        qpos = b * BLOCK + jnp.arange(BLOCK)
        causal = kpos[None, :] < qpos[:, None]
        log_keep = jnp.where(causal, jax.nn.log_sigmoid(-z), 0.0)
        later = lax.cumsum(log_keep, axis=3, reverse=True) - log_keep
        a = jnp.where(causal, jnp.exp(jax.nn.log_sigmoid(z) + later), 0.0)
        return jnp.einsum('bhqk,bkhc->bqhc', a, v)

    o = lax.map(one_block, jnp.arange(S // BLOCK))
    return jnp.moveaxis(o, 0, 1).reshape(B, S, H, Dh)


def cross_attend(h, m, w_q, w_kv, g_q, g_k, w_o):
    B, S, _ = h.shape
    M = m.shape[1]
    q = (h @ w_q).reshape(B, S, N_HEADS_CROSS, CROSS_HEAD_DIM)
    kv = (m @ w_kv).reshape(B, M, 2, N_HEADS_CROSS, CROSS_HEAD_DIM)
    q = rms_norm(q, g_q).astype(jnp.float32)
    k = rms_norm(kv[:, :, 0], g_k).astype(jnp.float32)
    v = kv[:, :, 1].astype(jnp.float32)
    p = jax.nn.softmax(jnp.einsum('bshc,bmhc->bhsm', q, k) / math.sqrt(CROSS_HEAD_DIM), axis=-1)
    o = jnp.einsum('bhsm,bmhc->bshc', p, v).reshape(B, S, W_CROSS)
    return o.astype(h.dtype) @ w_o


def hier_moe(h, w_group, b_group, w_router, b_router, w_gate, w_up, w_down):
    N, D = h.shape
    hf = h.astype(jnp.float32)
    pg = jax.nn.softmax(hf @ w_group.astype(jnp.float32) + b_group.astype(jnp.float32), axis=-1)
    g_gate, g_idx = lax.top_k(pg, 1)
    el = (hf @ w_router.astype(jnp.float32)).reshape(N, N_GROUPS, EXPERTS_PER_GROUP)
    el = el + b_router.astype(jnp.float32)
    el = jnp.take_along_axis(el, g_idx[:, :, None], axis=1)[:, 0]
    e_w, e_idx = lax.top_k(jax.nn.softmax(el, axis=-1), TOP_K_IN_GROUP)
    gate = g_gate * e_w / jnp.sum(e_w, axis=-1, keepdims=True)
    eid = (g_idx * EXPERTS_PER_GROUP + e_idx).reshape(-1)
    ew = gate.reshape(-1)
    tok = jnp.repeat(jnp.arange(N, dtype=jnp.int32), TOP_K_IN_GROUP)
    A = N * TOP_K_IN_GROUP
    order = jnp.argsort(eid)
    se = eid[order]
    counts = jax.ops.segment_sum(jnp.ones((A,), jnp.int32), eid, num_segments=N_EXPERTS)
    starts = jnp.cumsum(counts) - counts
    pcounts = (counts + MOE_CHUNK - 1) // MOE_CHUNK * MOE_CHUNK
    pends = jnp.cumsum(pcounts)
    pstarts = pends - pcounts
    dest = pstarts[se] + (jnp.arange(A) - starts[se])
    n_chunks = -(-A // MOE_CHUNK) + N_EXPERTS
    P = n_chunks * MOE_CHUNK
    slot_tok = jnp.full((P,), N, jnp.int32).at[dest].set(tok[order])
    slot_w = jnp.zeros((P,), jnp.float32).at[dest].set(ew[order])
    chunk_e = jnp.minimum(jnp.searchsorted(pends, jnp.arange(n_chunks) * MOE_CHUNK, side='right'), N_EXPERTS - 1)
    h_pad = jnp.concatenate([h, jnp.zeros((1, D), h.dtype)], axis=0)

    def run_chunk(args):
        t, e = args
        hc = h_pad[t]
        return (jax.nn.silu(hc @ w_gate[e]) * (hc @ w_up[e])) @ w_down[e]

    out = lax.map(run_chunk, (slot_tok.reshape(n_chunks, MOE_CHUNK), chunk_e)).reshape(P, D)
    y = jnp.zeros((N + 1, D), out.dtype).at[slot_tok].add(out * slot_w[:, None].astype(out.dtype))
    return y[:N]


def setup_inputs(seed: int = 0) -> dict:
    key = jax.random.key(seed)
    ks = jax.random.split(key, 24)
    f32 = jnp.float32
    nrm = lambda k, shape, fan_in: jax.random.normal(k, shape, f32) * (fan_in ** -0.5)
    gain = lambda k, shape: 1.0 + 0.02 * jax.random.normal(k, shape, f32)
    L, D = DEPTH, D_MODEL
    return {
        "x": jax.random.normal(ks[0], (BATCH, SEQ, D), f32),
        "mem": jax.random.normal(ks[1], (BATCH, N_MEM, D), f32),
        "g_mix": gain(ks[2], (L, D)),
        "w_in": nrm(ks[3], (L, D, PROJ_WIDTH), D),
        "g_q_dil": gain(ks[4], (L, HEAD_DIM)),
        "g_k_dil": gain(ks[5], (L, HEAD_DIM)),
        "g_out_dil": gain(ks[6], (L, W_DIL)),
        "g_out_sb": gain(ks[7], (L, W_SB)),
        "w_out": nrm(ks[8], (L, W_MIX, D), W_MIX),
        "g_cross": gain(ks[9], (L, D)),
        "g_mem": gain(ks[10], (L, D)),
        "w_q_cross": nrm(ks[11], (L, D, W_CROSS), D),
        "w_kv_cross": nrm(ks[12], (L, D, 2 * W_CROSS), D),
        "g_q_cross": gain(ks[13], (L, CROSS_HEAD_DIM)),
        "g_k_cross": gain(ks[14], (L, CROSS_HEAD_DIM)),
        "w_o_cross": nrm(ks[15], (L, W_CROSS, D), W_CROSS),
        "g_ffn": gain(ks[16], (L, D)),
        "w_group": nrm(ks[17], (L, D, N_GROUPS), D),
        "b_group": 0.01 * jax.random.normal(ks[18], (L, N_GROUPS), f32),
        "w_router": nrm(ks[19], (L, D, N_EXPERTS), D),
        "b_router": 0.01 * jax.random.normal(ks[20], (L, N_GROUPS, EXPERTS_PER_GROUP), f32),
        "w_exp_gate": nrm(ks[21], (L, N_EXPERTS, D, D_EXPERT), D),
        "w_exp_up": nrm(ks[22], (L, N_EXPERTS, D, D_EXPERT), D),
        "w_exp_down": nrm(ks[23], (L, N_EXPERTS, D_EXPERT, D), D_EXPERT),
    }


def reference(x, mem, g_mix, w_in, g_q_dil, g_k_dil, g_out_dil, g_out_sb, w_out, g_cross, g_mem, w_q_cross, w_kv_cross, g_q_cross, g_k_cross, w_o_cross, g_ffn, w_group, b_group, w_router, b_router, w_exp_gate, w_exp_up, w_exp_down):
    B, S, D = x.shape
    f32 = jnp.float32
    slopes = alibi_slopes(N_HEADS_DIL)
    for l in range(DEPTH):
        h = rms_norm(x, g_mix[l])
        qd, kd, vd, qs, ks_, vs = jnp.split(h @ w_in[l], SPLITS, axis=-1)
        qd = rms_norm(qd.reshape(B, S, N_HEADS_DIL, HEAD_DIM), g_q_dil[l]).astype(f32)
        kd = rms_norm(kd.reshape(B, S, N_HEADS_DIL, HEAD_DIM), g_k_dil[l]).astype(f32)
        vd = vd.reshape(B, S, N_HEADS_DIL, HEAD_DIM).astype(f32)
        o_dil = dilated_mixture(qd, kd, vd, slopes).reshape(B, S, W_DIL)
        o_sb = stick_breaking(qs.reshape(B, S, N_HEADS_SB, HEAD_DIM).astype(f32),
                              ks_.reshape(B, S, N_HEADS_SB, HEAD_DIM).astype(f32),
                              vs.reshape(B, S, N_HEADS_SB, HEAD_DIM).astype(f32)).reshape(B, S, W_SB)
        merged = jnp.concatenate([rms_norm(o_dil, g_out_dil[l]), rms_norm(o_sb, g_out_sb[l])], axis=-1)
        x = x + merged.astype(x.dtype) @ w_out[l]
        x = x + cross_attend(rms_norm(x, g_cross[l]), rms_norm(mem, g_mem[l]), w_q_cross[l], w_kv_cross[l], g_q_cross[l], g_k_cross[l], w_o_cross[l])
        hf = rms_norm(x, g_ffn[l]).reshape(B * S, D)
        x = x + hier_moe(hf, w_group[l], b_group[l], w_router[l], b_router[l], w_exp_gate[l], w_exp_up[l], w_exp_down[l]).reshape(B, S, D).astype(x.dtype)
    return x
```

```python
import functools
import math

import jax
import jax.numpy as jnp
from jax import lax
from jax.experimental import pallas as pl
from jax.experimental.pallas import tpu as pltpu

F32 = jnp.float32
BF16 = jnp.bfloat16
I32 = jnp.int32
U32 = jnp.uint32

LANE = 128
HEAD_DIM = 128
EPS = 1e-6
DIL_PAIRS = ((128, 1), (512, 4), (2048, 16))
DIL_STEPS = 128
ALIBI_MAX_BIAS = 8.0
N_HEADS_CROSS = 4
N_GROUPS = 8
EXPERTS_PER_GROUP = 8
MOE_ROW_CHUNK = 128
NEG_BIG = -1e30
VMEM_LIMIT_CAP = 60000 * 1024
VMEM_INTERNAL = 12 * 1024 * 1024


def _vmem_limit(*block_bytes):
    return int(min(VMEM_LIMIT_CAP, 2 * sum(block_bytes) + VMEM_INTERNAL))


def _nbytes(shape, dtype):
    return math.prod(shape) * jnp.dtype(dtype).itemsize


def _params(sem, *block_bytes):
    return pltpu.CompilerParams(dimension_semantics=sem, vmem_limit_bytes=_vmem_limit(*block_bytes))


def _rms(x, g):
    return x * lax.rsqrt(jnp.mean(x * x, axis=-1, keepdims=True) + EPS) * g


def _rmsnorm_cast_kernel(x_ref, g_ref, o_ref):
    o_ref[...] = _rms(x_ref[...], g_ref[...]).astype(o_ref.dtype)


def _rmsnorm_cast(x, g, tm):
    n, d = x.shape
    return pl.pallas_call(
        _rmsnorm_cast_kernel,
        grid=(n // tm,),
        in_specs=[pl.BlockSpec((tm, d), lambda i: (i, 0)), pl.BlockSpec((1, d), lambda i: (0, 0))],
        out_specs=pl.BlockSpec((tm, d), lambda i: (i, 0)),
        out_shape=jax.ShapeDtypeStruct((n, d), BF16),
        compiler_params=_params(("arbitrary",), _nbytes((tm, d), F32), _nbytes((tm, d), BF16)),
        name="rmsnorm_cast",
    )(x, g.reshape(1, d))


def _proj_kernel(a_ref, w_ref, g_ref, o_ref, *, n_norm_tiles):
    j = pl.program_id(1)
    acc = jnp.dot(a_ref[...], w_ref[...], preferred_element_type=F32)
    tn = acc.shape[1]

    @pl.when(j < n_norm_tiles)
    def _():
        for h in range(tn // HEAD_DIM):
            sl = slice(h * HEAD_DIM, (h + 1) * HEAD_DIM)
            o_ref[:, sl] = _rms(acc[:, sl], g_ref[:, sl]).astype(o_ref.dtype)

    @pl.when(j >= n_norm_tiles)
    def _():
        o_ref[...] = (acc * g_ref[...]).astype(o_ref.dtype)


def _proj(a, w, gain, col0, ncols, norm_cols, out_dtype, tm, tn, name):
    n, k = a.shape
    assert col0 % tn == 0 and ncols % tn == 0 and norm_cols % tn == 0 and n % tm == 0
    jb = col0 // tn
    return pl.pallas_call(
        functools.partial(_proj_kernel, n_norm_tiles=norm_cols // tn),
        grid=(n // tm, ncols // tn),
        in_specs=[
            pl.BlockSpec((tm, k), lambda i, j: (i, 0)),
            pl.BlockSpec((k, tn), lambda i, j: (0, j + jb)),
            pl.BlockSpec((1, tn), lambda i, j: (0, j)),
        ],
        out_specs=pl.BlockSpec((tm, tn), lambda i, j: (i, j)),
        out_shape=jax.ShapeDtypeStruct((n, ncols), out_dtype),
        compiler_params=_params(("arbitrary", "arbitrary"), _nbytes((tm, k), BF16), _nbytes((k, tn), BF16),
                                _nbytes((tm, tn), F32)),
        name=name,
    )(a, w, gain.reshape(1, ncols))


def _dil_kernel(slope_ref, q_ref, k_ref, v_ref, o_ref, ob0, ob1, ob2, ls0, ls1, ls2, *, tq):
    h = pl.program_id(1)
    t0 = pl.program_id(2) * tq
    slope = slope_ref[h]
    J = DIL_STEPS
    qi = lax.broadcasted_iota(I32, (J, 2 * J), 0)
    ki = lax.broadcasted_iota(I32, (J, 2 * J), 1)
    dist = J + qi - ki
    valid = (dist >= 0) & (dist <= J)
    distf = dist.astype(F32)
    in_cur = ki >= J

    for (window, d), ob, ls in zip(DIL_PAIRS, (ob0, ob1, ob2), (ls0, ls1, ls2)):
        assert window // d == J and tq % (d * J) == 0
        bias = jnp.where(valid, (-slope * d) * distf, NEG_BIG)
        shift = d.bit_length() - 1

        def ld(ref, start, d=d):
            if d == 1:
                return ref[pl.ds(start, J), :]
            return ref[pl.ds(start, J, stride=d), :]

        def unit(u, carry, d=d, shift=shift, bias=bias, ob=ob, ls=ls, ld=ld):
            off = (u & (d - 1)) + (u >> shift) * (d * J)
            cur0 = t0 + off
            prev0 = cur0 - d * J
            has_prev = prev0 >= 0
            prev0 = jnp.maximum(prev0, 0)
            qs = ld(q_ref, off).astype(BF16)
            kcat = jnp.concatenate([ld(k_ref, prev0), ld(k_ref, cur0)], axis=0).astype(BF16)
            vcat = jnp.concatenate([ld(v_ref, prev0), ld(v_ref, cur0)], axis=0).astype(BF16)
            s = lax.dot_general(qs, kcat, (((1,), (1,)), ((), ())), preferred_element_type=F32) + bias
            s = jnp.where(jnp.logical_or(in_cur, has_prev), s, NEG_BIG)
            m = jnp.max(s, axis=1, keepdims=True)
            p = jnp.exp(s - m)
            l = jnp.sum(p, axis=1, keepdims=True)
            o = jnp.dot(p.astype(BF16), vcat, preferred_element_type=F32) / l
            lse = jnp.broadcast_to(m + jnp.log(l), (J, HEAD_DIM))
            if d == 1:
                ob[pl.ds(off, J), :] = o
                ls[pl.ds(off, J), :] = lse
            else:
                ob[pl.ds(off, J, stride=d), :] = o
                ls[pl.ds(off, J, stride=d), :] = lse
            return carry

        lax.fori_loop(0, tq // J, unit, 0)

    l0, l1, l2 = ls0[...], ls1[...], ls2[...]
    mx = jnp.maximum(jnp.maximum(l0, l1), l2)
    w0, w1, w2 = jnp.exp(l0 - mx), jnp.exp(l1 - mx), jnp.exp(l2 - mx)
    o_ref[...] = (w0 * ob0[...] + w1 * ob1[...] + w2 * ob2[...]) / (w0 + w1 + w2)


def _dilated(qkv, slopes, batch, seq, n_heads, tq):
    assert seq % tq == 0
    n = batch * seq
    nq = seq // tq
    blk = (tq, HEAD_DIM)
    full = (seq, HEAD_DIM)
    scr = [pltpu.VMEM(blk, F32)] * 6
    return pl.pallas_call(
        functools.partial(_dil_kernel, tq=tq),
        grid=(batch, n_heads, nq),
        in_specs=[
            pl.BlockSpec(memory_space=pltpu.SMEM),
            pl.BlockSpec(blk, lambda b, h, c: (b * nq + c, h)),
            pl.BlockSpec(full, lambda b, h, c: (b, n_heads + h)),
            pl.BlockSpec(full, lambda b, h, c: (b, 2 * n_heads + h)),
        ],
        out_specs=pl.BlockSpec(blk, lambda b, h, c: (b * nq + c, h)),
        out_shape=jax.ShapeDtypeStruct((n, n_heads * HEAD_DIM), F32),
        scratch_shapes=scr,
        compiler_params=_params(("arbitrary",) * 3, 2 * _nbytes(full, F32), 5 * _nbytes(blk, F32)),
        name="dilated_attn",
    )(slopes, qkv, qkv, qkv)


def _sb_kernel(q_ref, k_ref, v_ref, o_ref, *, tq):
    i = pl.program_id(2)
    q = q_ref[...]
    row = lax.broadcasted_iota(I32, (tq, tq), 0)
    col = lax.broadcasted_iota(I32, (tq, tq), 1)
    after = (row > col).astype(BF16)
    causal = col < row

    def block(j, carry, acc, diag):
        start = pl.multiple_of(j * tq, tq)
        kj = k_ref[pl.ds(start, tq), :]
        vj = v_ref[pl.ds(start, tq), :]
        z = lax.dot_general(q, kj, (((1,), (1,)), ((), ())), preferred_element_type=F32)
        log_keep = -(jnp.maximum(z, 0.0) + jnp.log(1.0 + jnp.exp(-jnp.abs(z))))
        log_beta = z + log_keep
        if diag:
            log_keep = jnp.where(causal, log_keep, 0.0)
        hi = log_keep.astype(BF16)
        lo = (log_keep - hi.astype(F32)).astype(BF16)
        later = (jnp.dot(hi, after, preferred_element_type=F32)
                 + jnp.dot(lo, after, preferred_element_type=F32))
        a = jnp.exp(log_beta + later + carry)
        if diag:
            a = jnp.where(causal, a, 0.0)
        acc = acc + jnp.dot(a.astype(BF16), vj, preferred_element_type=F32)
        carry = carry + later[:, 0:1] + log_keep[:, 0:1]
        return carry, acc

    carry, acc = block(i, jnp.zeros((tq, 1), F32), jnp.zeros((tq, HEAD_DIM), F32), True)

    def body(jj, c):
        return block(i - jj, c[0], c[1], False)

    carry, acc = lax.fori_loop(1, i + 1, body, (carry, acc))
    o_ref[...] = acc


def _stick_breaking(qkv, batch, seq, n_heads, tq):
    assert seq % tq == 0
    n = batch * seq
    nq = seq // tq
    blk = (tq, HEAD_DIM)
    full = (seq, HEAD_DIM)
    return pl.pallas_call(
        functools.partial(_sb_kernel, tq=tq),
        grid=(batch, n_heads, nq),
        in_specs=[
            pl.BlockSpec(blk, lambda b, h, i: (b * nq + i, h)),
            pl.BlockSpec(full, lambda b, h, i: (b, n_heads + h)),
            pl.BlockSpec(full, lambda b, h, i: (b, 2 * n_heads + h)),
        ],
        out_specs=pl.BlockSpec(blk, lambda b, h, i: (b * nq + i, h)),
        out_shape=jax.ShapeDtypeStruct((n, n_heads * HEAD_DIM), F32),
        compiler_params=_params(("arbitrary",) * 3, 2 * _nbytes(full, BF16), 2 * _nbytes(blk, F32)),
        name="stick_breaking",
    )(qkv, qkv, qkv)


def _outproj_kernel(od_ref, os_ref, gd_ref, gs_ref, w_ref, x_ref, o_ref, m_scr):
    wd = od_ref.shape[1]

    @pl.when(pl.program_id(1) == 0)
    def _():
        m_scr[:, :wd] = _rms(od_ref[...], gd_ref[...]).astype(BF16)
        m_scr[:, wd:] = _rms(os_ref[...], gs_ref[...]).astype(BF16)

    o_ref[...] = x_ref[...] + jnp.dot(m_scr[...], w_ref[...], preferred_element_type=F32)


def _outproj(o_dil, o_sb, g_dil, g_sb, w, x, tm, tn):
    n, wd = o_dil.shape
    ws = o_sb.shape[1]
    d = x.shape[1]
    return pl.pallas_call(
        _outproj_kernel,
        grid=(n // tm, d // tn),
        in_specs=[
            pl.BlockSpec((tm, wd), lambda i, j: (i, 0)),
            pl.BlockSpec((tm, ws), lambda i, j: (i, 0)),
            pl.BlockSpec((1, wd), lambda i, j: (0, 0)),
            pl.BlockSpec((1, ws), lambda i, j: (0, 0)),
            pl.BlockSpec((wd + ws, tn), lambda i, j: (0, j)),
            pl.BlockSpec((tm, tn), lambda i, j: (i, j)),
        ],
        out_specs=pl.BlockSpec((tm, tn), lambda i, j: (i, j)),
        out_shape=jax.ShapeDtypeStruct((n, d), F32),
        scratch_shapes=[pltpu.VMEM((tm, wd + ws), BF16)],
        compiler_params=_params(("arbitrary", "arbitrary"), _nbytes((tm, wd + ws), F32),
                                _nbytes((wd + ws, tn), BF16), 2 * _nbytes((tm, tn), F32),
                                _nbytes((tm, wd + ws), BF16) // 2),
        name="out_proj",
    )(o_dil, o_sb, g_dil.reshape(1, wd), g_sb.reshape(1, ws), w, x)


def _mem_kv_kernel(m_ref, gm_ref, w_ref, gk_ref, k_ref, v_ref):
    hm = _rms(m_ref[0], gm_ref[...]).astype(BF16)
    kv = jnp.dot(hm, w_ref[...], preferred_element_type=F32)
    wc = k_ref.shape[2]
    for h in range(wc // HEAD_DIM):
        sl = slice(h * HEAD_DIM, (h + 1) * HEAD_DIM)
        k_ref[0, :, sl] = _rms(kv[:, sl], gk_ref[...]).astype(BF16)
    v_ref[0] = kv[:, wc:].astype(BF16)


def _mem_kv(mem, g_mem, w_kv, g_k):
    b, m, d = mem.shape
    wc = w_kv.shape[1] // 2
    out = jax.ShapeDtypeStruct((b, m, wc), BF16)
    return pl.pallas_call(
        _mem_kv_kernel,
        grid=(b,),
        in_specs=[
            pl.BlockSpec((1, m, d), lambda i: (i, 0, 0)),
            pl.BlockSpec((1, d), lambda i: (0, 0)),
            pl.BlockSpec((d, 2 * wc), lambda i: (0, 0)),
            pl.BlockSpec((1, HEAD_DIM), lambda i: (0, 0)),
        ],
        out_specs=[pl.BlockSpec((1, m, wc), lambda i: (i, 0, 0))] * 2,
        out_shape=[out, out],
        compiler_params=_params(("arbitrary",), _nbytes((m, d), F32), _nbytes((d, 2 * wc), BF16)),
        name="mem_kv",
    )(mem, g_mem.reshape(1, d), w_kv, g_k.reshape(1, HEAD_DIM))


def _pack_bf16_pairs(x):
    half = x.shape[1] // 2
    lo = pltpu.bitcast(x[:, :half].astype(BF16).astype(F32), U32)
    hi = pltpu.bitcast(x[:, half:].astype(BF16).astype(F32), U32)
    return hi | (lo >> 16)


def _unpack_bf16_pairs(p):
    lo = pltpu.bitcast(p << 16, F32)
    hi = pltpu.bitcast(p & jnp.uint32(0xFFFF0000), F32)
    return jnp.concatenate([lo, hi], axis=1).astype(BF16)


def _cross_router_kernel(x1_ref, gc_ref, wq_ref, gq_ref, kc_ref, vc_ref, wo_ref, gf_ref, wr_ref, br_ref,
                         x2_ref, hp_ref, eid_ref, gate_ref):
    x1 = x1_ref[...]
    h = _rms(x1, gc_ref[...]).astype(BF16)
    q = jnp.dot(h, wq_ref[...], preferred_element_type=F32)
    heads = []
    for hd in range(q.shape[1] // HEAD_DIM):
        sl = slice(hd * HEAD_DIM, (hd + 1) * HEAD_DIM)
        qh = _rms(q[:, sl], gq_ref[...]).astype(BF16)
        s = lax.dot_general(qh, kc_ref[0, :, sl], (((1,), (1,)), ((), ())), preferred_element_type=F32)
        p = jnp.exp(s - jnp.max(s, axis=1, keepdims=True))
        l = jnp.sum(p, axis=1, keepdims=True)
        heads.append(jnp.dot(p.astype(BF16), vc_ref[0, :, sl], preferred_element_type=F32) / l)
    o = jnp.concatenate(heads, axis=1).astype(BF16)
    x2 = x1 + jnp.dot(o, wo_ref[...], preferred_element_type=F32)
    x2_ref[...] = x2

    hf = _rms(x2, gf_ref[...])
    hp_ref[...] = _pack_bf16_pairs(hf)

    n_exp = N_GROUPS * EXPERTS_PER_GROUP
    lg = jnp.dot(hf, wr_ref[...], preferred_element_type=F32, precision=lax.Precision.HIGHEST) + br_ref[...]
    lane = lax.broadcasted_iota(I32, lg.shape, 1)
    lanef = lane.astype(F32)
    ninf = -jnp.inf
    is_g = (lane >= n_exp) & (lane < n_exp + N_GROUPS)
    gl = jnp.where(is_g, lg, ninf)
    gmax = jnp.max(gl, axis=1, keepdims=True)
    gidx = jnp.min(jnp.where(gl == gmax, lanef, 1e9), axis=1, keepdims=True) - n_exp
    g_gate = 1.0 / jnp.sum(jnp.exp(gl - gmax), axis=1, keepdims=True)
    in_grp = (lane < n_exp) & ((lane // EXPERTS_PER_GROUP).astype(F32) == gidx)
    el = jnp.where(in_grp, lg, ninf)
    m1 = jnp.max(el, axis=1, keepdims=True)
    i1 = jnp.min(jnp.where(el == m1, lanef, 1e9), axis=1, keepdims=True)
    el2 = jnp.where(lanef == i1, ninf, el)
    m2 = jnp.max(el2, axis=1, keepdims=True)
    i2 = jnp.min(jnp.where(el2 == m2, lanef, 1e9), axis=1, keepdims=True)
    p2 = jnp.exp(m2 - m1)
    den = 1.0 + p2
    eid_ref[...] = jnp.where(lane == 0, i1, jnp.where(lane == 1, i2, 0.0)).astype(I32)
    gate_ref[...] = jnp.where(lane == 0, g_gate / den, jnp.where(lane == 1, g_gate * p2 / den, 0.0))


def _cross_router(x1, g_cross, wq, gq, kc, vc, wo, g_ffn, wr, br, seq, tm):
    n, d = x1.shape
    wc = wq.shape[1]
    m = kc.shape[1]
    tiles_per_batch = seq // tm
    const = lambda i: (0, 0)
    return pl.pallas_call(
        _cross_router_kernel,
        grid=(n // tm,),
        in_specs=[
            pl.BlockSpec((tm, d), lambda i: (i, 0)),
            pl.BlockSpec((1, d), const),
            pl.BlockSpec((d, wc), const),
            pl.BlockSpec((1, HEAD_DIM), const),
            pl.BlockSpec((1, m, wc), lambda i: (i // tiles_per_batch, 0, 0)),
            pl.BlockSpec((1, m, wc), lambda i: (i // tiles_per_batch, 0, 0)),
            pl.BlockSpec((wc, d), const),
            pl.BlockSpec((1, d), const),
            pl.BlockSpec((d, LANE), const),
            pl.BlockSpec((1, LANE), const),
        ],
        out_specs=[
            pl.BlockSpec((tm, d), lambda i: (i, 0)),
            pl.BlockSpec((tm, d // 2), lambda i: (i, 0)),
            pl.BlockSpec((tm, LANE), lambda i: (i, 0)),
            pl.BlockSpec((tm, LANE), lambda i: (i, 0)),
        ],
        out_shape=[
            jax.ShapeDtypeStruct((n, d), F32),
            jax.ShapeDtypeStruct((n, d // 2), U32),
            jax.ShapeDtypeStruct((n, LANE), I32),
            jax.ShapeDtypeStruct((n, LANE), F32),
        ],
        compiler_params=_params(("arbitrary",), 2 * _nbytes((tm, d), F32), _nbytes((tm, d // 2), U32),
                                2 * _nbytes((d, wc), BF16), _nbytes((d, LANE), F32)),
        name="cross_router",
    )(x1, g_cross.reshape(1, d), wq, gq.reshape(1, HEAD_DIM), kc, vc, wo, g_ffn.reshape(1, d), wr, br)


def _moe_gather_kernel(nused_ref, tok_ref, src_ref, dst_ref, sem, *, rt):
    t = pl.program_id(0)

    def row_copy(r):
        return pltpu.make_async_copy(src_ref.at[pl.ds(tok_ref[0, 0, r], 1)], dst_ref.at[pl.ds(t * rt + r, 1)], sem)

    @pl.when(t < nused_ref[0])
    def _():
        def start(r, c):
            row_copy(r).start()
            return c

        def wait(r, c):
            row_copy(r).wait()
            return c

        lax.fori_loop(0, rt, start, 0)
        lax.fori_loop(0, rt, wait, 0)


def _moe_gather(n_used, slot_tok, hp, n_tiles, rt):
    half = hp.shape[1]
    return pl.pallas_call(
        functools.partial(_moe_gather_kernel, rt=rt),
        grid=(n_tiles,),
        in_specs=[
            pl.BlockSpec(memory_space=pltpu.SMEM),
            pl.BlockSpec((1, 1, rt), lambda t: (t, 0, 0), memory_space=pltpu.SMEM),
            pl.BlockSpec(memory_space=pl.ANY),
        ],
        out_specs=pl.BlockSpec(memory_space=pl.ANY),
        out_shape=jax.ShapeDtypeStruct((n_tiles * rt, half), U32),
        scratch_shapes=[pltpu.SemaphoreType.DMA(())],
        compiler_params=pltpu.CompilerParams(dimension_semantics=("arbitrary",)),
        name="moe_gather",
    )(n_used, slot_tok.reshape(n_tiles, 1, rt), hp)


def _moe_up_kernel(nused_ref, te_ref, tblk_ref, trows_ref, xs_ref, wg_ref, wu_ref, h_ref):
    t = pl.program_id(0)

    @pl.when(t < nused_ref[0])
    def _():
        h_ref[...] = jnp.zeros_like(h_ref)
        wg = wg_ref[0].astype(BF16)
        wu = wu_ref[0].astype(BF16)

        def chunk(ch, c):
            r0 = pl.multiple_of(ch * MOE_ROW_CHUNK, MOE_ROW_CHUNK)
            x = _unpack_bf16_pairs(xs_ref[pl.ds(r0, MOE_ROW_CHUNK), :])
            g = jnp.dot(x, wg, preferred_element_type=F32)
            u = jnp.dot(x, wu, preferred_element_type=F32)
            h_ref[pl.ds(r0, MOE_ROW_CHUNK), :] = (g * jax.nn.sigmoid(g) * u).astype(BF16)
            return c

        lax.fori_loop(0, pl.cdiv(trows_ref[t], MOE_ROW_CHUNK), chunk, 0)


def _moe_down_kernel(nused_ref, te_ref, tblk_ref, trows_ref, h_ref, wd_ref, sw_ref, y_ref):
    t = pl.program_id(0)

    @pl.when(t < nused_ref[0])
    def _():
        y_ref[...] = jnp.zeros_like(y_ref)
        wd = wd_ref[0].astype(BF16)

        def chunk(ch, c):
            r0 = pl.multiple_of(ch * MOE_ROW_CHUNK, MOE_ROW_CHUNK)
            rows = pl.ds(r0, MOE_ROW_CHUNK)
            y_ref[rows, :] = jnp.dot(h_ref[rows, :], wd, preferred_element_type=F32) * sw_ref[rows, :]
            return c

        lax.fori_loop(0, pl.cdiv(trows_ref[t], MOE_ROW_CHUNK), chunk, 0)


def _moe_experts(n_used, tile_e, tile_blk, tile_rows, xs, slot_w, w_gate, w_up, w_down, n_tiles, rt, ce, tn):
    n_exp, d, de = w_gate.shape
    half = xs.shape[1]
    p_rows = n_tiles * rt
    nc = de // ce
    nn = d // tn

    def inner(t, c, nused, last):
        return jnp.where(t < nused[0], c, last)

    h_mid = pl.pallas_call(
        _moe_up_kernel,
        grid_spec=pltpu.PrefetchScalarGridSpec(
            num_scalar_prefetch=4,
            grid=(n_tiles, nc),
            in_specs=[
                pl.BlockSpec((rt, half), lambda t, c, nu, te, tb, tr: (tb[t], 0)),
                pl.BlockSpec((1, d, ce), lambda t, c, nu, te, tb, tr: (te[t], 0, inner(t, c, nu, nc - 1))),
                pl.BlockSpec((1, d, ce), lambda t, c, nu, te, tb, tr: (te[t], 0, inner(t, c, nu, nc - 1))),
            ],
            out_specs=pl.BlockSpec((rt, ce), lambda t, c, nu, te, tb, tr: (tb[t], inner(t, c, nu, nc - 1))),
        ),
        out_shape=jax.ShapeDtypeStruct((p_rows, de), BF16),
        compiler_params=_params(("arbitrary", "arbitrary"), _nbytes((rt, half), U32), 2 * _nbytes((d, ce), F32),
                                _nbytes((rt, ce), BF16)),
        name="moe_gate_up",
    )(n_used, tile_e, tile_blk, tile_rows, xs, w_gate, w_up)

    return pl.pallas_call(
        _moe_down_kernel,
        grid_spec=pltpu.PrefetchScalarGridSpec(
            num_scalar_prefetch=4,
            grid=(n_tiles, nn),
            in_specs=[
                pl.BlockSpec((rt, de), lambda t, j, nu, te, tb, tr: (tb[t], 0)),
                pl.BlockSpec((1, de, tn), lambda t, j, nu, te, tb, tr: (te[t], 0, inner(t, j, nu, nn - 1))),
                pl.BlockSpec((rt, 1), lambda t, j, nu, te, tb, tr: (tb[t], 0)),
            ],
            out_specs=pl.BlockSpec((rt, tn), lambda t, j, nu, te, tb, tr: (tb[t], inner(t, j, nu, nn - 1))),
        ),
        out_shape=jax.ShapeDtypeStruct((p_rows, d), F32),
        compiler_params=_params(("arbitrary", "arbitrary"), _nbytes((rt, de), BF16), _nbytes((de, tn), F32),
                                _nbytes((rt, tn), F32), _nbytes((rt, LANE), F32)),
        name="moe_down",
    )(n_used, tile_e, tile_blk, tile_rows, h_mid, w_down, slot_w.reshape(p_rows, 1))


def _moe_combine_kernel(pos_ref, x_ref, y_ref, o_ref, buf, sem, *, tm, top_k):
    def row_copy(a):
        return pltpu.make_async_copy(y_ref.at[pl.ds(pos_ref[0, 0, a], 1)],
                                     buf.at[a % top_k, pl.ds(a // top_k, 1)], sem)

    def start(a, c):
        row_copy(a).start()
        return c

    def wait(a, c):
        row_copy(a).wait()
        return c

    lax.fori_loop(0, tm * top_k, start, 0)
    lax.fori_loop(0, tm * top_k, wait, 0)
    acc = x_ref[...]
    for k in range(top_k):
        acc = acc + buf[k]
    o_ref[...] = acc


def _moe_combine(dest, x2, y, tm, top_k):
    n, d = x2.shape
    return pl.pallas_call(
        functools.partial(_moe_combine_kernel, tm=tm, top_k=top_k),
        grid=(n // tm,),
        in_specs=[
            pl.BlockSpec((1, 1, tm * top_k), lambda i: (i, 0, 0), memory_space=pltpu.SMEM),
            pl.BlockSpec((tm, d), lambda i: (i, 0)),
            pl.BlockSpec(memory_space=pl.ANY),
        ],
        out_specs=pl.BlockSpec((tm, d), lambda i: (i, 0)),
        out_shape=jax.ShapeDtypeStruct((n, d), F32),
        scratch_shapes=[pltpu.VMEM((top_k, tm, d), F32), pltpu.SemaphoreType.DMA(())],
        compiler_params=_params(("arbitrary",), 2 * _nbytes((tm, d), F32), _nbytes((top_k, tm, d), F32) // 2),
        name="moe_combine",
    )(dest.reshape(n // tm, 1, tm * top_k), x2, y)


def _moe_schedule(eid, gate, n_exp, rt, n_tiles):
    n, top_k = eid.shape
    a = n * top_k
    eflat = eid.reshape(a)
    onehot = (eflat[:, None] == jnp.arange(n_exp, dtype=I32)[None, :]).astype(I32)
    csum = jnp.cumsum(onehot, axis=0)
    rank = jnp.sum(csum * onehot, axis=1) - 1
    counts = csum[-1]
    nt = (counts + rt - 1) // rt
    tcum = jnp.cumsum(nt)
    tstart = tcum - nt
    n_used = tcum[-1]
    dest = tstart[eflat] * rt + rank
    tid = jnp.arange(n_tiles, dtype=I32)
    tile_blk = jnp.minimum(tid, n_used - 1)
    tile_e = jnp.minimum(jnp.searchsorted(tcum, tile_blk, side="right"), n_exp - 1).astype(I32)
    tile_rows = jnp.where(tid < n_used, jnp.clip(counts[tile_e] - (tile_blk - tstart[tile_e]) * rt, 0, rt), 0)
    tok = jnp.arange(a, dtype=I32) // top_k
    slot_tok = jnp.zeros((n_tiles * rt,), I32).at[dest].set(tok)
    slot_w = jnp.zeros((n_tiles * rt,), F32).at[dest].set(gate.reshape(a))
    return (n_used.reshape(1).astype(I32), tile_e, tile_blk.astype(I32), tile_rows.astype(I32),
            slot_tok, slot_w, dest.astype(I32))


def _pick(n, pref):
    t = min(n, pref)
    assert n % t == 0
    return t


def kernel(x, mem, g_mix, w_in, g_q_dil, g_k_dil, g_out_dil, g_out_sb, w_out, g_cross, g_mem, w_q_cross, w_kv_cross, g_q_cross, g_k_cross, w_o_cross, g_ffn, w_group, b_group, w_router, b_router, w_exp_gate, w_exp_up, w_exp_down):
    batch, seq, d = x.shape
    n = batch * seq
    n_heads_mix = d // HEAD_DIM
    n_dil = n_heads_mix // 2
    n_sb = n_heads_mix - n_dil
    w_dil, w_sb = n_dil * HEAD_DIM, n_sb * HEAD_DIM
    n_exp, _, d_exp = w_exp_gate.shape[1:]
    top_k = 2
    scale = 1.0 / math.sqrt(HEAD_DIM)
    slopes = jnp.exp2(-ALIBI_MAX_BIAS * jnp.arange(1, n_dil + 1, dtype=F32) / n_dil)

    tm = _pick(n, 512)
    rt = -(-(5 * n * top_k) // (4 * n_exp * MOE_ROW_CHUNK)) * MOE_ROW_CHUNK
    n_tiles = -(-n * top_k // rt) + n_exp
    ce = d_exp // 3 if (d_exp // 3) % LANE == 0 else d_exp

    xf = x.reshape(n, d)
    for l in range(g_mix.shape[0]):
        h = _rmsnorm_cast(xf, g_mix[l], tm)
        w_in_b = w_in[l].astype(BF16)
        gain_dil = jnp.concatenate([jnp.tile(g_q_dil[l] * scale, n_dil), jnp.tile(g_k_dil[l], n_dil),
                                    jnp.ones((w_dil,), F32)])
        gain_sb = jnp.concatenate([jnp.full((w_sb,), scale, F32), jnp.ones((2 * w_sb,), F32)])
        tn = _pick(w_dil, 1024)
        qkv_dil = _proj(h, w_in_b, gain_dil, 0, 3 * w_dil, 2 * w_dil, F32, tm, tn, "in_proj_dil")
        qkv_sb = _proj(h, w_in_b, gain_sb, 3 * w_dil, 3 * w_sb, 0, BF16, tm, tn, "in_proj_sb")
        o_dil = _dilated(qkv_dil, slopes, batch, seq, n_dil, _pick(seq, 2048))
        o_sb = _stick_breaking(qkv_sb, batch, seq, n_sb, _pick(seq, 256))
        x1 = _outproj(o_dil, o_sb, g_out_dil[l], g_out_sb[l], w_out[l].astype(BF16), xf, tm, _pick(d, 512))

        kc, vc = _mem_kv(mem, g_mem[l], w_kv_cross[l].astype(BF16), g_k_cross[l])
        wr = jnp.concatenate([w_router[l], w_group[l],
                              jnp.zeros((d, LANE - n_exp - N_GROUPS), F32)], axis=1)
        br = jnp.concatenate([b_router[l].reshape(-1), b_group[l],
                              jnp.zeros((LANE - n_exp - N_GROUPS,), F32)]).reshape(1, LANE)
        x2, hp, eid, gate = _cross_router(x1, g_cross[l], w_q_cross[l].astype(BF16), g_q_cross[l] * scale, kc, vc,
                                          w_o_cross[l].astype(BF16), g_ffn[l], wr, br, seq, _pick(seq, 256))

        n_used, tile_e, tile_blk, tile_rows, slot_tok, slot_w, dest = _moe_schedule(
            eid[:, :top_k], gate[:, :top_k], n_exp, rt, n_tiles)
        xs = _moe_gather(n_used, slot_tok, hp, n_tiles, rt)
        y = _moe_experts(n_used, tile_e, tile_blk, tile_rows, xs, slot_w, w_exp_gate[l], w_exp_up[l],
                         w_exp_down[l], n_tiles, rt, ce, _pick(d, 1024))
        xf = _moe_combine(dest, x2, y, _pick(n, 128), top_k)
    return xf.reshape(batch, seq, d)
```

```python
import functools
import math

import jax
import jax.numpy as jnp
from jax import lax
from jax.experimental import pallas as pl
from jax.experimental.pallas import tpu as pltpu

F32 = jnp.float32
BF16 = jnp.bfloat16
I32 = jnp.int32
U32 = jnp.uint32

LANE = 128
HEAD_DIM = 128
EPS = 1e-6
DIL_PAIRS = ((128, 1), (512, 4), (2048, 16))
DIL_STEPS = 128
DIL_UNROLL = 8
ALIBI_MAX_BIAS = 8.0
N_HEADS_CROSS = 4
N_GROUPS = 8
EXPERTS_PER_GROUP = 8
MOE_ROW_CHUNK = 128
NEG_BIG = -1e30
VMEM_LIMIT_CAP = 60000 * 1024
VMEM_INTERNAL = 12 * 1024 * 1024


def _vmem_limit(*block_bytes):
    return int(min(VMEM_LIMIT_CAP, 2 * sum(block_bytes) + VMEM_INTERNAL))


def _nbytes(shape, dtype):
    return math.prod(shape) * jnp.dtype(dtype).itemsize


def _params(sem, *block_bytes):
    return pltpu.CompilerParams(dimension_semantics=sem, vmem_limit_bytes=_vmem_limit(*block_bytes))


def _rms(x, g):
    return x * lax.rsqrt(jnp.mean(x * x, axis=-1, keepdims=True) + EPS) * g


def _rmsnorm_cast_kernel(x_ref, g_ref, o_ref):
    o_ref[...] = _rms(x_ref[...], g_ref[...]).astype(o_ref.dtype)


def _rmsnorm_cast(x, g, tm):
    n, d = x.shape
    return pl.pallas_call(
        _rmsnorm_cast_kernel,
        grid=(n // tm,),
        in_specs=[pl.BlockSpec((tm, d), lambda i: (i, 0)), pl.BlockSpec((1, d), lambda i: (0, 0))],
        out_specs=pl.BlockSpec((tm, d), lambda i: (i, 0)),
        out_shape=jax.ShapeDtypeStruct((n, d), BF16),
        compiler_params=_params(("arbitrary",), _nbytes((tm, d), F32), _nbytes((tm, d), BF16)),
        name="rmsnorm_cast",
    )(x, g.reshape(1, d))


def _proj_kernel(a_ref, w_ref, g_ref, o_ref, *, n_norm_tiles):
    j = pl.program_id(1)
    acc = jnp.dot(a_ref[...], w_ref[...], preferred_element_type=F32)
    tn = acc.shape[1]

    @pl.when(j < n_norm_tiles)
    def _():
        for h in range(tn // HEAD_DIM):
            sl = slice(h * HEAD_DIM, (h + 1) * HEAD_DIM)
            o_ref[:, sl] = _rms(acc[:, sl], g_ref[:, sl]).astype(o_ref.dtype)

    @pl.when(j >= n_norm_tiles)
    def _():
        o_ref[...] = (acc * g_ref[...]).astype(o_ref.dtype)


def _proj(a, w, gain, col0, ncols, norm_cols, out_dtype, tm, tn, name):
    n, k = a.shape
    assert col0 % tn == 0 and ncols % tn == 0 and norm_cols % tn == 0 and n % tm == 0
    jb = col0 // tn
    return pl.pallas_call(
        functools.partial(_proj_kernel, n_norm_tiles=norm_cols // tn),
        grid=(n // tm, ncols // tn),
        in_specs=[
            pl.BlockSpec((tm, k), lambda i, j: (i, 0)),
            pl.BlockSpec((k, tn), lambda i, j: (0, j + jb)),
            pl.BlockSpec((1, tn), lambda i, j: (0, j)),
        ],
        out_specs=pl.BlockSpec((tm, tn), lambda i, j: (i, j)),
        out_shape=jax.ShapeDtypeStruct((n, ncols), out_dtype),
        compiler_params=_params(("arbitrary", "arbitrary"), _nbytes((tm, k), BF16), _nbytes((k, tn), BF16),
                                _nbytes((tm, tn), F32)),
        name=name,
    )(a, w, gain.reshape(1, ncols))


def _dil_kernel(slope_ref, q_ref, k_ref, v_ref, o_ref, ob0, ob1, ob2, ls0, ls1, ls2, *, tq):
    h = pl.program_id(1)
    t0 = pl.program_id(2) * tq
    slope = slope_ref[h]
    J = DIL_STEPS
    qi = lax.broadcasted_iota(I32, (J, 2 * J), 0)
    ki = lax.broadcasted_iota(I32, (J, 2 * J), 1)
    dist = J + qi - ki
    valid = (dist >= 0) & (dist <= J)
    distf = dist.astype(F32)
    in_cur = ki >= J

    for (window, d), ob, ls in zip(DIL_PAIRS, (ob0, ob1, ob2), (ls0, ls1, ls2)):
        assert window // d == J and tq % (d * J) == 0
        bias = jnp.where(valid, (-slope * d) * distf, NEG_BIG)
        shift = d.bit_length() - 1

        def ld(ref, start, d=d):
            if d == 1:
                return ref[pl.ds(start, J), :]
            return ref[pl.ds(start, J, stride=d), :]

        def unit(u, carry, d=d, shift=shift, bias=bias, ob=ob, ls=ls, ld=ld):
            off = (u & (d - 1)) + (u >> shift) * (d * J)
            cur0 = t0 + off
            prev0 = cur0 - d * J
            has_prev = prev0 >= 0
            prev0 = jnp.maximum(prev0, 0)
            qs = ld(q_ref, off).astype(BF16)
            kcat = jnp.concatenate([ld(k_ref, prev0), ld(k_ref, cur0)], axis=0).astype(BF16)
            vcat = jnp.concatenate([ld(v_ref, prev0), ld(v_ref, cur0)], axis=0).astype(BF16)
            s = lax.dot_general(qs, kcat, (((1,), (1,)), ((), ())), preferred_element_type=F32) + bias
            s = jnp.where(jnp.logical_or(in_cur, has_prev), s, NEG_BIG)
            m = jnp.max(s, axis=1, keepdims=True)
            p = jnp.exp(s - m)
            l = jnp.sum(p, axis=1, keepdims=True)
            o = jnp.dot(p.astype(BF16), vcat, preferred_element_type=F32) / l
            lse = jnp.broadcast_to(m + jnp.log(l), (J, HEAD_DIM))
            if d == 1:
                ob[pl.ds(off, J), :] = o
                ls[pl.ds(off, J), :] = lse
            else:
                ob[pl.ds(off, J, stride=d), :] = o
                ls[pl.ds(off, J, stride=d), :] = lse
            return carry

        lax.fori_loop(0, tq // J, unit, 0, unroll=DIL_UNROLL)

    l0, l1, l2 = ls0[...], ls1[...], ls2[...]
    mx = jnp.maximum(jnp.maximum(l0, l1), l2)
    w0, w1, w2 = jnp.exp(l0 - mx), jnp.exp(l1 - mx), jnp.exp(l2 - mx)
    o_ref[...] = (w0 * ob0[...] + w1 * ob1[...] + w2 * ob2[...]) / (w0 + w1 + w2)


def _dilated(qkv, slopes, batch, seq, n_heads, tq):
    assert seq % tq == 0
    n = batch * seq
    nq = seq // tq
    blk = (tq, HEAD_DIM)
    full = (seq, HEAD_DIM)
    scr = [pltpu.VMEM(blk, F32)] * 6
    return pl.pallas_call(
        functools.partial(_dil_kernel, tq=tq),
        grid=(batch, n_heads, nq),
        in_specs=[
            pl.BlockSpec(memory_space=pltpu.SMEM),
            pl.BlockSpec(blk, lambda b, h, c: (b * nq + c, h)),
            pl.BlockSpec(full, lambda b, h, c: (b, n_heads + h)),
            pl.BlockSpec(full, lambda b, h, c: (b, 2 * n_heads + h)),
        ],
        out_specs=pl.BlockSpec(blk, lambda b, h, c: (b * nq + c, h)),
        out_shape=jax.ShapeDtypeStruct((n, n_heads * HEAD_DIM), F32),
        scratch_shapes=scr,
        compiler_params=_params(("arbitrary",) * 3, 2 * _nbytes(full, F32), 5 * _nbytes(blk, F32)),
        name="dilated_attn",
    )(slopes, qkv, qkv, qkv)


def _sb_kernel(q_ref, k_ref, v_ref, o_ref, *, tq, sub):
    n_chain = tq // sub
    kb0 = pl.program_id(2) * n_chain
    row = lax.broadcasted_iota(I32, (sub, sub), 0)
    col = lax.broadcasted_iota(I32, (sub, sub), 1)
    causal = col < row
    after = (row > col).astype(BF16)
    after2 = jnp.concatenate([after, after], axis=0)
    sign = jnp.uint32(0x80000000)

    def block(c, kb, drop, acc, diag, valid):
        start = pl.multiple_of(kb * sub, sub)
        z2 = lax.dot_general(q_ref[c * sub:(c + 1) * sub, :], k_ref[pl.ds(start, sub), :],
                             (((1,), (1,)), ((), ())), preferred_element_type=F32)
        neg_abs = pltpu.bitcast(pltpu.bitcast(z2, U32) | sign, F32)
        sp2 = jnp.maximum(z2, 0.0) + jnp.log2(1.0 + jnp.exp2(neg_abs))
        log_beta2 = z2 - sp2
        if diag:
            sp2 = jnp.where(causal, sp2, 0.0)
        hi = sp2.astype(BF16)
        lo = (sp2 - hi.astype(F32)).astype(BF16)
        later = jnp.dot(jnp.concatenate([hi, lo], axis=1), after2, preferred_element_type=F32)
        a = jnp.exp2(log_beta2 - later)
        if diag:
            a = jnp.where(causal, a, 0.0)
        scale = jnp.exp2(-drop)
        new_drop = drop + later[:, 0:1] + sp2[:, 0:1]
        if valid is not None:
            scale = jnp.where(valid, scale, 0.0)
            new_drop = jnp.where(valid, new_drop, drop)
        pv = jnp.dot(a.astype(BF16), v_ref[pl.ds(start, sub), :], preferred_element_type=F32)
        return new_drop, acc + scale * pv

    def any_alive(drops):
        m = jnp.max(jnp.exp2(-drops[0]))
        for d in drops[1:]:
            m = jnp.maximum(m, jnp.max(jnp.exp2(-d)))
        return (m > 0.0).astype(I32)

    drops, accs = [], []
    for c in range(n_chain):
        d, a = block(c, kb0 + c, jnp.zeros((sub, 1), F32), jnp.zeros((sub, HEAD_DIM), F32), True, None)
        drops.append(d)
        accs.append(a)

    def cond(st):
        return jnp.logical_and(st[0] <= kb0 + n_chain - 1, st[1] > 0)

    def body(st):
        g, _, drops, accs = st
        new_d, new_a = [], []
        for c in range(n_chain):
            kb = kb0 + c - g
            d, a = block(c, jnp.maximum(kb, 0), drops[c], accs[c], False, kb >= 0)
            new_d.append(d)
            new_a.append(a)
        return g + 1, any_alive(new_d), tuple(new_d), tuple(new_a)

    _, _, _, accs = lax.while_loop(cond, body, (jnp.int32(1), any_alive(drops), tuple(drops), tuple(accs)))
    for c in range(n_chain):
        o_ref[c * sub:(c + 1) * sub, :] = accs[c]


def _stick_breaking(qkv, batch, seq, n_heads, tq, sub):
    assert seq % tq == 0 and tq % sub == 0
    n = batch * seq
    nq = seq // tq
    blk = (tq, HEAD_DIM)
    full = (seq, HEAD_DIM)
    return pl.pallas_call(
        functools.partial(_sb_kernel, tq=tq, sub=sub),
        grid=(batch, n_heads, nq),
        in_specs=[
            pl.BlockSpec(blk, lambda b, h, i: (b * nq + i, h)),
            pl.BlockSpec(full, lambda b, h, i: (b, n_heads + h)),
            pl.BlockSpec(full, lambda b, h, i: (b, 2 * n_heads + h)),
        ],
        out_specs=pl.BlockSpec(blk, lambda b, h, i: (b * nq + i, h)),
        out_shape=jax.ShapeDtypeStruct((n, n_heads * HEAD_DIM), F32),
        compiler_params=_params(("arbitrary",) * 3, 2 * _nbytes(full, BF16), 2 * _nbytes(blk, F32)),
        name="stick_breaking",
    )(qkv, qkv, qkv)


def _outproj_kernel(od_ref, os_ref, gd_ref, gs_ref, w_ref, x_ref, o_ref, m_scr):
    wd = od_ref.shape[1]

    @pl.when(pl.program_id(1) == 0)
    def _():
        m_scr[:, :wd] = _rms(od_ref[...], gd_ref[...]).astype(BF16)
        m_scr[:, wd:] = _rms(os_ref[...], gs_ref[...]).astype(BF16)

    o_ref[...] = x_ref[...] + jnp.dot(m_scr[...], w_ref[...], preferred_element_type=F32)


def _outproj(o_dil, o_sb, g_dil, g_sb, w, x, tm, tn):
    n, wd = o_dil.shape
    ws = o_sb.shape[1]
    d = x.shape[1]
    return pl.pallas_call(
        _outproj_kernel,
        grid=(n // tm, d // tn),
        in_specs=[
            pl.BlockSpec((tm, wd), lambda i, j: (i, 0)),
            pl.BlockSpec((tm, ws), lambda i, j: (i, 0)),
            pl.BlockSpec((1, wd), lambda i, j: (0, 0)),
            pl.BlockSpec((1, ws), lambda i, j: (0, 0)),
            pl.BlockSpec((wd + ws, tn), lambda i, j: (0, j)),
            pl.BlockSpec((tm, tn), lambda i, j: (i, j)),
        ],
        out_specs=pl.BlockSpec((tm, tn), lambda i, j: (i, j)),
        out_shape=jax.ShapeDtypeStruct((n, d), F32),
        scratch_shapes=[pltpu.VMEM((tm, wd + ws), BF16)],
        compiler_params=_params(("arbitrary", "arbitrary"), _nbytes((tm, wd + ws), F32),
                                _nbytes((wd + ws, tn), BF16), 2 * _nbytes((tm, tn), F32),
                                _nbytes((tm, wd + ws), BF16) // 2),
        name="out_proj",
    )(o_dil, o_sb, g_dil.reshape(1, wd), g_sb.reshape(1, ws), w, x)


def _mem_kv_kernel(m_ref, gm_ref, w_ref, gk_ref, k_ref, v_ref):
    hm = _rms(m_ref[0], gm_ref[...]).astype(BF16)
    kv = jnp.dot(hm, w_ref[...], preferred_element_type=F32)
    wc = k_ref.shape[2]
    for h in range(wc // HEAD_DIM):
        sl = slice(h * HEAD_DIM, (h + 1) * HEAD_DIM)
        k_ref[0, :, sl] = _rms(kv[:, sl], gk_ref[...]).astype(BF16)
    v_ref[0] = kv[:, wc:].astype(BF16)


def _mem_kv(mem, g_mem, w_kv, g_k):
    b, m, d = mem.shape
    wc = w_kv.shape[1] // 2
    out = jax.ShapeDtypeStruct((b, m, wc), BF16)
    return pl.pallas_call(
        _mem_kv_kernel,
        grid=(b,),
        in_specs=[
            pl.BlockSpec((1, m, d), lambda i: (i, 0, 0)),
            pl.BlockSpec((1, d), lambda i: (0, 0)),
            pl.BlockSpec((d, 2 * wc), lambda i: (0, 0)),
            pl.BlockSpec((1, HEAD_DIM), lambda i: (0, 0)),
        ],
        out_specs=[pl.BlockSpec((1, m, wc), lambda i: (i, 0, 0))] * 2,
        out_shape=[out, out],
        compiler_params=_params(("arbitrary",), _nbytes((m, d), F32), _nbytes((d, 2 * wc), BF16)),
        name="mem_kv",
    )(mem, g_mem.reshape(1, d), w_kv, g_k.reshape(1, HEAD_DIM))


def _pack_bf16_pairs(x):
    half = x.shape[1] // 2
    lo = pltpu.bitcast(x[:, :half].astype(BF16).astype(F32), U32)
    hi = pltpu.bitcast(x[:, half:].astype(BF16).astype(F32), U32)
    return hi | (lo >> 16)


def _unpack_bf16_pairs(p):
    lo = pltpu.bitcast(p << 16, F32)
    hi = pltpu.bitcast(p & jnp.uint32(0xFFFF0000), F32)
    return jnp.concatenate([lo, hi], axis=1).astype(BF16)


def _cross_router_kernel(x1_ref, gc_ref, wq_ref, gq_ref, kc_ref, vc_ref, wo_ref, gf_ref, wr_ref, br_ref,
                         x2_ref, hp_ref, eid_ref, gate_ref):
    x1 = x1_ref[...]
    h = _rms(x1, gc_ref[...]).astype(BF16)
    q = jnp.dot(h, wq_ref[...], preferred_element_type=F32)
    heads = []
    for hd in range(q.shape[1] // HEAD_DIM):
        sl = slice(hd * HEAD_DIM, (hd + 1) * HEAD_DIM)
        qh = _rms(q[:, sl], gq_ref[...]).astype(BF16)
        s = lax.dot_general(qh, kc_ref[0, :, sl], (((1,), (1,)), ((), ())), preferred_element_type=F32)
        p = jnp.exp(s - jnp.max(s, axis=1, keepdims=True))
        l = jnp.sum(p, axis=1, keepdims=True)
        heads.append(jnp.dot(p.astype(BF16), vc_ref[0, :, sl], preferred_element_type=F32) / l)
    o = jnp.concatenate(heads, axis=1).astype(BF16)
    x2 = x1 + jnp.dot(o, wo_ref[...], preferred_element_type=F32)
    x2_ref[...] = x2

    hf = _rms(x2, gf_ref[...])
    packed = _pack_bf16_pairs(hf)
    tm = packed.shape[0]
    nb = packed.shape[1] // LANE
    for j in range(nb):
        hp_ref[pl.ds(j, tm, stride=nb), :] = packed[:, j * LANE:(j + 1) * LANE]

    n_exp = N_GROUPS * EXPERTS_PER_GROUP
    lg = jnp.dot(hf, wr_ref[...], preferred_element_type=F32, precision=lax.Precision.HIGHEST) + br_ref[...]
    lane = lax.broadcasted_iota(I32, lg.shape, 1)
    lanef = lane.astype(F32)
    ninf = -jnp.inf
    is_g = (lane >= n_exp) & (lane < n_exp + N_GROUPS)
    gl = jnp.where(is_g, lg, ninf)
    gmax = jnp.max(gl, axis=1, keepdims=True)
    gidx = jnp.min(jnp.where(gl == gmax, lanef, 1e9), axis=1, keepdims=True) - n_exp
    g_gate = 1.0 / jnp.sum(jnp.exp(gl - gmax), axis=1, keepdims=True)
    in_grp = (lane < n_exp) & ((lane // EXPERTS_PER_GROUP).astype(F32) == gidx)
    el = jnp.where(in_grp, lg, ninf)
    m1 = jnp.max(el, axis=1, keepdims=True)
    i1 = jnp.min(jnp.where(el == m1, lanef, 1e9), axis=1, keepdims=True)
    el2 = jnp.where(lanef == i1, ninf, el)
    m2 = jnp.max(el2, axis=1, keepdims=True)
    i2 = jnp.min(jnp.where(el2 == m2, lanef, 1e9), axis=1, keepdims=True)
    p2 = jnp.exp(m2 - m1)
    den = 1.0 + p2
    eid_ref[...] = jnp.where(lane == 0, i1, jnp.where(lane == 1, i2, 0.0)).astype(I32)
    gate_ref[...] = jnp.where(lane == 0, g_gate / den, jnp.where(lane == 1, g_gate * p2 / den, 0.0))


def _cross_router(x1, g_cross, wq, gq, kc, vc, wo, g_ffn, wr, br, seq, tm):
    n, d = x1.shape
    wc = wq.shape[1]
    m = kc.shape[1]
    tiles_per_batch = seq // tm
    const = lambda i: (0, 0)
    return pl.pallas_call(
        _cross_router_kernel,
        grid=(n // tm,),
        in_specs=[
            pl.BlockSpec((tm, d), lambda i: (i, 0)),
            pl.BlockSpec((1, d), const),
            pl.BlockSpec((d, wc), const),
            pl.BlockSpec((1, HEAD_DIM), const),
            pl.BlockSpec((1, m, wc), lambda i: (i // tiles_per_batch, 0, 0)),
            pl.BlockSpec((1, m, wc), lambda i: (i // tiles_per_batch, 0, 0)),
            pl.BlockSpec((wc, d), const),
            pl.BlockSpec((1, d), const),
            pl.BlockSpec((d, LANE), const),
            pl.BlockSpec((1, LANE), const),
        ],
        out_specs=[
            pl.BlockSpec((tm, d), lambda i: (i, 0)),
            pl.BlockSpec((tm * (d // 2 // LANE), LANE), lambda i: (i, 0)),
            pl.BlockSpec((tm, LANE), lambda i: (i, 0)),
            pl.BlockSpec((tm, LANE), lambda i: (i, 0)),
        ],
        out_shape=[
            jax.ShapeDtypeStruct((n, d), F32),
            jax.ShapeDtypeStruct((n * (d // 2 // LANE), LANE), U32),
            jax.ShapeDtypeStruct((n, LANE), I32),
            jax.ShapeDtypeStruct((n, LANE), F32),
        ],
        compiler_params=_params(("arbitrary",), 2 * _nbytes((tm, d), F32), _nbytes((tm, d // 2), U32),
                                2 * _nbytes((d, wc), BF16), _nbytes((d, LANE), F32)),
        name="cross_router",
    )(x1, g_cross.reshape(1, d), wq, gq.reshape(1, HEAD_DIM), kc, vc, wo, g_ffn.reshape(1, d), wr, br)


def _moe_gather_kernel(nused_ref, tblk_ref, tok_ref, src_ref, o_ref, buf, sem, *, rt, nb):
    t = pl.program_id(0)

    def row_copy(r):
        return pltpu.make_async_copy(src_ref.at[pl.ds(tok_ref[0, 0, r] * nb, nb)], buf.at[pl.ds(r * nb, nb)], sem)

    @pl.when(t < nused_ref[0])
    def _():
        def start(r, c):
            row_copy(r).start()
            return c

        def wait(r, c):
            row_copy(r).wait()
            return c

        lax.fori_loop(0, rt, start, 0)
        lax.fori_loop(0, rt, wait, 0)
        for j in range(nb):
            o_ref[:, j * LANE:(j + 1) * LANE] = buf[pl.ds(j, rt, stride=nb), :]


def _moe_gather(n_used, tile_blk, slot_tok, hp, n, n_tiles, rt):
    nb = hp.shape[0] // n
    return pl.pallas_call(
        functools.partial(_moe_gather_kernel, rt=rt, nb=nb),
        grid_spec=pltpu.PrefetchScalarGridSpec(
            num_scalar_prefetch=2,
            grid=(n_tiles,),
            in_specs=[
                pl.BlockSpec((1, 1, rt), lambda t, nu, tb: (t, 0, 0), memory_space=pltpu.SMEM),
                pl.BlockSpec(memory_space=pl.ANY),
            ],
            out_specs=pl.BlockSpec((rt, nb * LANE), lambda t, nu, tb: (tb[t], 0)),
            scratch_shapes=[pltpu.VMEM((rt * nb, LANE), U32), pltpu.SemaphoreType.DMA(())],
        ),
        out_shape=jax.ShapeDtypeStruct((n_tiles * rt, nb * LANE), U32),
        compiler_params=_params(("arbitrary",), _nbytes((rt, nb * LANE), U32), _nbytes((rt * nb, LANE), U32) // 2),
        name="moe_gather",
    )(n_used, tile_blk, slot_tok.reshape(n_tiles, 1, rt), hp)


def _moe_up_kernel(nused_ref, te_ref, tblk_ref, trows_ref, xs_ref, wg_ref, wu_ref, h_ref):
    t = pl.program_id(0)

    @pl.when(t < nused_ref[0])
    def _():
        h_ref[...] = jnp.zeros_like(h_ref)
        wg = wg_ref[0].astype(BF16)
        wu = wu_ref[0].astype(BF16)

        def chunk(ch, c):
            r0 = pl.multiple_of(ch * MOE_ROW_CHUNK, MOE_ROW_CHUNK)
            x = _unpack_bf16_pairs(xs_ref[pl.ds(r0, MOE_ROW_CHUNK), :])
            g = jnp.dot(x, wg, preferred_element_type=F32)
            u = jnp.dot(x, wu, preferred_element_type=F32)
            h_ref[pl.ds(r0, MOE_ROW_CHUNK), :] = (g * jax.nn.sigmoid(g) * u).astype(BF16)
            return c

        lax.fori_loop(0, pl.cdiv(trows_ref[t], MOE_ROW_CHUNK), chunk, 0)


def _moe_down_kernel(nused_ref, te_ref, tblk_ref, trows_ref, h_ref, wd_ref, sw_ref, y_ref):
    t = pl.program_id(0)

    @pl.when(t < nused_ref[0])
    def _():
        y_ref[...] = jnp.zeros_like(y_ref)
        wd = wd_ref[0].astype(BF16)

        def chunk(ch, c):
            r0 = pl.multiple_of(ch * MOE_ROW_CHUNK, MOE_ROW_CHUNK)
            rows = pl.ds(r0, MOE_ROW_CHUNK)
            y_ref[rows, :] = jnp.dot(h_ref[rows, :], wd, preferred_element_type=F32) * sw_ref[rows, :]
            return c

        lax.fori_loop(0, pl.cdiv(trows_ref[t], MOE_ROW_CHUNK), chunk, 0)


def _moe_experts(n_used, tile_e, tile_blk, tile_rows, xs, slot_w, w_gate, w_up, w_down, n_tiles, rt, ce, tn):
    n_exp, d, de = w_gate.shape
    half = xs.shape[1]
    p_rows = n_tiles * rt
    nc = de // ce
    nn = d // tn

    def inner(t, c, nused, last):
        return jnp.where(t < nused[0], c, last)

    h_mid = pl.pallas_call(
        _moe_up_kernel,
        grid_spec=pltpu.PrefetchScalarGridSpec(
            num_scalar_prefetch=4,
            grid=(n_tiles, nc),
            in_specs=[
                pl.BlockSpec((rt, half), lambda t, c, nu, te, tb, tr: (tb[t], 0)),
                pl.BlockSpec((1, d, ce), lambda t, c, nu, te, tb, tr: (te[t], 0, inner(t, c, nu, nc - 1))),
                pl.BlockSpec((1, d, ce), lambda t, c, nu, te, tb, tr: (te[t], 0, inner(t, c, nu, nc - 1))),
            ],
            out_specs=pl.BlockSpec((rt, ce), lambda t, c, nu, te, tb, tr: (tb[t], inner(t, c, nu, nc - 1))),
        ),
        out_shape=jax.ShapeDtypeStruct((p_rows, de), BF16),
        compiler_params=_params(("arbitrary", "arbitrary"), _nbytes((rt, half), U32), 2 * _nbytes((d, ce), F32),
                                _nbytes((rt, ce), BF16)),
        name="moe_gate_up",
    )(n_used, tile_e, tile_blk, tile_rows, xs, w_gate, w_up)

    return pl.pallas_call(
        _moe_down_kernel,
        grid_spec=pltpu.PrefetchScalarGridSpec(
            num_scalar_prefetch=4,
            grid=(n_tiles, nn),
            in_specs=[
                pl.BlockSpec((rt, de), lambda t, j, nu, te, tb, tr: (tb[t], 0)),
                pl.BlockSpec((1, de, tn), lambda t, j, nu, te, tb, tr: (te[t], 0, inner(t, j, nu, nn - 1))),
                pl.BlockSpec((rt, 1), lambda t, j, nu, te, tb, tr: (tb[t], 0)),
            ],
            out_specs=pl.BlockSpec((rt, tn), lambda t, j, nu, te, tb, tr: (tb[t], inner(t, j, nu, nn - 1))),
        ),
        out_shape=jax.ShapeDtypeStruct((p_rows, d), F32),
        compiler_params=_params(("arbitrary", "arbitrary"), _nbytes((rt, de), BF16), _nbytes((de, tn), F32),
                                _nbytes((rt, tn), F32), _nbytes((rt, LANE), F32)),
        name="moe_down",
    )(n_used, tile_e, tile_blk, tile_rows, h_mid, w_down, slot_w.reshape(p_rows, 1))


def _moe_combine_kernel(pos_ref, x_ref, y_ref, o_ref, buf, sem, *, tm, top_k):
    def row_copy(a):
        return pltpu.make_async_copy(y_ref.at[pl.ds(pos_ref[0, 0, a], 1)],
                                     buf.at[a % top_k, pl.ds(a // top_k, 1)], sem)

    def start(a, c):
        row_copy(a).start()
        return c

    def wait(a, c):
        row_copy(a).wait()
        return c

    lax.fori_loop(0, tm * top_k, start, 0)
    lax.fori_loop(0, tm * top_k, wait, 0)
    acc = x_ref[...]
    for k in range(top_k):
        acc = acc + buf[k]
    o_ref[...] = acc


def _moe_combine(dest, x2, y, tm, top_k):
    n, d = x2.shape
    return pl.pallas_call(
        functools.partial(_moe_combine_kernel, tm=tm, top_k=top_k),
        grid=(n // tm,),
        in_specs=[
            pl.BlockSpec((1, 1, tm * top_k), lambda i: (i, 0, 0), memory_space=pltpu.SMEM),
            pl.BlockSpec((tm, d), lambda i: (i, 0)),
            pl.BlockSpec(memory_space=pl.ANY),
        ],
        out_specs=pl.BlockSpec((tm, d), lambda i: (i, 0)),
        out_shape=jax.ShapeDtypeStruct((n, d), F32),
        scratch_shapes=[pltpu.VMEM((top_k, tm, d), F32), pltpu.SemaphoreType.DMA(())],
        compiler_params=_params(("arbitrary",), 2 * _nbytes((tm, d), F32), _nbytes((top_k, tm, d), F32) // 2),
        name="moe_combine",
    )(dest.reshape(n // tm, 1, tm * top_k), x2, y)


def _moe_schedule(eid, gate, n_exp, rt, n_tiles):
    n, top_k = eid.shape
    a = n * top_k
    eflat = eid.reshape(a)
    onehot = (eflat[:, None] == jnp.arange(n_exp, dtype=I32)[None, :]).astype(I32)
    csum = jnp.cumsum(onehot, axis=0)
    rank = jnp.sum(csum * onehot, axis=1) - 1
    counts = csum[-1]
    nt = (counts + rt - 1) // rt
    tcum = jnp.cumsum(nt)
    tstart = tcum - nt
    n_used = tcum[-1]
    dest = tstart[eflat] * rt + rank
    tid = jnp.arange(n_tiles, dtype=I32)
    tile_blk = jnp.minimum(tid, n_used - 1)
    tile_e = jnp.minimum(jnp.searchsorted(tcum, tile_blk, side="right"), n_exp - 1).astype(I32)
    tile_rows = jnp.where(tid < n_used, jnp.clip(counts[tile_e] - (tile_blk - tstart[tile_e]) * rt, 0, rt), 0)
    tok = jnp.arange(a, dtype=I32) // top_k
    slot_tok = jnp.zeros((n_tiles * rt,), I32).at[dest].set(tok)
    slot_w = jnp.zeros((n_tiles * rt,), F32).at[dest].set(gate.reshape(a))
    return (n_used.reshape(1).astype(I32), tile_e, tile_blk.astype(I32), tile_rows.astype(I32),
            slot_tok, slot_w, dest.astype(I32))


def _pick(n, pref):
    t = min(n, pref)
    assert n % t == 0
    return t


def kernel(x, mem, g_mix, w_in, g_q_dil, g_k_dil, g_out_dil, g_out_sb, w_out, g_cross, g_mem, w_q_cross, w_kv_cross, g_q_cross, g_k_cross, w_o_cross, g_ffn, w_group, b_group, w_router, b_router, w_exp_gate, w_exp_up, w_exp_down):
    batch, seq, d = x.shape
    n = batch * seq
    n_heads_mix = d // HEAD_DIM
    n_dil = n_heads_mix // 2
    n_sb = n_heads_mix - n_dil
    w_dil, w_sb = n_dil * HEAD_DIM, n_sb * HEAD_DIM
    n_exp, _, d_exp = w_exp_gate.shape[1:]
    top_k = 2
    scale = 1.0 / math.sqrt(HEAD_DIM)
    slopes = jnp.exp2(-ALIBI_MAX_BIAS * jnp.arange(1, n_dil + 1, dtype=F32) / n_dil)

    tm = _pick(n, 512)
    rt = -(-(5 * n * top_k) // (4 * n_exp * MOE_ROW_CHUNK)) * MOE_ROW_CHUNK
    n_tiles = -(-n * top_k // rt) + n_exp
    ce = d_exp // 3 if (d_exp // 3) % LANE == 0 else d_exp

    xf = x.reshape(n, d)
    for l in range(g_mix.shape[0]):
        h = _rmsnorm_cast(xf, g_mix[l], tm)
        w_in_b = w_in[l].astype(BF16)
        gain_dil = jnp.concatenate([jnp.tile(g_q_dil[l] * scale, n_dil), jnp.tile(g_k_dil[l], n_dil),
                                    jnp.ones((w_dil,), F32)])
        gain_sb = jnp.concatenate([jnp.full((w_sb,), scale * math.log2(math.e), F32), jnp.ones((2 * w_sb,), F32)])
        tn = _pick(w_dil, 1024)
        qkv_dil = _proj(h, w_in_b, gain_dil, 0, 3 * w_dil, 2 * w_dil, F32, tm, tn, "in_proj_dil")
        qkv_sb = _proj(h, w_in_b, gain_sb, 3 * w_dil, 3 * w_sb, 0, BF16, tm, tn, "in_proj_sb")
        o_dil = _dilated(qkv_dil, slopes, batch, seq, n_dil, _pick(seq, 2048))
        o_sb = _stick_breaking(qkv_sb, batch, seq, n_sb, _pick(seq, 1024), _pick(seq, 256))
        x1 = _outproj(o_dil, o_sb, g_out_dil[l], g_out_sb[l], w_out[l].astype(BF16), xf, tm, _pick(d, 512))

        kc, vc = _mem_kv(mem, g_mem[l], w_kv_cross[l].astype(BF16), g_k_cross[l])
        wr = jnp.concatenate([w_router[l], w_group[l],
                              jnp.zeros((d, LANE - n_exp - N_GROUPS), F32)], axis=1)
        br = jnp.concatenate([b_router[l].reshape(-1), b_group[l],
                              jnp.zeros((LANE - n_exp - N_GROUPS,), F32)]).reshape(1, LANE)
        x2, hp, eid, gate = _cross_router(x1, g_cross[l], w_q_cross[l].astype(BF16), g_q_cross[l] * scale, kc, vc,
                                          w_o_cross[l].astype(BF16), g_ffn[l], wr, br, seq, _pick(seq, 256))

        n_used, tile_e, tile_blk, tile_rows, slot_tok, slot_w, dest = _moe_schedule(
            eid[:, :top_k], gate[:, :top_k], n_exp, rt, n_tiles)
        xs = _moe_gather(n_used, tile_blk, slot_tok, hp, n, n_tiles, rt)
        y = _moe_experts(n_used, tile_e, tile_blk, tile_rows, xs, slot_w, w_exp_gate[l], w_exp_up[l],
                         w_exp_down[l], n_tiles, rt, ce, _pick(d, 1024))
        xf = _moe_combine(dest, x2, y, _pick(n, 128), top_k)
    return xf.reshape(batch, seq, d)
```

```python
import functools
import math

import jax
import jax.numpy as jnp
from jax import lax
from jax.experimental import pallas as pl
from jax.experimental.pallas import tpu as pltpu

F32 = jnp.float32
BF16 = jnp.bfloat16
I32 = jnp.int32
U32 = jnp.uint32

LANE = 128
HEAD_DIM = 128
EPS = 1e-6
DIL_PAIRS = ((128, 1), (512, 4), (2048, 16))
DIL_STEPS = 128
DIL_UNROLL = 8
ALIBI_MAX_BIAS = 8.0
N_HEADS_CROSS = 4
N_GROUPS = 8
EXPERTS_PER_GROUP = 8
MOE_ROW_CHUNK = 128
NEG_BIG = -1e30
VMEM_LIMIT_CAP = 60000 * 1024
VMEM_INTERNAL = 12 * 1024 * 1024


def _vmem_limit(*block_bytes):
    return int(min(VMEM_LIMIT_CAP, 2 * sum(block_bytes) + VMEM_INTERNAL))


def _nbytes(shape, dtype):
    return math.prod(shape) * jnp.dtype(dtype).itemsize


def _params(sem, *block_bytes):
    return pltpu.CompilerParams(dimension_semantics=sem, vmem_limit_bytes=_vmem_limit(*block_bytes))


def _rms(x, g):
    return x * lax.rsqrt(jnp.mean(x * x, axis=-1, keepdims=True) + EPS) * g


def _rmsnorm_cast_kernel(x_ref, g_ref, o_ref):
    o_ref[...] = _rms(x_ref[...], g_ref[...]).astype(o_ref.dtype)


def _rmsnorm_cast(x, g, tm):
    n, d = x.shape
    return pl.pallas_call(
        _rmsnorm_cast_kernel,
        grid=(n // tm,),
        in_specs=[pl.BlockSpec((tm, d), lambda i: (i, 0)), pl.BlockSpec((1, d), lambda i: (0, 0))],
        out_specs=pl.BlockSpec((tm, d), lambda i: (i, 0)),
        out_shape=jax.ShapeDtypeStruct((n, d), BF16),
        compiler_params=_params(("arbitrary",), _nbytes((tm, d), F32), _nbytes((tm, d), BF16)),
        name="rmsnorm_cast",
    )(x, g.reshape(1, d))


def _proj_kernel(a_ref, w_ref, g_ref, o_ref, *, n_norm_tiles):
    j = pl.program_id(1)
    acc = jnp.dot(a_ref[...], w_ref[...], preferred_element_type=F32)
    tn = acc.shape[1]

    @pl.when(j < n_norm_tiles)
    def _():
        for h in range(tn // HEAD_DIM):
            sl = slice(h * HEAD_DIM, (h + 1) * HEAD_DIM)
            o_ref[:, sl] = _rms(acc[:, sl], g_ref[:, sl]).astype(o_ref.dtype)

    @pl.when(j >= n_norm_tiles)
    def _():
        o_ref[...] = (acc * g_ref[...]).astype(o_ref.dtype)


def _proj(a, w, gain, col0, ncols, norm_cols, out_dtype, tm, tn, name):
    n, k = a.shape
    assert col0 % tn == 0 and ncols % tn == 0 and norm_cols % tn == 0 and n % tm == 0
    jb = col0 // tn
    return pl.pallas_call(
        functools.partial(_proj_kernel, n_norm_tiles=norm_cols // tn),
        grid=(n // tm, ncols // tn),
        in_specs=[
            pl.BlockSpec((tm, k), lambda i, j: (i, 0)),
            pl.BlockSpec((k, tn), lambda i, j: (0, j + jb)),
            pl.BlockSpec((1, tn), lambda i, j: (0, j)),
        ],
        out_specs=pl.BlockSpec((tm, tn), lambda i, j: (i, j)),
        out_shape=jax.ShapeDtypeStruct((n, ncols), out_dtype),
        compiler_params=_params(("arbitrary", "arbitrary"), _nbytes((tm, k), BF16), _nbytes((k, tn), BF16),
                                _nbytes((tm, tn), F32)),
        name=name,
    )(a, w, gain.reshape(1, ncols))


def _dil_kernel(slope_ref, q_ref, k_ref, v_ref, o_ref, ob0, ob1, ob2, ls0, ls1, ls2, *, tq):
    h = pl.program_id(1)
    t0 = pl.program_id(2) * tq
    slope = slope_ref[h]
    J = DIL_STEPS
    qi = lax.broadcasted_iota(I32, (J, 2 * J), 0)
    ki = lax.broadcasted_iota(I32, (J, 2 * J), 1)
    dist = J + qi - ki
    valid = (dist >= 0) & (dist <= J)
    distf = dist.astype(F32)
    in_cur = ki >= J

    for (window, d), ob, ls in zip(DIL_PAIRS, (ob0, ob1, ob2), (ls0, ls1, ls2)):
        assert window // d == J and tq % (d * J) == 0
        bias = jnp.where(valid, (-slope * d) * distf, NEG_BIG)
        shift = d.bit_length() - 1

        def ld(ref, start, d=d):
            if d == 1:
                return ref[pl.ds(start, J), :]
            return ref[pl.ds(start, J, stride=d), :]

        def unit(u, carry, d=d, shift=shift, bias=bias, ob=ob, ls=ls, ld=ld):
            off = (u & (d - 1)) + (u >> shift) * (d * J)
            cur0 = t0 + off
            prev0 = cur0 - d * J
            has_prev = prev0 >= 0
            prev0 = jnp.maximum(prev0, 0)
            qs = ld(q_ref, off).astype(BF16)
            kcat = jnp.concatenate([ld(k_ref, prev0), ld(k_ref, cur0)], axis=0).astype(BF16)
            vcat = jnp.concatenate([ld(v_ref, prev0), ld(v_ref, cur0)], axis=0).astype(BF16)
            s = lax.dot_general(qs, kcat, (((1,), (1,)), ((), ())), preferred_element_type=F32) + bias
            s = jnp.where(jnp.logical_or(in_cur, has_prev), s, NEG_BIG)
            m = jnp.max(s, axis=1, keepdims=True)
            p = jnp.exp(s - m)
            l = jnp.sum(p, axis=1, keepdims=True)
            o = jnp.dot(p.astype(BF16), vcat, preferred_element_type=F32) / l
            lse = jnp.broadcast_to(m + jnp.log(l), (J, HEAD_DIM))
            if d == 1:
                ob[pl.ds(off, J), :] = o
                ls[pl.ds(off, J), :] = lse
            else:
                ob[pl.ds(off, J, stride=d), :] = o
                ls[pl.ds(off, J, stride=d), :] = lse
            return carry

        lax.fori_loop(0, tq // J, unit, 0, unroll=DIL_UNROLL)

    l0, l1, l2 = ls0[...], ls1[...], ls2[...]
    mx = jnp.maximum(jnp.maximum(l0, l1), l2)
    w0, w1, w2 = jnp.exp(l0 - mx), jnp.exp(l1 - mx), jnp.exp(l2 - mx)
    o_ref[...] = (w0 * ob0[...] + w1 * ob1[...] + w2 * ob2[...]) / (w0 + w1 + w2)


def _dilated(qkv, slopes, batch, seq, n_heads, tq):
    assert seq % tq == 0
    n = batch * seq
    nq = seq // tq
    blk = (tq, HEAD_DIM)
    full = (seq, HEAD_DIM)
    scr = [pltpu.VMEM(blk, F32)] * 6
    return pl.pallas_call(
        functools.partial(_dil_kernel, tq=tq),
        grid=(batch, n_heads, nq),
        in_specs=[
            pl.BlockSpec(memory_space=pltpu.SMEM),
            pl.BlockSpec(blk, lambda b, h, c: (b * nq + c, h)),
            pl.BlockSpec(full, lambda b, h, c: (b, n_heads + h)),
            pl.BlockSpec(full, lambda b, h, c: (b, 2 * n_heads + h)),
        ],
        out_specs=pl.BlockSpec(blk, lambda b, h, c: (b * nq + c, h)),
        out_shape=jax.ShapeDtypeStruct((n, n_heads * HEAD_DIM), F32),
        scratch_shapes=scr,
        compiler_params=_params(("arbitrary",) * 3, 2 * _nbytes(full, F32), 5 * _nbytes(blk, F32)),
        name="dilated_attn",
    )(slopes, qkv, qkv, qkv)


def _sb_kernel(q_ref, k_ref, v_ref, o_ref, *, tq, sub):
    n_chain = tq // sub
    kb0 = pl.program_id(2) * n_chain
    row = lax.broadcasted_iota(I32, (sub, sub), 0)
    col = lax.broadcasted_iota(I32, (sub, sub), 1)
    causal = col < row
    after = (row > col).astype(BF16)
    after2 = jnp.concatenate([after, after], axis=0)
    sign = jnp.uint32(0x80000000)

    def block(c, kb, drop, acc, diag, valid):
        start = pl.multiple_of(kb * sub, sub)
        z2 = lax.dot_general(q_ref[c * sub:(c + 1) * sub, :], k_ref[pl.ds(start, sub), :],
                             (((1,), (1,)), ((), ())), preferred_element_type=F32)
        neg_abs = pltpu.bitcast(pltpu.bitcast(z2, U32) | sign, F32)
        sp2 = jnp.maximum(z2, 0.0) + jnp.log2(1.0 + jnp.exp2(neg_abs))
        log_beta2 = z2 - sp2
        if diag:
            sp2 = jnp.where(causal, sp2, 0.0)
        hi = sp2.astype(BF16)
        lo = (sp2 - hi.astype(F32)).astype(BF16)
        later = jnp.dot(jnp.concatenate([hi, lo], axis=1), after2, preferred_element_type=F32)
        a = jnp.exp2(log_beta2 - later)
        if diag:
            a = jnp.where(causal, a, 0.0)
        scale = jnp.exp2(-drop)
        new_drop = drop + later[:, 0:1] + sp2[:, 0:1]
        if valid is not None:
            scale = jnp.where(valid, scale, 0.0)
            new_drop = jnp.where(valid, new_drop, drop)
        pv = jnp.dot(a.astype(BF16), v_ref[pl.ds(start, sub), :], preferred_element_type=F32)
        return new_drop, acc + scale * pv

    def any_alive(drops):
        m = jnp.max(jnp.exp2(-drops[0]))
        for d in drops[1:]:
            m = jnp.maximum(m, jnp.max(jnp.exp2(-d)))
        return (m > 0.0).astype(I32)

    drops, accs = [], []
    for c in range(n_chain):
        d, a = block(c, kb0 + c, jnp.zeros((sub, 1), F32), jnp.zeros((sub, HEAD_DIM), F32), True, None)
        drops.append(d)
        accs.append(a)

    def cond(st):
        return jnp.logical_and(st[0] <= kb0 + n_chain - 1, st[1] > 0)

    def body(st):
        g, _, drops, accs = st
        new_d, new_a = [], []
        for c in range(n_chain):
            kb = kb0 + c - g
            d, a = block(c, jnp.maximum(kb, 0), drops[c], accs[c], False, kb >= 0)
            new_d.append(d)
            new_a.append(a)
        return g + 1, any_alive(new_d), tuple(new_d), tuple(new_a)

    _, _, _, accs = lax.while_loop(cond, body, (jnp.int32(1), any_alive(drops), tuple(drops), tuple(accs)))
    for c in range(n_chain):
        o_ref[c * sub:(c + 1) * sub, :] = accs[c]


def _stick_breaking(qkv, batch, seq, n_heads, tq, sub):
    assert seq % tq == 0 and tq % sub == 0
    n = batch * seq
    nq = seq // tq
    blk = (tq, HEAD_DIM)
    full = (seq, HEAD_DIM)
    return pl.pallas_call(
        functools.partial(_sb_kernel, tq=tq, sub=sub),
        grid=(batch, n_heads, nq),
        in_specs=[
            pl.BlockSpec(blk, lambda b, h, i: (b * nq + i, h)),
            pl.BlockSpec(full, lambda b, h, i: (b, n_heads + h)),
            pl.BlockSpec(full, lambda b, h, i: (b, 2 * n_heads + h)),
        ],
        out_specs=pl.BlockSpec(blk, lambda b, h, i: (b * nq + i, h)),
        out_shape=jax.ShapeDtypeStruct((n, n_heads * HEAD_DIM), F32),
        compiler_params=_params(("arbitrary",) * 3, 2 * _nbytes(full, BF16), 2 * _nbytes(blk, F32)),
        name="stick_breaking",
    )(qkv, qkv, qkv)


def _outproj_kernel(od_ref, os_ref, gd_ref, gs_ref, w_ref, x_ref, o_ref, m_scr):
    wd = od_ref.shape[1]

    @pl.when(pl.program_id(1) == 0)
    def _():
        m_scr[:, :wd] = _rms(od_ref[...], gd_ref[...]).astype(BF16)
        m_scr[:, wd:] = _rms(os_ref[...], gs_ref[...]).astype(BF16)

    o_ref[...] = x_ref[...] + jnp.dot(m_scr[...], w_ref[...], preferred_element_type=F32)


def _outproj(o_dil, o_sb, g_dil, g_sb, w, x, tm, tn):
    n, wd = o_dil.shape
    ws = o_sb.shape[1]
    d = x.shape[1]
    return pl.pallas_call(
        _outproj_kernel,
        grid=(n // tm, d // tn),
        in_specs=[
            pl.BlockSpec((tm, wd), lambda i, j: (i, 0)),
            pl.BlockSpec((tm, ws), lambda i, j: (i, 0)),
            pl.BlockSpec((1, wd), lambda i, j: (0, 0)),
            pl.BlockSpec((1, ws), lambda i, j: (0, 0)),
            pl.BlockSpec((wd + ws, tn), lambda i, j: (0, j)),
            pl.BlockSpec((tm, tn), lambda i, j: (i, j)),
        ],
        out_specs=pl.BlockSpec((tm, tn), lambda i, j: (i, j)),
        out_shape=jax.ShapeDtypeStruct((n, d), F32),
        scratch_shapes=[pltpu.VMEM((tm, wd + ws), BF16)],
        compiler_params=_params(("arbitrary", "arbitrary"), _nbytes((tm, wd + ws), F32),
                                _nbytes((wd + ws, tn), BF16), 2 * _nbytes((tm, tn), F32),
                                _nbytes((tm, wd + ws), BF16) // 2),
        name="out_proj",
    )(o_dil, o_sb, g_dil.reshape(1, wd), g_sb.reshape(1, ws), w, x)


def _mem_kv_kernel(m_ref, gm_ref, w_ref, gk_ref, k_ref, v_ref):
    hm = _rms(m_ref[0], gm_ref[...]).astype(BF16)
    kv = jnp.dot(hm, w_ref[...], preferred_element_type=F32)
    wc = k_ref.shape[2]
    for h in range(wc // HEAD_DIM):
        sl = slice(h * HEAD_DIM, (h + 1) * HEAD_DIM)
        k_ref[0, :, sl] = _rms(kv[:, sl], gk_ref[...]).astype(BF16)
    v_ref[0] = kv[:, wc:].astype(BF16)


def _mem_kv(mem, g_mem, w_kv, g_k):
    b, m, d = mem.shape
    wc = w_kv.shape[1] // 2
    out = jax.ShapeDtypeStruct((b, m, wc), BF16)
    return pl.pallas_call(
        _mem_kv_kernel,
        grid=(b,),
        in_specs=[
            pl.BlockSpec((1, m, d), lambda i: (i, 0, 0)),
            pl.BlockSpec((1, d), lambda i: (0, 0)),
            pl.BlockSpec((d, 2 * wc), lambda i: (0, 0)),
            pl.BlockSpec((1, HEAD_DIM), lambda i: (0, 0)),
        ],
        out_specs=[pl.BlockSpec((1, m, wc), lambda i: (i, 0, 0))] * 2,
        out_shape=[out, out],
        compiler_params=_params(("arbitrary",), _nbytes((m, d), F32), _nbytes((d, 2 * wc), BF16)),
        name="mem_kv",
    )(mem, g_mem.reshape(1, d), w_kv, g_k.reshape(1, HEAD_DIM))


def _pack_bf16_pairs(xr):
    half = xr.shape[1] // 2
    lo = pltpu.bitcast(xr[:, :half], U32)
    hi = pltpu.bitcast(xr[:, half:], U32)
    return hi | (lo >> 16)


def _cross_router_kernel(x1_ref, gc_ref, wq_ref, gq_ref, kc_ref, vc_ref, wo_ref, gf_ref, wrh_ref, wrl_ref, br_ref,
                         x2_ref, hp_ref, eid_ref, gate_ref):
    x1 = x1_ref[...]
    h = _rms(x1, gc_ref[...]).astype(BF16)
    q = jnp.dot(h, wq_ref[...], preferred_element_type=F32)
    heads = []
    for hd in range(q.shape[1] // HEAD_DIM):
        sl = slice(hd * HEAD_DIM, (hd + 1) * HEAD_DIM)
        qh = _rms(q[:, sl], gq_ref[...]).astype(BF16)
        s = lax.dot_general(qh, kc_ref[0, :, sl], (((1,), (1,)), ((), ())), preferred_element_type=F32)
        p = jnp.exp(s - jnp.max(s, axis=1, keepdims=True))
        l = jnp.sum(p, axis=1, keepdims=True)
        heads.append(jnp.dot(p.astype(BF16), vc_ref[0, :, sl], preferred_element_type=F32) / l)
    o = jnp.concatenate(heads, axis=1).astype(BF16)
    x2 = x1 + jnp.dot(o, wo_ref[...], preferred_element_type=F32)
    x2_ref[...] = x2

    hf = _rms(x2, gf_ref[...])
    hf_hi = hf.astype(BF16)
    hf_hi32 = hf_hi.astype(F32)
    hf_lo = (hf - hf_hi32).astype(BF16)
    packed = _pack_bf16_pairs(hf_hi32)
    tm = packed.shape[0]
    nb = packed.shape[1] // LANE
    for j in range(nb):
        hp_ref[pl.ds(j, tm, stride=nb), :] = packed[:, j * LANE:(j + 1) * LANE]

    n_exp = N_GROUPS * EXPERTS_PER_GROUP
    lg = (jnp.dot(hf_hi, wrh_ref[...], preferred_element_type=F32)
          + jnp.dot(hf_hi, wrl_ref[...], preferred_element_type=F32)
          + jnp.dot(hf_lo, wrh_ref[...], preferred_element_type=F32)) + br_ref[...]
    lane = lax.broadcasted_iota(I32, lg.shape, 1)
    lanef = lane.astype(F32)
    ninf = -jnp.inf
    is_g = (lane >= n_exp) & (lane < n_exp + N_GROUPS)
    gl = jnp.where(is_g, lg, ninf)
    gmax = jnp.max(gl, axis=1, keepdims=True)
    gidx = jnp.min(jnp.where(gl == gmax, lanef, 1e9), axis=1, keepdims=True) - n_exp
    g_gate = 1.0 / jnp.sum(jnp.exp(gl - gmax), axis=1, keepdims=True)
    in_grp = (lane < n_exp) & ((lane // EXPERTS_PER_GROUP).astype(F32) == gidx)
    el = jnp.where(in_grp, lg, ninf)
    m1 = jnp.max(el, axis=1, keepdims=True)
    i1 = jnp.min(jnp.where(el == m1, lanef, 1e9), axis=1, keepdims=True)
    el2 = jnp.where(lanef == i1, ninf, el)
    m2 = jnp.max(el2, axis=1, keepdims=True)
    i2 = jnp.min(jnp.where(el2 == m2, lanef, 1e9), axis=1, keepdims=True)
    p2 = jnp.exp(m2 - m1)
    den = 1.0 + p2
    eid_ref[...] = jnp.where(lane == 0, i1, jnp.where(lane == 1, i2, 0.0)).astype(I32)
    gate_ref[...] = jnp.where(lane == 0, g_gate / den, jnp.where(lane == 1, g_gate * p2 / den, 0.0))


def _cross_router(x1, g_cross, wq, gq, kc, vc, wo, g_ffn, wr_hi, wr_lo, br, seq, tm):
    n, d = x1.shape
    wc = wq.shape[1]
    m = kc.shape[1]
    tiles_per_batch = seq // tm
    const = lambda i: (0, 0)
    return pl.pallas_call(
        _cross_router_kernel,
        grid=(n // tm,),
        in_specs=[
            pl.BlockSpec((tm, d), lambda i: (i, 0)),
            pl.BlockSpec((1, d), const),
            pl.BlockSpec((d, wc), const),
            pl.BlockSpec((1, HEAD_DIM), const),
            pl.BlockSpec((1, m, wc), lambda i: (i // tiles_per_batch, 0, 0)),
            pl.BlockSpec((1, m, wc), lambda i: (i // tiles_per_batch, 0, 0)),
            pl.BlockSpec((wc, d), const),
            pl.BlockSpec((1, d), const),
            pl.BlockSpec((d, LANE), const),
            pl.BlockSpec((d, LANE), const),
            pl.BlockSpec((1, LANE), const),
        ],
        out_specs=[
            pl.BlockSpec((tm, d), lambda i: (i, 0)),
            pl.BlockSpec((tm * (d // 2 // LANE), LANE), lambda i: (i, 0)),
            pl.BlockSpec((tm, LANE), lambda i: (i, 0)),
            pl.BlockSpec((tm, LANE), lambda i: (i, 0)),
        ],
        out_shape=[
            jax.ShapeDtypeStruct((n, d), F32),
            jax.ShapeDtypeStruct((n * (d // 2 // LANE), LANE), U32),
            jax.ShapeDtypeStruct((n, LANE), I32),
            jax.ShapeDtypeStruct((n, LANE), F32),
        ],
        compiler_params=_params(("arbitrary",), 2 * _nbytes((tm, d), F32), _nbytes((tm, d // 2), U32),
                                2 * _nbytes((d, wc), BF16), 2 * _nbytes((d, LANE), BF16)),
        name="cross_router",
    )(x1, g_cross.reshape(1, d), wq, gq.reshape(1, HEAD_DIM), kc, vc, wo, g_ffn.reshape(1, d), wr_hi, wr_lo, br)


def _start_rows(copy, n_rows):
    def body(i, c):
        copy(2 * i).start(priority=0)
        copy(2 * i + 1).start(priority=1)
        return c

    lax.fori_loop(0, n_rows // 2, body, 0)


def _wait_rows(copy, n_rows):
    def body(r, c):
        copy(r).wait()
        return c

    lax.fori_loop(0, n_rows, body, 0)


def _moe_gather_kernel(nused_ref, tblk_ref, tok_ref, tok_next_ref, src_ref, o_ref, buf0, buf1, sem0, sem1, *, rt, nb):
    t = pl.program_id(0)
    n_used = nused_ref[0]
    bufs, sems = (buf0, buf1), (sem0, sem1)
    half = nb * LANE
    himask = jnp.uint32(0xFFFF0000)

    def row_copy(tok, slot):
        return lambda r: pltpu.make_async_copy(src_ref.at[pl.ds(tok[0, 0, r] * nb, nb)],
                                               bufs[slot].at[pl.ds(r * nb, nb)], sems[slot])

    for slot in range(2):
        @pl.when(jnp.logical_and(t % 2 == slot, t < n_used))
        def _(slot=slot):
            if slot == 0:
                @pl.when(t == 0)
                def _():
                    _start_rows(row_copy(tok_ref, 0), rt)

            @pl.when(t + 1 < n_used)
            def _():
                _start_rows(row_copy(tok_next_ref, 1 - slot), rt)

            _wait_rows(row_copy(tok_ref, slot), rt)
            for j in range(nb):
                w = bufs[slot][pl.ds(j, rt, stride=nb), :]
                o_ref[:, j * LANE:(j + 1) * LANE] = pltpu.bitcast(w << 16, F32).astype(BF16)
                o_ref[:, half + j * LANE:half + (j + 1) * LANE] = pltpu.bitcast(w & himask, F32).astype(BF16)


def _moe_gather(n_used, tile_blk, slot_tok, hp, n, n_tiles, rt):
    nb = hp.shape[0] // n
    d = 2 * nb * LANE
    tok3 = slot_tok.reshape(n_tiles, 1, rt)
    stage = pltpu.VMEM((rt * nb, LANE), U32)
    return pl.pallas_call(
        functools.partial(_moe_gather_kernel, rt=rt, nb=nb),
        grid_spec=pltpu.PrefetchScalarGridSpec(
            num_scalar_prefetch=2,
            grid=(n_tiles,),
            in_specs=[
                pl.BlockSpec((1, 1, rt), lambda t, nu, tb: (t, 0, 0), memory_space=pltpu.SMEM),
                pl.BlockSpec((1, 1, rt), lambda t, nu, tb: (jnp.minimum(t + 1, n_tiles - 1), 0, 0),
                             memory_space=pltpu.SMEM),
                pl.BlockSpec(memory_space=pl.ANY),
            ],
            out_specs=pl.BlockSpec((rt, d), lambda t, nu, tb: (tb[t], 0)),
            scratch_shapes=[stage, stage, pltpu.SemaphoreType.DMA(()), pltpu.SemaphoreType.DMA(())],
        ),
        out_shape=jax.ShapeDtypeStruct((n_tiles * rt, d), BF16),
        compiler_params=_params(("arbitrary",), _nbytes((rt, d), BF16), _nbytes((rt * nb, LANE), U32)),
        name="moe_gather",
    )(n_used, tile_blk, tok3, tok3, hp)


def _moe_up_kernel(nused_ref, te_ref, tblk_ref, trows_ref, xs_ref, wg_ref, wu_ref, h_ref):
    t = pl.program_id(0)

    @pl.when(t < nused_ref[0])
    def _():
        h_ref[...] = jnp.zeros_like(h_ref)
        wg = wg_ref[0].astype(BF16)
        wu = wu_ref[0].astype(BF16)

        def chunk(ch, c):
            rows = pl.ds(pl.multiple_of(ch * MOE_ROW_CHUNK, MOE_ROW_CHUNK), MOE_ROW_CHUNK)
            x = xs_ref[rows, :]
            g = jnp.dot(x, wg, preferred_element_type=F32)
            u = jnp.dot(x, wu, preferred_element_type=F32)
            h_ref[rows, :] = (g * jax.nn.sigmoid(g) * u).astype(BF16)
            return c

        lax.fori_loop(0, pl.cdiv(trows_ref[t], MOE_ROW_CHUNK), chunk, 0)


def _moe_down_kernel(nused_ref, te_ref, tblk_ref, trows_ref, h_ref, wd_ref, y_ref):
    t = pl.program_id(0)

    @pl.when(t < nused_ref[0])
    def _():
        y_ref[...] = jnp.zeros_like(y_ref)
        wd = wd_ref[0].astype(BF16)

        def chunk(ch, c):
            rows = pl.ds(pl.multiple_of(ch * MOE_ROW_CHUNK, MOE_ROW_CHUNK), MOE_ROW_CHUNK)
            y_ref[rows, :] = jnp.dot(h_ref[rows, :], wd, preferred_element_type=F32)
            return c

        lax.fori_loop(0, pl.cdiv(trows_ref[t], MOE_ROW_CHUNK), chunk, 0)


def _moe_experts(n_used, tile_e, tile_blk, tile_rows, xs, w_gate, w_up, w_down, n_tiles, rt, ce, tn):
    n_exp, d, de = w_gate.shape
    p_rows = n_tiles * rt
    nc = de // ce
    nn = d // tn

    def inner(t, c, nused, last):
        return jnp.where(t < nused[0], c, last)

    h_mid = pl.pallas_call(
        _moe_up_kernel,
        grid_spec=pltpu.PrefetchScalarGridSpec(
            num_scalar_prefetch=4,
            grid=(n_tiles, nc),
            in_specs=[
                pl.BlockSpec((rt, d), lambda t, c, nu, te, tb, tr: (tb[t], 0)),
                pl.BlockSpec((1, d, ce), lambda t, c, nu, te, tb, tr: (te[t], 0, inner(t, c, nu, nc - 1))),
                pl.BlockSpec((1, d, ce), lambda t, c, nu, te, tb, tr: (te[t], 0, inner(t, c, nu, nc - 1))),
            ],
            out_specs=pl.BlockSpec((rt, ce), lambda t, c, nu, te, tb, tr: (tb[t], inner(t, c, nu, nc - 1))),
        ),
        out_shape=jax.ShapeDtypeStruct((p_rows, de), BF16),
        compiler_params=_params(("arbitrary", "arbitrary"), _nbytes((rt, d), BF16), 2 * _nbytes((d, ce), F32),
                                _nbytes((rt, ce), BF16)),
        name="moe_gate_up",
    )(n_used, tile_e, tile_blk, tile_rows, xs, w_gate, w_up)

    return pl.pallas_call(
        _moe_down_kernel,
        grid_spec=pltpu.PrefetchScalarGridSpec(
            num_scalar_prefetch=4,
            grid=(n_tiles, nn),
            in_specs=[
                pl.BlockSpec((rt, de), lambda t, j, nu, te, tb, tr: (tb[t], 0)),
                pl.BlockSpec((1, de, tn), lambda t, j, nu, te, tb, tr: (te[t], 0, inner(t, j, nu, nn - 1))),
            ],
            out_specs=pl.BlockSpec((rt, tn), lambda t, j, nu, te, tb, tr: (tb[t], inner(t, j, nu, nn - 1))),
        ),
        out_shape=jax.ShapeDtypeStruct((p_rows, d), F32),
        compiler_params=_params(("arbitrary", "arbitrary"), _nbytes((rt, de), BF16), _nbytes((de, tn), F32),
                                _nbytes((rt, tn), F32)),
        name="moe_down",
    )(n_used, tile_e, tile_blk, tile_rows, h_mid, w_down)


def _moe_combine_kernel(pos_ref, pos_next_ref, x_ref, g_ref, y_ref, o_ref, buf0, buf1, sem0, sem1, *, tm, top_k):
    i = pl.program_id(0)
    bufs, sems = (buf0, buf1), (sem0, sem1)

    def row_copy(pos, slot):
        return lambda a: pltpu.make_async_copy(y_ref.at[pl.ds(pos[0, 0, a], 1)],
                                               bufs[slot].at[a % top_k, pl.ds(a // top_k, 1)], sems[slot])

    for slot in range(2):
        @pl.when(i % 2 == slot)
        def _(slot=slot):
            if slot == 0:
                @pl.when(i == 0)
                def _():
                    _start_rows(row_copy(pos_ref, 0), tm * top_k)

            @pl.when(i + 1 < pl.num_programs(0))
            def _():
                _start_rows(row_copy(pos_next_ref, 1 - slot), tm * top_k)

            _wait_rows(row_copy(pos_ref, slot), tm * top_k)
            acc = x_ref[...]
            for k in range(top_k):
                acc = acc + g_ref[:, k:k + 1] * bufs[slot][k]
            o_ref[...] = acc


def _moe_combine(dest, x2, gate, y, tm, top_k):
    n, d = x2.shape
    steps = n // tm
    pos3 = dest.reshape(steps, 1, tm * top_k)
    stage = pltpu.VMEM((top_k, tm, d), F32)
    return pl.pallas_call(
        functools.partial(_moe_combine_kernel, tm=tm, top_k=top_k),
        grid=(steps,),
        in_specs=[
            pl.BlockSpec((1, 1, tm * top_k), lambda i: (i, 0, 0), memory_space=pltpu.SMEM),
            pl.BlockSpec((1, 1, tm * top_k), lambda i: (jnp.minimum(i + 1, steps - 1), 0, 0),
                         memory_space=pltpu.SMEM),
            pl.BlockSpec((tm, d), lambda i: (i, 0)),
            pl.BlockSpec((tm, LANE), lambda i: (i, 0)),
            pl.BlockSpec(memory_space=pl.ANY),
        ],
        out_specs=pl.BlockSpec((tm, d), lambda i: (i, 0)),
        out_shape=jax.ShapeDtypeStruct((n, d), F32),
        scratch_shapes=[stage, stage, pltpu.SemaphoreType.DMA(()), pltpu.SemaphoreType.DMA(())],
        compiler_params=_params(("arbitrary",), 2 * _nbytes((tm, d), F32), _nbytes((top_k, tm, d), F32)),
        name="moe_combine",
    )(pos3, pos3, x2, gate, y)


def _moe_schedule(eid, n_exp, rt, n_tiles):
    n, top_k = eid.shape
    a = n * top_k
    eflat = eid.reshape(a)
    onehot = (eflat[:, None] == jnp.arange(n_exp, dtype=I32)[None, :]).astype(I32)
    csum = jnp.cumsum(onehot, axis=0)
    rank = jnp.sum(csum * onehot, axis=1) - 1
    counts = csum[-1]
    nt = (counts + rt - 1) // rt
    tcum = jnp.cumsum(nt)
    tstart = tcum - nt
    n_used = tcum[-1]
    dest = tstart[eflat] * rt + rank
    tid = jnp.arange(n_tiles, dtype=I32)
    tile_blk = jnp.minimum(tid, n_used - 1)
    tile_e = jnp.minimum(jnp.searchsorted(tcum, tile_blk, side="right"), n_exp - 1).astype(I32)
    tile_rows = jnp.where(tid < n_used, jnp.clip(counts[tile_e] - (tile_blk - tstart[tile_e]) * rt, 0, rt), 0)
    tok = jnp.arange(a, dtype=I32) // top_k
    slot_tok = jnp.zeros((n_tiles * rt,), I32).at[dest].set(tok)
    return (n_used.reshape(1).astype(I32), tile_e, tile_blk.astype(I32), tile_rows.astype(I32),
            slot_tok, dest.astype(I32))


def _pick(n, pref):
    t = min(n, pref)
    assert n % t == 0
    return t


def kernel(x, mem, g_mix, w_in, g_q_dil, g_k_dil, g_out_dil, g_out_sb, w_out, g_cross, g_mem, w_q_cross, w_kv_cross, g_q_cross, g_k_cross, w_o_cross, g_ffn, w_group, b_group, w_router, b_router, w_exp_gate, w_exp_up, w_exp_down):
    batch, seq, d = x.shape
    n = batch * seq
    n_heads_mix = d // HEAD_DIM
    n_dil = n_heads_mix // 2
    n_sb = n_heads_mix - n_dil
    w_dil, w_sb = n_dil * HEAD_DIM, n_sb * HEAD_DIM
    n_exp, _, d_exp = w_exp_gate.shape[1:]
    top_k = 2
    scale = 1.0 / math.sqrt(HEAD_DIM)
    slopes = jnp.exp2(-ALIBI_MAX_BIAS * jnp.arange(1, n_dil + 1, dtype=F32) / n_dil)

    tm = _pick(n, 512)
    rt = -(-(5 * n * top_k) // (4 * n_exp * MOE_ROW_CHUNK)) * MOE_ROW_CHUNK
    n_tiles = -(-n * top_k // rt) + n_exp
    ce = d_exp // 3 if (d_exp // 3) % LANE == 0 else d_exp

    xf = x.reshape(n, d)
    for l in range(g_mix.shape[0]):
        h = _rmsnorm_cast(xf, g_mix[l], tm)
        w_in_b = w_in[l].astype(BF16)
        gain_dil = jnp.concatenate([jnp.tile(g_q_dil[l] * scale, n_dil), jnp.tile(g_k_dil[l], n_dil),
                                    jnp.ones((w_dil,), F32)])
        gain_sb = jnp.concatenate([jnp.full((w_sb,), scale * math.log2(math.e), F32), jnp.ones((2 * w_sb,), F32)])
        tn = _pick(w_dil, 1024)
        qkv_dil = _proj(h, w_in_b, gain_dil, 0, 3 * w_dil, 2 * w_dil, F32, tm, tn, "in_proj_dil")
        qkv_sb = _proj(h, w_in_b, gain_sb, 3 * w_dil, 3 * w_sb, 0, BF16, tm, tn, "in_proj_sb")
        o_dil = _dilated(qkv_dil, slopes, batch, seq, n_dil, _pick(seq, 2048))
        o_sb = _stick_breaking(qkv_sb, batch, seq, n_sb, _pick(seq, 1024), _pick(seq, 256))
        x1 = _outproj(o_dil, o_sb, g_out_dil[l], g_out_sb[l], w_out[l].astype(BF16), xf, tm, _pick(d, 512))

        kc, vc = _mem_kv(mem, g_mem[l], w_kv_cross[l].astype(BF16), g_k_cross[l])
        wr = jnp.concatenate([w_router[l], w_group[l],
                              jnp.zeros((d, LANE - n_exp - N_GROUPS), F32)], axis=1)
        wr_hi = wr.astype(BF16)
        wr_lo = (wr - wr_hi.astype(F32)).astype(BF16)
        br = jnp.concatenate([b_router[l].reshape(-1), b_group[l],
                              jnp.zeros((LANE - n_exp - N_GROUPS,), F32)]).reshape(1, LANE)
        x2, hp, eid, gate = _cross_router(x1, g_cross[l], w_q_cross[l].astype(BF16), g_q_cross[l] * scale, kc, vc,
                                          w_o_cross[l].astype(BF16), g_ffn[l], wr_hi, wr_lo, br, seq,
                                          _pick(seq, 256))

        n_used, tile_e, tile_blk, tile_rows, slot_tok, dest = _moe_schedule(eid[:, :top_k], n_exp, rt, n_tiles)
        xs = _moe_gather(n_used, tile_blk, slot_tok, hp, n, n_tiles, rt)
        y = _moe_experts(n_used, tile_e, tile_blk, tile_rows, xs, w_exp_gate[l], w_exp_up[l], w_exp_down[l],
                         n_tiles, rt, ce, _pick(d, 1024))
        xf = _moe_combine(dest, x2, gate, y, _pick(n, 128), top_k)
    return xf.reshape(batch, seq, d)
```

```python
import functools
import math

import jax
import jax.numpy as jnp
from jax import lax
from jax.experimental import pallas as pl
from jax.experimental.pallas import tpu as pltpu

F32 = jnp.float32
BF16 = jnp.bfloat16
I32 = jnp.int32
U32 = jnp.uint32

LANE = 128
HEAD_DIM = 128
EPS = 1e-6
DIL_PAIRS = ((128, 1), (512, 4), (2048, 16))
DIL_STEPS = 128
DIL_UNROLL = 8
ALIBI_MAX_BIAS = 8.0
N_HEADS_CROSS = 4
N_GROUPS = 8
EXPERTS_PER_GROUP = 8
MOE_ROW_ALIGN = 64
NEG_BIG = -1e30
VMEM_LIMIT_CAP = 60000 * 1024
VMEM_INTERNAL = 12 * 1024 * 1024


def _vmem_limit(*block_bytes):
    return int(min(VMEM_LIMIT_CAP, 2 * sum(block_bytes) + VMEM_INTERNAL))


def _nbytes(shape, dtype):
    return math.prod(shape) * jnp.dtype(dtype).itemsize


def _params(sem, *block_bytes):
    return pltpu.CompilerParams(dimension_semantics=sem, vmem_limit_bytes=_vmem_limit(*block_bytes))


def _rms(x, g):
    return x * lax.rsqrt(jnp.mean(x * x, axis=-1, keepdims=True) + EPS) * g


def _rmsnorm_cast_kernel(x_ref, g_ref, o_ref):
    o_ref[...] = _rms(x_ref[...], g_ref[...]).astype(o_ref.dtype)


def _rmsnorm_cast(x, g, tm):
    n, d = x.shape
    return pl.pallas_call(
        _rmsnorm_cast_kernel,
        grid=(n // tm,),
        in_specs=[pl.BlockSpec((tm, d), lambda i: (i, 0)), pl.BlockSpec((1, d), lambda i: (0, 0))],
        out_specs=pl.BlockSpec((tm, d), lambda i: (i, 0)),
        out_shape=jax.ShapeDtypeStruct((n, d), BF16),
        compiler_params=_params(("arbitrary",), _nbytes((tm, d), F32), _nbytes((tm, d), BF16)),
        name="rmsnorm_cast",
    )(x, g.reshape(1, d))


def _proj_kernel(a_ref, w_ref, g_ref, o_ref, *, n_norm_tiles):
    j = pl.program_id(1)
    acc = jnp.dot(a_ref[...], w_ref[...], preferred_element_type=F32)
    tn = acc.shape[1]

    @pl.when(j < n_norm_tiles)
    def _():
        for h in range(tn // HEAD_DIM):
            sl = slice(h * HEAD_DIM, (h + 1) * HEAD_DIM)
            o_ref[:, sl] = _rms(acc[:, sl], g_ref[:, sl]).astype(o_ref.dtype)

    @pl.when(j >= n_norm_tiles)
    def _():
        o_ref[...] = (acc * g_ref[...]).astype(o_ref.dtype)


def _proj(a, w, gain, col0, ncols, norm_cols, out_dtype, tm, tn, name):
    n, k = a.shape
    assert col0 % tn == 0 and ncols % tn == 0 and norm_cols % tn == 0 and n % tm == 0
    jb = col0 // tn
    return pl.pallas_call(
        functools.partial(_proj_kernel, n_norm_tiles=norm_cols // tn),
        grid=(n // tm, ncols // tn),
        in_specs=[
            pl.BlockSpec((tm, k), lambda i, j: (i, 0)),
            pl.BlockSpec((k, tn), lambda i, j: (0, j + jb)),
            pl.BlockSpec((1, tn), lambda i, j: (0, j)),
        ],
        out_specs=pl.BlockSpec((tm, tn), lambda i, j: (i, j)),
        out_shape=jax.ShapeDtypeStruct((n, ncols), out_dtype),
        compiler_params=_params(("arbitrary", "arbitrary"), _nbytes((tm, k), BF16), _nbytes((k, tn), BF16),
                                _nbytes((tm, tn), F32)),
        name=name,
    )(a, w, gain.reshape(1, ncols))


def _dil_kernel(slope_ref, q_ref, k_ref, v_ref, o_ref, ob0, ob1, ob2, ls0, ls1, ls2, *, tq):
    h = pl.program_id(1)
    t0 = pl.program_id(2) * tq
    slope = slope_ref[h]
    J = DIL_STEPS
    qi = lax.broadcasted_iota(I32, (J, 2 * J), 0)
    ki = lax.broadcasted_iota(I32, (J, 2 * J), 1)
    dist = J + qi - ki
    valid = (dist >= 0) & (dist <= J)
    distf = dist.astype(F32)
    in_cur = ki >= J

    for (window, d), ob, ls in zip(DIL_PAIRS, (ob0, ob1, ob2), (ls0, ls1, ls2)):
        assert window // d == J and tq % (d * J) == 0
        bias = jnp.where(valid, (-slope * d) * distf, NEG_BIG)
        shift = d.bit_length() - 1

        def ld(ref, start, d=d):
            if d == 1:
                return ref[pl.ds(start, J), :]
            return ref[pl.ds(start, J, stride=d), :]

        def unit(u, carry, d=d, shift=shift, bias=bias, ob=ob, ls=ls, ld=ld):
            off = (u & (d - 1)) + (u >> shift) * (d * J)
            cur0 = t0 + off
            prev0 = cur0 - d * J
            has_prev = prev0 >= 0
            prev0 = jnp.maximum(prev0, 0)
            qs = ld(q_ref, off).astype(BF16)
            kcat = jnp.concatenate([ld(k_ref, prev0), ld(k_ref, cur0)], axis=0).astype(BF16)
            vcat = jnp.concatenate([ld(v_ref, prev0), ld(v_ref, cur0)], axis=0).astype(BF16)
            s = lax.dot_general(qs, kcat, (((1,), (1,)), ((), ())), preferred_element_type=F32) + bias
            s = jnp.where(jnp.logical_or(in_cur, has_prev), s, NEG_BIG)
            m = jnp.max(s, axis=1, keepdims=True)
            p = jnp.exp(s - m)
            l = jnp.sum(p, axis=1, keepdims=True)
            o = jnp.dot(p.astype(BF16), vcat, preferred_element_type=F32) / l
            lse = jnp.broadcast_to(m + jnp.log(l), (J, HEAD_DIM))
            if d == 1:
                ob[pl.ds(off, J), :] = o
                ls[pl.ds(off, J), :] = lse
            else:
                ob[pl.ds(off, J, stride=d), :] = o
                ls[pl.ds(off, J, stride=d), :] = lse
            return carry

        lax.fori_loop(0, tq // J, unit, 0, unroll=DIL_UNROLL)

    l0, l1, l2 = ls0[...], ls1[...], ls2[...]
    mx = jnp.maximum(jnp.maximum(l0, l1), l2)
    w0, w1, w2 = jnp.exp(l0 - mx), jnp.exp(l1 - mx), jnp.exp(l2 - mx)
    o_ref[...] = (w0 * ob0[...] + w1 * ob1[...] + w2 * ob2[...]) / (w0 + w1 + w2)


def _dilated(qkv, slopes, batch, seq, n_heads, tq):
    assert seq % tq == 0
    n = batch * seq
    nq = seq // tq
    blk = (tq, HEAD_DIM)
    full = (seq, HEAD_DIM)
    scr = [pltpu.VMEM(blk, F32)] * 6
    return pl.pallas_call(
        functools.partial(_dil_kernel, tq=tq),
        grid=(batch, n_heads, nq),
        in_specs=[
            pl.BlockSpec(memory_space=pltpu.SMEM),
            pl.BlockSpec(blk, lambda b, h, c: (b * nq + c, h)),
            pl.BlockSpec(full, lambda b, h, c: (b, n_heads + h)),
            pl.BlockSpec(full, lambda b, h, c: (b, 2 * n_heads + h)),
        ],
        out_specs=pl.BlockSpec(blk, lambda b, h, c: (b * nq + c, h)),
        out_shape=jax.ShapeDtypeStruct((n, n_heads * HEAD_DIM), F32),
        scratch_shapes=scr,
        compiler_params=_params(("arbitrary",) * 3, 2 * _nbytes(full, F32), 5 * _nbytes(blk, F32)),
        name="dilated_attn",
    )(slopes, qkv, qkv, qkv)


def _sb_kernel(q_ref, k_ref, v_ref, o_ref, *, tq, sub):
    n_chain = tq // sub
    assert sub & (sub - 1) == 0
    kb0 = pl.program_id(2) * n_chain
    row = lax.broadcasted_iota(I32, (sub, sub), 0)
    col = lax.broadcasted_iota(I32, (sub, sub), 1)
    after = (row > col).astype(BF16)
    after2 = jnp.concatenate([after, after], axis=0)
    qrow = lax.broadcasted_iota(I32, (tq, sub), 0)
    causal = lax.broadcasted_iota(I32, (tq, sub), 1) < (qrow & (sub - 1))
    chain_of_row = lax.broadcasted_iota(I32, (tq, 1), 0) >> (sub.bit_length() - 1)
    sign = jnp.uint32(0x80000000)

    def sweep(kbs, drop, acc, diag, first_valid):
        starts = [pl.multiple_of(kb * sub, sub) for kb in kbs]
        z2 = jnp.concatenate(
            [lax.dot_general(q_ref[c * sub:(c + 1) * sub, :], k_ref[pl.ds(starts[c], sub), :],
                             (((1,), (1,)), ((), ())), preferred_element_type=F32) for c in range(n_chain)], axis=0)
        neg_abs = pltpu.bitcast(pltpu.bitcast(z2, U32) | sign, F32)
        sp2 = jnp.maximum(z2, 0.0) + jnp.log2(1.0 + jnp.exp2(neg_abs))
        log_beta2 = z2 - sp2
        if diag:
            sp2 = jnp.where(causal, sp2, 0.0)
        hi = sp2.astype(BF16)
        lo = (sp2 - hi.astype(F32)).astype(BF16)
        later = jnp.dot(jnp.concatenate([hi, lo], axis=1), after2, preferred_element_type=F32)
        a = jnp.exp2(log_beta2 - later)
        if diag:
            a = jnp.where(causal, a, 0.0)
        a = a.astype(BF16)
        pv = jnp.concatenate(
            [jnp.dot(a[c * sub:(c + 1) * sub, :], v_ref[pl.ds(starts[c], sub), :], preferred_element_type=F32)
             for c in range(n_chain)], axis=0)
        scale = jnp.exp2(-drop)
        new_drop = drop + later[:, 0:1] + sp2[:, 0:1]
        if first_valid is not None:
            valid = chain_of_row >= first_valid
            scale = jnp.where(valid, scale, 0.0)
            new_drop = jnp.where(valid, new_drop, drop)
        return new_drop, acc + scale * pv

    def alive(drop):
        return (jnp.max(jnp.exp2(-drop)) > 0.0).astype(I32)

    drop, acc = sweep([kb0 + c for c in range(n_chain)], jnp.zeros((tq, 1), F32),
                      jnp.zeros((tq, HEAD_DIM), F32), True, None)

    def cond(st):
        return jnp.logical_and(st[0] <= kb0 + n_chain - 1, st[1] > 0)

    def body(st):
        g, _, drop, acc = st
        drop, acc = sweep([jnp.maximum(kb0 + c - g, 0) for c in range(n_chain)], drop, acc, False, g - kb0)
        return g + 1, alive(drop), drop, acc

    _, _, _, acc = lax.while_loop(cond, body, (jnp.int32(1), alive(drop), drop, acc))
    o_ref[...] = acc


def _stick_breaking(qkv, batch, seq, n_heads, tq, sub):
    assert seq % tq == 0 and tq % sub == 0
    n = batch * seq
    nq = seq // tq
    blk = (tq, HEAD_DIM)
    full = (seq, HEAD_DIM)
    return pl.pallas_call(
        functools.partial(_sb_kernel, tq=tq, sub=sub),
        grid=(batch, n_heads, nq),
        in_specs=[
            pl.BlockSpec(blk, lambda b, h, i: (b * nq + i, h)),
            pl.BlockSpec(full, lambda b, h, i: (b, n_heads + h)),
            pl.BlockSpec(full, lambda b, h, i: (b, 2 * n_heads + h)),
        ],
        out_specs=pl.BlockSpec(blk, lambda b, h, i: (b * nq + i, h)),
        out_shape=jax.ShapeDtypeStruct((n, n_heads * HEAD_DIM), F32),
        compiler_params=_params(("arbitrary",) * 3, 2 * _nbytes(full, BF16), 2 * _nbytes(blk, F32)),
        name="stick_breaking",
    )(qkv, qkv, qkv)


def _outproj_kernel(od_ref, os_ref, gd_ref, gs_ref, w_ref, x_ref, o_ref, m_scr):
    wd = od_ref.shape[1]

    @pl.when(pl.program_id(1) == 0)
    def _():
        m_scr[:, :wd] = _rms(od_ref[...], gd_ref[...]).astype(BF16)
        m_scr[:, wd:] = _rms(os_ref[...], gs_ref[...]).astype(BF16)

    o_ref[...] = x_ref[...] + jnp.dot(m_scr[...], w_ref[...], preferred_element_type=F32)


def _outproj(o_dil, o_sb, g_dil, g_sb, w, x, tm, tn):
    n, wd = o_dil.shape
    ws = o_sb.shape[1]
    d = x.shape[1]
    return pl.pallas_call(
        _outproj_kernel,
        grid=(n // tm, d // tn),
        in_specs=[
            pl.BlockSpec((tm, wd), lambda i, j: (i, 0)),
            pl.BlockSpec((tm, ws), lambda i, j: (i, 0)),
            pl.BlockSpec((1, wd), lambda i, j: (0, 0)),
            pl.BlockSpec((1, ws), lambda i, j: (0, 0)),
            pl.BlockSpec((wd + ws, tn), lambda i, j: (0, j)),
            pl.BlockSpec((tm, tn), lambda i, j: (i, j)),
        ],
        out_specs=pl.BlockSpec((tm, tn), lambda i, j: (i, j)),
        out_shape=jax.ShapeDtypeStruct((n, d), F32),
        scratch_shapes=[pltpu.VMEM((tm, wd + ws), BF16)],
        compiler_params=_params(("arbitrary", "arbitrary"), _nbytes((tm, wd + ws), F32),
                                _nbytes((wd + ws, tn), BF16), 2 * _nbytes((tm, tn), F32),
                                _nbytes((tm, wd + ws), BF16) // 2),
        name="out_proj",
    )(o_dil, o_sb, g_dil.reshape(1, wd), g_sb.reshape(1, ws), w, x)


def _mem_kv_kernel(m_ref, gm_ref, w_ref, gk_ref, k_ref, v_ref):
    hm = _rms(m_ref[0], gm_ref[...]).astype(BF16)
    kv = jnp.dot(hm, w_ref[...], preferred_element_type=F32)
    wc = k_ref.shape[2]
    for h in range(wc // HEAD_DIM):
        sl = slice(h * HEAD_DIM, (h + 1) * HEAD_DIM)
        k_ref[0, :, sl] = _rms(kv[:, sl], gk_ref[...]).astype(BF16)
    v_ref[0] = kv[:, wc:].astype(BF16)


def _mem_kv(mem, g_mem, w_kv, g_k):
    b, m, d = mem.shape
    wc = w_kv.shape[1] // 2
    out = jax.ShapeDtypeStruct((b, m, wc), BF16)
    return pl.pallas_call(
        _mem_kv_kernel,
        grid=(b,),
        in_specs=[
            pl.BlockSpec((1, m, d), lambda i: (i, 0, 0)),
            pl.BlockSpec((1, d), lambda i: (0, 0)),
            pl.BlockSpec((d, 2 * wc), lambda i: (0, 0)),
            pl.BlockSpec((1, HEAD_DIM), lambda i: (0, 0)),
        ],
        out_specs=[pl.BlockSpec((1, m, wc), lambda i: (i, 0, 0))] * 2,
        out_shape=[out, out],
        compiler_params=_params(("arbitrary",), _nbytes((m, d), F32), _nbytes((d, 2 * wc), BF16)),
        name="mem_kv",
    )(mem, g_mem.reshape(1, d), w_kv, g_k.reshape(1, HEAD_DIM))


def _pack_bf16_pairs(xr):
    half = xr.shape[1] // 2
    lo = pltpu.bitcast(xr[:, :half], U32)
    hi = pltpu.bitcast(xr[:, half:], U32)
    return hi | (lo >> 16)


def _cross_router_kernel(x1_ref, gc_ref, wq_ref, gq_ref, kc_ref, vc_ref, wo_ref, gf_ref, wrh_ref, wrl_ref, br_ref,
                         x2_ref, hp_ref, eid_ref, gate_ref):
    x1 = x1_ref[...]
    h = _rms(x1, gc_ref[...]).astype(BF16)
    q = jnp.dot(h, wq_ref[...], preferred_element_type=F32)
    heads = []
    for hd in range(q.shape[1] // HEAD_DIM):
        sl = slice(hd * HEAD_DIM, (hd + 1) * HEAD_DIM)
        qh = _rms(q[:, sl], gq_ref[...]).astype(BF16)
        s = lax.dot_general(qh, kc_ref[0, :, sl], (((1,), (1,)), ((), ())), preferred_element_type=F32)
        p = jnp.exp(s - jnp.max(s, axis=1, keepdims=True))
        l = jnp.sum(p, axis=1, keepdims=True)
        heads.append(jnp.dot(p.astype(BF16), vc_ref[0, :, sl], preferred_element_type=F32) / l)
    o = jnp.concatenate(heads, axis=1).astype(BF16)
    x2 = x1 + jnp.dot(o, wo_ref[...], preferred_element_type=F32)
    x2_ref[...] = x2

    hf = _rms(x2, gf_ref[...])
    hf_hi = hf.astype(BF16)
    hf_hi32 = hf_hi.astype(F32)
    hf_lo = (hf - hf_hi32).astype(BF16)
    packed = _pack_bf16_pairs(hf_hi32)
    tm = packed.shape[0]
    nb = packed.shape[1] // LANE
    for j in range(nb):
        hp_ref[pl.ds(j, tm, stride=nb), :] = packed[:, j * LANE:(j + 1) * LANE]

    n_exp = N_GROUPS * EXPERTS_PER_GROUP
    lg = (jnp.dot(hf_hi, wrh_ref[...], preferred_element_type=F32)
          + jnp.dot(hf_hi, wrl_ref[...], preferred_element_type=F32)
          + jnp.dot(hf_lo, wrh_ref[...], preferred_element_type=F32)) + br_ref[...]
    lane = lax.broadcasted_iota(I32, lg.shape, 1)
    lanef = lane.astype(F32)
    ninf = -jnp.inf
    is_g = (lane >= n_exp) & (lane < n_exp + N_GROUPS)
    gl = jnp.where(is_g, lg, ninf)
    gmax = jnp.max(gl, axis=1, keepdims=True)
    gidx = jnp.min(jnp.where(gl == gmax, lanef, 1e9), axis=1, keepdims=True) - n_exp
    g_gate = 1.0 / jnp.sum(jnp.exp(gl - gmax), axis=1, keepdims=True)
    in_grp = (lane < n_exp) & ((lane // EXPERTS_PER_GROUP).astype(F32) == gidx)
    el = jnp.where(in_grp, lg, ninf)
    m1 = jnp.max(el, axis=1, keepdims=True)
    i1 = jnp.min(jnp.where(el == m1, lanef, 1e9), axis=1, keepdims=True)
    el2 = jnp.where(lanef == i1, ninf, el)
    m2 = jnp.max(el2, axis=1, keepdims=True)
    i2 = jnp.min(jnp.where(el2 == m2, lanef, 1e9), axis=1, keepdims=True)
    p2 = jnp.exp(m2 - m1)
    den = 1.0 + p2
    eid_ref[...] = jnp.where(lane == 0, i1, jnp.where(lane == 1, i2, 0.0)).astype(I32)
    gate_ref[...] = jnp.where(lane == 0, g_gate / den, jnp.where(lane == 1, g_gate * p2 / den, 0.0))


def _cross_router(x1, g_cross, wq, gq, kc, vc, wo, g_ffn, wr_hi, wr_lo, br, seq, tm):
    n, d = x1.shape
    wc = wq.shape[1]
    m = kc.shape[1]
    tiles_per_batch = seq // tm
    const = lambda i: (0, 0)
    return pl.pallas_call(
        _cross_router_kernel,
        grid=(n // tm,),
        in_specs=[
            pl.BlockSpec((tm, d), lambda i: (i, 0)),
            pl.BlockSpec((1, d), const),
            pl.BlockSpec((d, wc), const),
            pl.BlockSpec((1, HEAD_DIM), const),
            pl.BlockSpec((1, m, wc), lambda i: (i // tiles_per_batch, 0, 0)),
            pl.BlockSpec((1, m, wc), lambda i: (i // tiles_per_batch, 0, 0)),
            pl.BlockSpec((wc, d), const),
            pl.BlockSpec((1, d), const),
            pl.BlockSpec((d, LANE), const),
            pl.BlockSpec((d, LANE), const),
            pl.BlockSpec((1, LANE), const),
        ],
        out_specs=[
            pl.BlockSpec((tm, d), lambda i: (i, 0)),
            pl.BlockSpec((tm * (d // 2 // LANE), LANE), lambda i: (i, 0)),
            pl.BlockSpec((tm, LANE), lambda i: (i, 0)),
            pl.BlockSpec((tm, LANE), lambda i: (i, 0)),
        ],
        out_shape=[
            jax.ShapeDtypeStruct((n, d), F32),
            jax.ShapeDtypeStruct((n * (d // 2 // LANE), LANE), U32),
            jax.ShapeDtypeStruct((n, LANE), I32),
            jax.ShapeDtypeStruct((n, LANE), F32),
        ],
        compiler_params=_params(("arbitrary",), 2 * _nbytes((tm, d), F32), _nbytes((tm, d // 2), U32),
                                2 * _nbytes((d, wc), BF16), 2 * _nbytes((d, LANE), BF16)),
        name="cross_router",
    )(x1, g_cross.reshape(1, d), wq, gq.reshape(1, HEAD_DIM), kc, vc, wo, g_ffn.reshape(1, d), wr_hi, wr_lo, br)


def _moe_gather_kernel(nused_ref, tblk_ref, tok_ref, tok_next_ref, src_ref, o_ref, buf0, buf1, sem0, sem1, *, rt, nb):
    t = pl.program_id(0)
    n_used = nused_ref[0]
    bufs, sems = (buf0, buf1), (sem0, sem1)
    half = nb * LANE
    himask = jnp.uint32(0xFFFF0000)

    def start_tile(tok, slot):
        def row_copy(r):
            return pltpu.make_async_copy(src_ref.at[pl.ds(tok[0, 0, r] * nb, nb)],
                                         bufs[slot].at[pl.ds(r * nb, nb)], sems[slot])

        def body(i, c):
            row_copy(2 * i).start(priority=0)
            row_copy(2 * i + 1).start(priority=1)
            return c

        lax.fori_loop(0, rt // 2, body, 0)

    for slot in range(2):
        @pl.when(jnp.logical_and(t % 2 == slot, t < n_used))
        def _(slot=slot):
            if slot == 0:
                @pl.when(t == 0)
                def _():
                    start_tile(tok_ref, 0)

            @pl.when(t + 1 < n_used)
            def _():
                start_tile(tok_next_ref, 1 - slot)

            pltpu.make_async_copy(src_ref.at[pl.ds(0, rt * nb)], bufs[slot], sems[slot]).wait()
            for j in range(nb):
                w = bufs[slot][pl.ds(j, rt, stride=nb), :]
                o_ref[:, j * LANE:(j + 1) * LANE] = pltpu.bitcast(w << 16, F32).astype(BF16)
                o_ref[:, half + j * LANE:half + (j + 1) * LANE] = pltpu.bitcast(w & himask, F32).astype(BF16)


def _moe_gather(n_used, tile_blk, slot_tok, hp, n, n_tiles, rt):
    nb = hp.shape[0] // n
    d = 2 * nb * LANE
    assert rt % 2 == 0 and hp.shape[0] >= rt * nb
    tok3 = slot_tok.reshape(n_tiles, 1, rt)
    stage = pltpu.VMEM((rt * nb, LANE), U32)
    return pl.pallas_call(
        functools.partial(_moe_gather_kernel, rt=rt, nb=nb),
        grid_spec=pltpu.PrefetchScalarGridSpec(
            num_scalar_prefetch=2,
            grid=(n_tiles,),
            in_specs=[
                pl.BlockSpec((1, 1, rt), lambda t, nu, tb: (t, 0, 0), memory_space=pltpu.SMEM),
                pl.BlockSpec((1, 1, rt), lambda t, nu, tb: (jnp.minimum(t + 1, n_tiles - 1), 0, 0),
                             memory_space=pltpu.SMEM),
                pl.BlockSpec(memory_space=pl.ANY),
            ],
            out_specs=pl.BlockSpec((rt, d), lambda t, nu, tb: (tb[t], 0)),
            scratch_shapes=[stage, stage, pltpu.SemaphoreType.DMA(()), pltpu.SemaphoreType.DMA(())],
        ),
        out_shape=jax.ShapeDtypeStruct((n_tiles * rt, d), BF16),
        compiler_params=_params(("arbitrary",), _nbytes((rt, d), BF16), _nbytes((rt * nb, LANE), U32)),
        name="moe_gather",
    )(n_used, tile_blk, tok3, tok3, hp)


def _moe_up_kernel(nused_ref, te_ref, tblk_ref, xs_ref, wg_ref, wu_ref, h_ref, g_acc, u_acc):
    k = pl.program_id(1)

    @pl.when(pl.program_id(0) < nused_ref[0])
    def _():
        x = xs_ref[...]
        g = jnp.dot(x, wg_ref[0].astype(BF16), preferred_element_type=F32)
        u = jnp.dot(x, wu_ref[0].astype(BF16), preferred_element_type=F32)

        @pl.when(k == 0)
        def _():
            g_acc[...] = g
            u_acc[...] = u

        @pl.when(k > 0)
        def _():
            g_acc[...] += g
            u_acc[...] += u

        @pl.when(k == pl.num_programs(1) - 1)
        def _():
            gg = g_acc[...]
            h_ref[...] = (gg * jax.nn.sigmoid(gg) * u_acc[...]).astype(BF16)


def _moe_down_kernel(nused_ref, te_ref, tblk_ref, h_ref, wd_ref, y_ref):
    @pl.when(pl.program_id(0) < nused_ref[0])
    def _():
        y_ref[...] = jnp.dot(h_ref[...], wd_ref[0].astype(BF16), preferred_element_type=F32)


def _moe_experts(n_used, tile_e, tile_blk, xs, w_gate, w_up, w_down, n_tiles, rt, kd, tn):
    n_exp, d, de = w_gate.shape
    p_rows = n_tiles * rt
    nk = d // kd
    nn = d // tn

    def inner(t, k, nused, last):
        return jnp.where(t < nused[0], k, last)

    h_mid = pl.pallas_call(
        _moe_up_kernel,
        grid_spec=pltpu.PrefetchScalarGridSpec(
            num_scalar_prefetch=3,
            grid=(n_tiles, nk),
            in_specs=[
                pl.BlockSpec((rt, kd), lambda t, k, nu, te, tb: (tb[t], inner(t, k, nu, nk - 1))),
                pl.BlockSpec((1, kd, de), lambda t, k, nu, te, tb: (te[t], inner(t, k, nu, nk - 1), 0)),
                pl.BlockSpec((1, kd, de), lambda t, k, nu, te, tb: (te[t], inner(t, k, nu, nk - 1), 0)),
            ],
            out_specs=pl.BlockSpec((rt, de), lambda t, k, nu, te, tb: (tb[t], 0)),
            scratch_shapes=[pltpu.VMEM((rt, de), F32), pltpu.VMEM((rt, de), F32)],
        ),
        out_shape=jax.ShapeDtypeStruct((p_rows, de), BF16),
        compiler_params=_params(("arbitrary", "arbitrary"), _nbytes((rt, kd), BF16), 2 * _nbytes((kd, de), F32),
                                _nbytes((rt, de), F32)),
        name="moe_gate_up",
    )(n_used, tile_e, tile_blk, xs, w_gate, w_up)

    return pl.pallas_call(
        _moe_down_kernel,
        grid_spec=pltpu.PrefetchScalarGridSpec(
            num_scalar_prefetch=3,
            grid=(n_tiles, nn),
            in_specs=[
                pl.BlockSpec((rt, de), lambda t, j, nu, te, tb: (tb[t], 0)),
                pl.BlockSpec((1, de, tn), lambda t, j, nu, te, tb: (te[t], 0, inner(t, j, nu, nn - 1))),
            ],
            out_specs=pl.BlockSpec((rt, tn), lambda t, j, nu, te, tb: (tb[t], inner(t, j, nu, nn - 1))),
        ),
        out_shape=jax.ShapeDtypeStruct((p_rows, d), F32),
        compiler_params=_params(("arbitrary", "arbitrary"), _nbytes((rt, de), BF16), _nbytes((de, tn), F32),
                                _nbytes((rt, tn), F32)),
        name="moe_down",
    )(n_used, tile_e, tile_blk, h_mid, w_down)


def _moe_combine_kernel(pos_ref, pos_next_ref, x_ref, g_ref, y_ref, o_ref, buf0, buf1, sem0, sem1, *, tm, top_k):
    i = pl.program_id(0)
    bufs, sems = (buf0, buf1), (sem0, sem1)

    def start_step(pos, slot):
        def body(r, c):
            for k in range(top_k):
                pltpu.make_async_copy(y_ref.at[pl.ds(pos[0, 0, top_k * r + k], 1)],
                                      bufs[slot].at[pl.ds(k * tm + r, 1)], sems[slot]).start(priority=k % 2)
            return c

        lax.fori_loop(0, tm, body, 0)

    for slot in range(2):
        @pl.when(i % 2 == slot)
        def _(slot=slot):
            if slot == 0:
                @pl.when(i == 0)
                def _():
                    start_step(pos_ref, 0)

            @pl.when(i + 1 < pl.num_programs(0))
            def _():
                start_step(pos_next_ref, 1 - slot)

            pltpu.make_async_copy(y_ref.at[pl.ds(0, top_k * tm)], bufs[slot], sems[slot]).wait()
            acc = x_ref[...]
            for k in range(top_k):
                acc = acc + g_ref[:, k:k + 1] * bufs[slot][k * tm:(k + 1) * tm, :]
            o_ref[...] = acc


def _moe_combine(dest, x2, gate, y, tm, top_k):
    n, d = x2.shape
    steps = n // tm
    assert y.shape[0] >= top_k * tm
    pos3 = dest.reshape(steps, 1, tm * top_k)
    stage = pltpu.VMEM((top_k * tm, d), F32)
    return pl.pallas_call(
        functools.partial(_moe_combine_kernel, tm=tm, top_k=top_k),
        grid=(steps,),
        in_specs=[
            pl.BlockSpec((1, 1, tm * top_k), lambda i: (i, 0, 0), memory_space=pltpu.SMEM),
            pl.BlockSpec((1, 1, tm * top_k), lambda i: (jnp.minimum(i + 1, steps - 1), 0, 0),
                         memory_space=pltpu.SMEM),
            pl.BlockSpec((tm, d), lambda i: (i, 0)),
            pl.BlockSpec((tm, LANE), lambda i: (i, 0)),
            pl.BlockSpec(memory_space=pl.ANY),
        ],
        out_specs=pl.BlockSpec((tm, d), lambda i: (i, 0)),
        out_shape=jax.ShapeDtypeStruct((n, d), F32),
        scratch_shapes=[stage, stage, pltpu.SemaphoreType.DMA(()), pltpu.SemaphoreType.DMA(())],
        compiler_params=_params(("arbitrary",), 2 * _nbytes((tm, d), F32), _nbytes((top_k * tm, d), F32)),
        name="moe_combine",
    )(pos3, pos3, x2, gate, y)


def _moe_schedule(eid, n_exp, rt, n_tiles):
    n, top_k = eid.shape
    a = n * top_k
    eflat = eid.reshape(a)
    onehot = (eflat[:, None] == jnp.arange(n_exp, dtype=I32)[None, :]).astype(I32)
    csum = jnp.cumsum(onehot, axis=0)
    rank = jnp.sum(csum * onehot, axis=1) - 1
    counts = csum[-1]
    nt = (counts + rt - 1) // rt
    tcum = jnp.cumsum(nt)
    tstart = tcum - nt
    n_used = tcum[-1]
    dest = tstart[eflat] * rt + rank
    tid = jnp.arange(n_tiles, dtype=I32)
    tile_blk = jnp.minimum(tid, n_used - 1)
    tile_e = jnp.minimum(jnp.searchsorted(tcum, tile_blk, side="right"), n_exp - 1).astype(I32)
    tok = jnp.arange(a, dtype=I32) // top_k
    slot_tok = jnp.zeros((n_tiles * rt,), I32).at[dest].set(tok)
    return n_used.reshape(1).astype(I32), tile_e, tile_blk.astype(I32), slot_tok, dest.astype(I32)


def _pick(n, pref):
    t = min(n, pref)
    assert n % t == 0
    return t


def kernel(x, mem, g_mix, w_in, g_q_dil, g_k_dil, g_out_dil, g_out_sb, w_out, g_cross, g_mem, w_q_cross, w_kv_cross, g_q_cross, g_k_cross, w_o_cross, g_ffn, w_group, b_group, w_router, b_router, w_exp_gate, w_exp_up, w_exp_down):
    batch, seq, d = x.shape
    n = batch * seq
    n_heads_mix = d // HEAD_DIM
    n_dil = n_heads_mix // 2
    n_sb = n_heads_mix - n_dil
    w_dil, w_sb = n_dil * HEAD_DIM, n_sb * HEAD_DIM
    n_exp, _, d_exp = w_exp_gate.shape[1:]
    top_k = 2
    scale = 1.0 / math.sqrt(HEAD_DIM)
    slopes = jnp.exp2(-ALIBI_MAX_BIAS * jnp.arange(1, n_dil + 1, dtype=F32) / n_dil)

    tm = _pick(n, 512)
    rt = -(-(9 * n * top_k) // (8 * n_exp * MOE_ROW_ALIGN)) * MOE_ROW_ALIGN
    n_tiles = -(-n * top_k // rt) + n_exp
    kd = max(d // 4, LANE)

    xf = x.reshape(n, d)
    for l in range(g_mix.shape[0]):
        h = _rmsnorm_cast(xf, g_mix[l], tm)
        w_in_b = w_in[l].astype(BF16)
        gain_dil = jnp.concatenate([jnp.tile(g_q_dil[l] * scale, n_dil), jnp.tile(g_k_dil[l], n_dil),
                                    jnp.ones((w_dil,), F32)])
        gain_sb = jnp.concatenate([jnp.full((w_sb,), scale * math.log2(math.e), F32), jnp.ones((2 * w_sb,), F32)])
        tn = _pick(w_dil, 1024)
        qkv_dil = _proj(h, w_in_b, gain_dil, 0, 3 * w_dil, 2 * w_dil, F32, tm, tn, "in_proj_dil")
        qkv_sb = _proj(h, w_in_b, gain_sb, 3 * w_dil, 3 * w_sb, 0, BF16, tm, tn, "in_proj_sb")
        o_dil = _dilated(qkv_dil, slopes, batch, seq, n_dil, _pick(seq, 2048))
        o_sb = _stick_breaking(qkv_sb, batch, seq, n_sb, _pick(seq, 1024), _pick(seq, 256))
        x1 = _outproj(o_dil, o_sb, g_out_dil[l], g_out_sb[l], w_out[l].astype(BF16), xf, tm, _pick(d, 1024))

        kc, vc = _mem_kv(mem, g_mem[l], w_kv_cross[l].astype(BF16), g_k_cross[l])
        wr = jnp.concatenate([w_router[l], w_group[l],
                              jnp.zeros((d, LANE - n_exp - N_GROUPS), F32)], axis=1)
        wr_hi = wr.astype(BF16)
        wr_lo = (wr - wr_hi.astype(F32)).astype(BF16)
        br = jnp.concatenate([b_router[l].reshape(-1), b_group[l],
                              jnp.zeros((LANE - n_exp - N_GROUPS,), F32)]).reshape(1, LANE)
        x2, hp, eid, gate = _cross_router(x1, g_cross[l], w_q_cross[l].astype(BF16), g_q_cross[l] * scale, kc, vc,
                                          w_o_cross[l].astype(BF16), g_ffn[l], wr_hi, wr_lo, br, seq,
                                          _pick(seq, 256))

        n_used, tile_e, tile_blk, slot_tok, dest = _moe_schedule(eid[:, :top_k], n_exp, rt, n_tiles)
        xs = _moe_gather(n_used, tile_blk, slot_tok, hp, n, n_tiles, rt)
        y = _moe_experts(n_used, tile_e, tile_blk, xs, w_exp_gate[l], w_exp_up[l], w_exp_down[l],
                         n_tiles, rt, kd, _pick(d, 1024))
        xf = _moe_combine(dest, x2, gate, y, _pick(n, 128), top_k)
    return xf.reshape(batch, seq, d)
```

```python
import functools
import math

import jax
import jax.numpy as jnp
from jax import lax
from jax.experimental import pallas as pl
from jax.experimental.pallas import tpu as pltpu

F32 = jnp.float32
BF16 = jnp.bfloat16
I32 = jnp.int32
U32 = jnp.uint32

LANE = 128
HEAD_DIM = 128
EPS = 1e-6
DIL_PAIRS = ((128, 1), (512, 4), (2048, 16))
DIL_STEPS = 128
DIL_UNROLL = 16
ALIBI_MAX_BIAS = 8.0
N_HEADS_CROSS = 4
N_GROUPS = 8
EXPERTS_PER_GROUP = 8
MOE_ROW_ALIGN = 64
NEG_BIG = -1e30
VMEM_LIMIT_CAP = 60000 * 1024
VMEM_INTERNAL = 12 * 1024 * 1024


def _vmem_limit(*block_bytes):
    return int(min(VMEM_LIMIT_CAP, 2 * sum(block_bytes) + VMEM_INTERNAL))


def _nbytes(shape, dtype):
    return math.prod(shape) * jnp.dtype(dtype).itemsize


def _params(sem, *block_bytes):
    return pltpu.CompilerParams(dimension_semantics=sem, vmem_limit_bytes=_vmem_limit(*block_bytes))


def _rms(x, g):
    return x * lax.rsqrt(jnp.mean(x * x, axis=-1, keepdims=True) + EPS) * g


def _rmsnorm_cast_kernel(x_ref, g_ref, o_ref):
    o_ref[...] = _rms(x_ref[...], g_ref[...]).astype(o_ref.dtype)


def _rmsnorm_cast(x, g, tm):
    n, d = x.shape
    return pl.pallas_call(
        _rmsnorm_cast_kernel,
        grid=(n // tm,),
        in_specs=[pl.BlockSpec((tm, d), lambda i: (i, 0)), pl.BlockSpec((1, d), lambda i: (0, 0))],
        out_specs=pl.BlockSpec((tm, d), lambda i: (i, 0)),
        out_shape=jax.ShapeDtypeStruct((n, d), BF16),
        compiler_params=_params(("arbitrary",), _nbytes((tm, d), F32), _nbytes((tm, d), BF16)),
        name="rmsnorm_cast",
    )(x, g.reshape(1, d))


def _proj_kernel(a_ref, w_ref, g_ref, o_ref, *, n_norm_tiles):
    j = pl.program_id(1)
    acc = jnp.dot(a_ref[...], w_ref[...], preferred_element_type=F32)
    tn = acc.shape[1]

    @pl.when(j < n_norm_tiles)
    def _():
        for h in range(tn // HEAD_DIM):
            sl = slice(h * HEAD_DIM, (h + 1) * HEAD_DIM)
            o_ref[:, sl] = _rms(acc[:, sl], g_ref[:, sl]).astype(o_ref.dtype)

    @pl.when(j >= n_norm_tiles)
    def _():
        o_ref[...] = (acc * g_ref[...]).astype(o_ref.dtype)


def _proj(a, w, gain, col0, ncols, norm_cols, out_dtype, tm, tn, name):
    n, k = a.shape
    assert col0 % tn == 0 and ncols % tn == 0 and norm_cols % tn == 0 and n % tm == 0
    jb = col0 // tn
    return pl.pallas_call(
        functools.partial(_proj_kernel, n_norm_tiles=norm_cols // tn),
        grid=(n // tm, ncols // tn),
        in_specs=[
            pl.BlockSpec((tm, k), lambda i, j: (i, 0)),
            pl.BlockSpec((k, tn), lambda i, j: (0, j + jb)),
            pl.BlockSpec((1, tn), lambda i, j: (0, j)),
        ],
        out_specs=pl.BlockSpec((tm, tn), lambda i, j: (i, j)),
        out_shape=jax.ShapeDtypeStruct((n, ncols), out_dtype),
        compiler_params=_params(("arbitrary", "arbitrary"), _nbytes((tm, k), BF16), _nbytes((k, tn), BF16),
                                _nbytes((tm, tn), F32)),
        name=name,
    )(a, w, gain.reshape(1, ncols))


def _dil_kernel(slope_ref, q_ref, k_ref, v_ref, o_ref, ob0, ob1, ob2, ls0, ls1, ls2, *, tq):
    h = pl.program_id(1)
    t0 = pl.program_id(2) * tq
    slope = slope_ref[h]
    J = DIL_STEPS
    qi = lax.broadcasted_iota(I32, (J, 2 * J), 0)
    ki = lax.broadcasted_iota(I32, (J, 2 * J), 1)
    dist = J + qi - ki
    valid = (dist >= 0) & (dist <= J)
    distf = dist.astype(F32)
    in_cur = ki >= J
    ones_v = jnp.ones((2 * J, HEAD_DIM), BF16)

    for (window, d), ob, ls in zip(DIL_PAIRS, (ob0, ob1, ob2), (ls0, ls1, ls2)):
        assert window // d == J and tq % (d * J) == 0
        bias = jnp.where(valid, (-slope * d) * distf, NEG_BIG)
        shift = d.bit_length() - 1

        def ld(ref, start, d=d):
            if d == 1:
                return ref[pl.ds(start, J), :]
            return ref[pl.ds(start, J, stride=d), :]

        def unit(u, carry, d=d, shift=shift, bias=bias, ob=ob, ls=ls, ld=ld):
            off = (u & (d - 1)) + (u >> shift) * (d * J)
            cur0 = t0 + off
            prev0 = cur0 - d * J
            has_prev = prev0 >= 0
            prev0 = jnp.maximum(prev0, 0)
            qs = ld(q_ref, off).astype(BF16)
            kcat = jnp.concatenate([ld(k_ref, prev0), ld(k_ref, cur0)], axis=0).astype(BF16)
            vcat = jnp.concatenate([ld(v_ref, prev0), ld(v_ref, cur0)], axis=0).astype(BF16)
            s = lax.dot_general(qs, kcat, (((1,), (1,)), ((), ())), preferred_element_type=F32) + bias
            s = jnp.where(jnp.logical_or(in_cur, has_prev), s, NEG_BIG)
            m = jnp.max(s, axis=1, keepdims=True)
            p = jnp.exp(s - m)
            ov = jnp.dot(p.astype(BF16), jnp.concatenate([vcat, ones_v], axis=1), preferred_element_type=F32)
            l = ov[:, HEAD_DIM:]
            o = ov[:, :HEAD_DIM] / l
            lse = m + jnp.log(l)
            if d == 1:
                ob[pl.ds(off, J), :] = o
                ls[pl.ds(off, J), :] = lse
            else:
                ob[pl.ds(off, J, stride=d), :] = o
                ls[pl.ds(off, J, stride=d), :] = lse
            return carry

        lax.fori_loop(0, tq // J, unit, 0, unroll=DIL_UNROLL)

    l0, l1, l2 = ls0[...], ls1[...], ls2[...]
    mx = jnp.maximum(jnp.maximum(l0, l1), l2)
    w0, w1, w2 = jnp.exp(l0 - mx), jnp.exp(l1 - mx), jnp.exp(l2 - mx)
    o_ref[...] = (w0 * ob0[...] + w1 * ob1[...] + w2 * ob2[...]) / (w0 + w1 + w2)


def _dilated(qkv, slopes, batch, seq, n_heads, tq):
    assert seq % tq == 0
    n = batch * seq
    nq = seq // tq
    blk = (tq, HEAD_DIM)
    full = (seq, HEAD_DIM)
    scr = [pltpu.VMEM(blk, F32)] * 6
    return pl.pallas_call(
        functools.partial(_dil_kernel, tq=tq),
        grid=(batch, n_heads, nq),
        in_specs=[
            pl.BlockSpec(memory_space=pltpu.SMEM),
            pl.BlockSpec(blk, lambda b, h, c: (b * nq + c, h)),
            pl.BlockSpec(full, lambda b, h, c: (b, n_heads + h)),
            pl.BlockSpec(full, lambda b, h, c: (b, 2 * n_heads + h)),
        ],
        out_specs=pl.BlockSpec(blk, lambda b, h, c: (b * nq + c, h)),
        out_shape=jax.ShapeDtypeStruct((n, n_heads * HEAD_DIM), F32),
        scratch_shapes=scr,
        compiler_params=_params(("arbitrary",) * 3, 2 * _nbytes(full, F32), 5 * _nbytes(blk, F32)),
        name="dilated_attn",
    )(slopes, qkv, qkv, qkv)


def _sb_kernel(q_ref, k_ref, v_ref, o_ref, *, tq, sub):
    n_chain = tq // sub
    assert sub & (sub - 1) == 0
    kb0 = pl.program_id(2) * n_chain
    row = lax.broadcasted_iota(I32, (sub, sub), 0)
    col = lax.broadcasted_iota(I32, (sub, sub), 1)
    after = (row > col).astype(BF16)
    after2 = jnp.concatenate([after, after], axis=0)
    qrow = lax.broadcasted_iota(I32, (tq, sub), 0)
    causal = lax.broadcasted_iota(I32, (tq, sub), 1) < (qrow & (sub - 1))
    chain_of_row = lax.broadcasted_iota(I32, (tq, 1), 0) >> (sub.bit_length() - 1)
    sign = jnp.uint32(0x80000000)

    def sweep(kbs, drop, acc, diag, first_valid):
        starts = [pl.multiple_of(kb * sub, sub) for kb in kbs]
        z2 = jnp.concatenate(
            [lax.dot_general(q_ref[c * sub:(c + 1) * sub, :], k_ref[pl.ds(starts[c], sub), :],
                             (((1,), (1,)), ((), ())), preferred_element_type=F32) for c in range(n_chain)], axis=0)
        neg_abs = pltpu.bitcast(pltpu.bitcast(z2, U32) | sign, F32)
        sp2 = jnp.maximum(z2, 0.0) + jnp.log2(1.0 + jnp.exp2(neg_abs))
        log_beta2 = z2 - sp2
        if diag:
            sp2 = jnp.where(causal, sp2, 0.0)
        hi = sp2.astype(BF16)
        lo = (sp2 - hi.astype(F32)).astype(BF16)
        later = jnp.dot(jnp.concatenate([hi, lo], axis=1), after2, preferred_element_type=F32)
        a = jnp.exp2(log_beta2 - later)
        if diag:
            a = jnp.where(causal, a, 0.0)
        a = a.astype(BF16)
        pv = jnp.concatenate(
            [jnp.dot(a[c * sub:(c + 1) * sub, :], v_ref[pl.ds(starts[c], sub), :], preferred_element_type=F32)
             for c in range(n_chain)], axis=0)
        scale = jnp.exp2(-drop)
        new_drop = drop + later[:, 0:1] + sp2[:, 0:1]
        if first_valid is not None:
            valid = chain_of_row >= first_valid
            scale = jnp.where(valid, scale, 0.0)
            new_drop = jnp.where(valid, new_drop, drop)
        return new_drop, acc + scale * pv

    def alive(drop):
        return (jnp.max(jnp.exp2(-drop)) > 0.0).astype(I32)

    drop, acc = sweep([kb0 + c for c in range(n_chain)], jnp.zeros((tq, 1), F32),
                      jnp.zeros((tq, HEAD_DIM), F32), True, None)

    def cond(st):
        return jnp.logical_and(st[0] <= kb0 + n_chain - 1, st[1] > 0)

    def body(st):
        g, _, drop, acc = st
        drop, acc = sweep([jnp.maximum(kb0 + c - g, 0) for c in range(n_chain)], drop, acc, False, g - kb0)
        return g + 1, alive(drop), drop, acc

    _, _, _, acc = lax.while_loop(cond, body, (jnp.int32(1), alive(drop), drop, acc))
    o_ref[...] = acc


def _stick_breaking(qkv, batch, seq, n_heads, tq, sub):
    assert seq % tq == 0 and tq % sub == 0
    n = batch * seq
    nq = seq // tq
    blk = (tq, HEAD_DIM)
    full = (seq, HEAD_DIM)
    return pl.pallas_call(
        functools.partial(_sb_kernel, tq=tq, sub=sub),
        grid=(batch, n_heads, nq),
        in_specs=[
            pl.BlockSpec(blk, lambda b, h, i: (b * nq + i, h)),
            pl.BlockSpec(full, lambda b, h, i: (b, n_heads + h)),
            pl.BlockSpec(full, lambda b, h, i: (b, 2 * n_heads + h)),
        ],
        out_specs=pl.BlockSpec(blk, lambda b, h, i: (b * nq + i, h)),
        out_shape=jax.ShapeDtypeStruct((n, n_heads * HEAD_DIM), F32),
        compiler_params=_params(("arbitrary",) * 3, 2 * _nbytes(full, BF16), 2 * _nbytes(blk, F32)),
        name="stick_breaking",
    )(qkv, qkv, qkv)


def _outproj_kernel(od_ref, os_ref, gd_ref, gs_ref, w_ref, x_ref, o_ref, m_scr):
    wd = od_ref.shape[1]

    @pl.when(pl.program_id(1) == 0)
    def _():
        m_scr[:, :wd] = _rms(od_ref[...], gd_ref[...]).astype(BF16)
        m_scr[:, wd:] = _rms(os_ref[...], gs_ref[...]).astype(BF16)

    o_ref[...] = x_ref[...] + jnp.dot(m_scr[...], w_ref[...], preferred_element_type=F32)


def _outproj(o_dil, o_sb, g_dil, g_sb, w, x, tm, tn):
    n, wd = o_dil.shape
    ws = o_sb.shape[1]
    d = x.shape[1]
    return pl.pallas_call(
        _outproj_kernel,
        grid=(n // tm, d // tn),
        in_specs=[
            pl.BlockSpec((tm, wd), lambda i, j: (i, 0)),
            pl.BlockSpec((tm, ws), lambda i, j: (i, 0)),
            pl.BlockSpec((1, wd), lambda i, j: (0, 0)),
            pl.BlockSpec((1, ws), lambda i, j: (0, 0)),
            pl.BlockSpec((wd + ws, tn), lambda i, j: (0, j)),
            pl.BlockSpec((tm, tn), lambda i, j: (i, j)),
        ],
        out_specs=pl.BlockSpec((tm, tn), lambda i, j: (i, j)),
        out_shape=jax.ShapeDtypeStruct((n, d), F32),
        scratch_shapes=[pltpu.VMEM((tm, wd + ws), BF16)],
        compiler_params=_params(("arbitrary", "arbitrary"), _nbytes((tm, wd + ws), F32),
                                _nbytes((wd + ws, tn), BF16), 2 * _nbytes((tm, tn), F32),
                                _nbytes((tm, wd + ws), BF16) // 2),
        name="out_proj",
    )(o_dil, o_sb, g_dil.reshape(1, wd), g_sb.reshape(1, ws), w, x)


def _mem_kv_kernel(m_ref, gm_ref, w_ref, gk_ref, k_ref, v_ref):
    hm = _rms(m_ref[0], gm_ref[...]).astype(BF16)
    kv = jnp.dot(hm, w_ref[...], preferred_element_type=F32)
    wc = k_ref.shape[2]
    for h in range(wc // HEAD_DIM):
        sl = slice(h * HEAD_DIM, (h + 1) * HEAD_DIM)
        k_ref[0, :, sl] = _rms(kv[:, sl], gk_ref[...]).astype(BF16)
    v_ref[0] = kv[:, wc:].astype(BF16)


def _mem_kv(mem, g_mem, w_kv, g_k):
    b, m, d = mem.shape
    wc = w_kv.shape[1] // 2
    out = jax.ShapeDtypeStruct((b, m, wc), BF16)
    return pl.pallas_call(
        _mem_kv_kernel,
        grid=(b,),
        in_specs=[
            pl.BlockSpec((1, m, d), lambda i: (i, 0, 0)),
            pl.BlockSpec((1, d), lambda i: (0, 0)),
            pl.BlockSpec((d, 2 * wc), lambda i: (0, 0)),
            pl.BlockSpec((1, HEAD_DIM), lambda i: (0, 0)),
        ],
        out_specs=[pl.BlockSpec((1, m, wc), lambda i: (i, 0, 0))] * 2,
        out_shape=[out, out],
        compiler_params=_params(("arbitrary",), _nbytes((m, d), F32), _nbytes((d, 2 * wc), BF16)),
        name="mem_kv",
    )(mem, g_mem.reshape(1, d), w_kv, g_k.reshape(1, HEAD_DIM))


def _pack_bf16_pairs(xr):
    half = xr.shape[1] // 2
    lo = pltpu.bitcast(xr[:, :half], U32)
    hi = pltpu.bitcast(xr[:, half:], U32)
    return hi | (lo >> 16)


def _cross_router_kernel(x1_ref, gc_ref, wq_ref, gq_ref, kc_ref, vc_ref, wo_ref, gf_ref, wrh_ref, wrl_ref, br_ref,
                         x2_ref, hp_ref, eid_ref, gate_ref):
    x1 = x1_ref[...]
    h = _rms(x1, gc_ref[...]).astype(BF16)
    q = jnp.dot(h, wq_ref[...], preferred_element_type=F32)
    heads = []
    for hd in range(q.shape[1] // HEAD_DIM):
        sl = slice(hd * HEAD_DIM, (hd + 1) * HEAD_DIM)
        qh = _rms(q[:, sl], gq_ref[...]).astype(BF16)
        s = lax.dot_general(qh, kc_ref[0, :, sl], (((1,), (1,)), ((), ())), preferred_element_type=F32)
        p = jnp.exp(s - jnp.max(s, axis=1, keepdims=True))
        l = jnp.sum(p, axis=1, keepdims=True)
        heads.append(jnp.dot(p.astype(BF16), vc_ref[0, :, sl], preferred_element_type=F32) / l)
    o = jnp.concatenate(heads, axis=1).astype(BF16)
    x2 = x1 + jnp.dot(o, wo_ref[...], preferred_element_type=F32)
    x2_ref[...] = x2

    hf = _rms(x2, gf_ref[...])
    hf_hi = hf.astype(BF16)
    hf_hi32 = hf_hi.astype(F32)
    hf_lo = (hf - hf_hi32).astype(BF16)
    packed = _pack_bf16_pairs(hf_hi32)
    tm = packed.shape[0]
    nb = packed.shape[1] // LANE
    for j in range(nb):
        hp_ref[pl.ds(j, tm, stride=nb), :] = packed[:, j * LANE:(j + 1) * LANE]

    n_exp = N_GROUPS * EXPERTS_PER_GROUP
    lg = (jnp.dot(hf_hi, wrh_ref[...], preferred_element_type=F32)
          + jnp.dot(hf_hi, wrl_ref[...], preferred_element_type=F32)
          + jnp.dot(hf_lo, wrh_ref[...], preferred_element_type=F32)) + br_ref[...]
    lane = lax.broadcasted_iota(I32, lg.shape, 1)
    lanef = lane.astype(F32)
    ninf = -jnp.inf
    is_g = (lane >= n_exp) & (lane < n_exp + N_GROUPS)
    gl = jnp.where(is_g, lg, ninf)
    gmax = jnp.max(gl, axis=1, keepdims=True)
    gidx = jnp.min(jnp.where(gl == gmax, lanef, 1e9), axis=1, keepdims=True) - n_exp
    g_gate = 1.0 / jnp.sum(jnp.exp(gl - gmax), axis=1, keepdims=True)
    in_grp = (lane < n_exp) & ((lane // EXPERTS_PER_GROUP).astype(F32) == gidx)
    el = jnp.where(in_grp, lg, ninf)
    m1 = jnp.max(el, axis=1, keepdims=True)
    i1 = jnp.min(jnp.where(el == m1, lanef, 1e9), axis=1, keepdims=True)
    el2 = jnp.where(lanef == i1, ninf, el)
    m2 = jnp.max(el2, axis=1, keepdims=True)
    i2 = jnp.min(jnp.where(el2 == m2, lanef, 1e9), axis=1, keepdims=True)
    p2 = jnp.exp(m2 - m1)
    den = 1.0 + p2
    eid_ref[...] = jnp.where(lane == 0, i1, jnp.where(lane == 1, i2, 0.0)).astype(I32)
    gate_ref[...] = jnp.where(lane == 0, g_gate / den, jnp.where(lane == 1, g_gate * p2 / den, 0.0))


def _cross_router(x1, g_cross, wq, gq, kc, vc, wo, g_ffn, wr_hi, wr_lo, br, seq, tm):
    n, d = x1.shape
    wc = wq.shape[1]
    m = kc.shape[1]
    tiles_per_batch = seq // tm
    const = lambda i: (0, 0)
    return pl.pallas_call(
        _cross_router_kernel,
        grid=(n // tm,),
        in_specs=[
            pl.BlockSpec((tm, d), lambda i: (i, 0)),
            pl.BlockSpec((1, d), const),
            pl.BlockSpec((d, wc), const),
            pl.BlockSpec((1, HEAD_DIM), const),
            pl.BlockSpec((1, m, wc), lambda i: (i // tiles_per_batch, 0, 0)),
            pl.BlockSpec((1, m, wc), lambda i: (i // tiles_per_batch, 0, 0)),
            pl.BlockSpec((wc, d), const),
            pl.BlockSpec((1, d), const),
            pl.BlockSpec((d, LANE), const),
            pl.BlockSpec((d, LANE), const),
            pl.BlockSpec((1, LANE), const),
        ],
        out_specs=[
            pl.BlockSpec((tm, d), lambda i: (i, 0)),
            pl.BlockSpec((tm * (d // 2 // LANE), LANE), lambda i: (i, 0)),
            pl.BlockSpec((tm, LANE), lambda i: (i, 0)),
            pl.BlockSpec((tm, LANE), lambda i: (i, 0)),
        ],
        out_shape=[
            jax.ShapeDtypeStruct((n, d), F32),
            jax.ShapeDtypeStruct((n * (d // 2 // LANE), LANE), U32),
            jax.ShapeDtypeStruct((n, LANE), I32),
            jax.ShapeDtypeStruct((n, LANE), F32),
        ],
        compiler_params=_params(("arbitrary",), 2 * _nbytes((tm, d), F32), _nbytes((tm, d // 2), U32),
                                2 * _nbytes((d, wc), BF16), 2 * _nbytes((d, LANE), BF16)),
        name="cross_router",
    )(x1, g_cross.reshape(1, d), wq, gq.reshape(1, HEAD_DIM), kc, vc, wo, g_ffn.reshape(1, d), wr_hi, wr_lo, br)


def _moe_up_kernel(nused_ref, te_ref, tblk_ref, tok_ref, tok_next_ref, hp_ref, wg_ref, wu_ref, h_ref,
                   x_scr, stage0, stage1, g_acc, u_acc, sem0, sem1, *, rt, nb, kd):
    t = pl.program_id(0)
    k = pl.program_id(1)
    n_used = nused_ref[0]
    stages, sems = (stage0, stage1), (sem0, sem1)
    half = nb * LANE
    himask = jnp.uint32(0xFFFF0000)

    def start_tile(tok, slot):
        def body(r, c):
            pltpu.make_async_copy(hp_ref.at[pl.ds(tok[0, 0, r] * nb, nb)],
                                  stages[slot].at[pl.ds(r * nb, nb)], sems[slot]).start(priority=1)
            return c

        lax.fori_loop(0, rt, body, 0)

    def put(col, val):
        x_scr[col // kd, :, col % kd:col % kd + LANE] = val.astype(BF16)

    @pl.when(t < n_used)
    def _():
        for slot in range(2):
            @pl.when(jnp.logical_and(k == 0, t % 2 == slot))
            def _(slot=slot):
                if slot == 0:
                    @pl.when(t == 0)
                    def _():
                        start_tile(tok_ref, 0)

                @pl.when(t + 1 < n_used)
                def _():
                    start_tile(tok_next_ref, 1 - slot)

                pltpu.make_async_copy(hp_ref.at[pl.ds(0, rt * nb)], stages[slot], sems[slot]).wait()
                for j in range(nb):
                    w = stages[slot][pl.ds(j, rt, stride=nb), :]
                    put(j * LANE, pltpu.bitcast(w << 16, F32))
                    put(half + j * LANE, pltpu.bitcast(w & himask, F32))

        x = x_scr[k]
        g = jnp.dot(x, wg_ref[0].astype(BF16), preferred_element_type=F32)
        u = jnp.dot(x, wu_ref[0].astype(BF16), preferred_element_type=F32)

        @pl.when(k == 0)
        def _():
            g_acc[...] = g
            u_acc[...] = u

        @pl.when(k > 0)
        def _():
            g_acc[...] += g
            u_acc[...] += u

        @pl.when(k == pl.num_programs(1) - 1)
        def _():
            gg = g_acc[...]
            h_ref[...] = (gg * jax.nn.sigmoid(gg) * u_acc[...]).astype(BF16)


def _moe_down_kernel(nused_ref, te_ref, tblk_ref, h_ref, wd_ref, y_ref):
    @pl.when(pl.program_id(0) < nused_ref[0])
    def _():
        y_ref[...] = jnp.dot(h_ref[...], wd_ref[0].astype(BF16), preferred_element_type=F32)


def _moe_experts(n_used, tile_e, tile_blk, slot_tok, hp, w_gate, w_up, w_down, n, n_tiles, rt, kd, tn):
    n_exp, d, de = w_gate.shape
    nb = hp.shape[0] // n
    assert 2 * nb * LANE == d and kd % LANE == 0 and hp.shape[0] >= rt * nb
    p_rows = n_tiles * rt
    nk = d // kd
    nn = d // tn
    tok3 = slot_tok.reshape(n_tiles, 1, rt)

    def inner(t, k, nused, last):
        return jnp.where(t < nused[0], k, last)

    stage = pltpu.VMEM((rt * nb, LANE), U32)
    h_mid = pl.pallas_call(
        functools.partial(_moe_up_kernel, rt=rt, nb=nb, kd=kd),
        grid_spec=pltpu.PrefetchScalarGridSpec(
            num_scalar_prefetch=3,
            grid=(n_tiles, nk),
            in_specs=[
                pl.BlockSpec((1, 1, rt), lambda t, k, nu, te, tb: (t, 0, 0), memory_space=pltpu.SMEM),
                pl.BlockSpec((1, 1, rt), lambda t, k, nu, te, tb: (jnp.minimum(t + 1, n_tiles - 1), 0, 0),
                             memory_space=pltpu.SMEM),
                pl.BlockSpec(memory_space=pl.ANY),
                pl.BlockSpec((1, kd, de), lambda t, k, nu, te, tb: (te[t], inner(t, k, nu, nk - 1), 0)),
                pl.BlockSpec((1, kd, de), lambda t, k, nu, te, tb: (te[t], inner(t, k, nu, nk - 1), 0)),
            ],
            out_specs=pl.BlockSpec((rt, de), lambda t, k, nu, te, tb: (tb[t], 0)),
            scratch_shapes=[pltpu.VMEM((nk, rt, kd), BF16), stage, stage,
                            pltpu.VMEM((rt, de), F32), pltpu.VMEM((rt, de), F32),
                            pltpu.SemaphoreType.DMA(()), pltpu.SemaphoreType.DMA(())],
        ),
        out_shape=jax.ShapeDtypeStruct((p_rows, de), BF16),
        compiler_params=_params(("arbitrary", "arbitrary"), 2 * _nbytes((kd, de), F32), _nbytes((rt, de), F32),
                                _nbytes((rt * nb, LANE), U32), _nbytes((rt, d), BF16) // 2),
        name="moe_gate_up",
    )(n_used, tile_e, tile_blk, tok3, tok3, hp, w_gate, w_up)

    return pl.pallas_call(
        _moe_down_kernel,
        grid_spec=pltpu.PrefetchScalarGridSpec(
            num_scalar_prefetch=3,
            grid=(n_tiles, nn),
            in_specs=[
                pl.BlockSpec((rt, de), lambda t, j, nu, te, tb: (tb[t], 0)),
                pl.BlockSpec((1, de, tn), lambda t, j, nu, te, tb: (te[t], 0, inner(t, j, nu, nn - 1))),
            ],
            out_specs=pl.BlockSpec((rt, tn), lambda t, j, nu, te, tb: (tb[t], inner(t, j, nu, nn - 1))),
        ),
        out_shape=jax.ShapeDtypeStruct((p_rows, d), F32),
        compiler_params=_params(("arbitrary", "arbitrary"), _nbytes((rt, de), BF16), _nbytes((de, tn), F32),
                                _nbytes((rt, tn), F32)),
        name="moe_down",
    )(n_used, tile_e, tile_blk, h_mid, w_down)


def _moe_combine_kernel(pos_ref, pos_next_ref, x_ref, g_ref, y_ref, o_ref, buf0, buf1, sem0, sem1, *, tm, top_k):
    i = pl.program_id(0)
    bufs, sems = (buf0, buf1), (sem0, sem1)

    def start_step(pos, slot):
        def body(r, c):
            for k in range(top_k):
                pltpu.make_async_copy(y_ref.at[pl.ds(pos[0, 0, top_k * r + k], 1)],
                                      bufs[slot].at[pl.ds(k * tm + r, 1)], sems[slot]).start(priority=k % 2)
            return c

        lax.fori_loop(0, tm, body, 0)

    for slot in range(2):
        @pl.when(i % 2 == slot)
        def _(slot=slot):
            if slot == 0:
                @pl.when(i == 0)
                def _():
                    start_step(pos_ref, 0)

            @pl.when(i + 1 < pl.num_programs(0))
            def _():
                start_step(pos_next_ref, 1 - slot)

            pltpu.make_async_copy(y_ref.at[pl.ds(0, top_k * tm)], bufs[slot], sems[slot]).wait()
            acc = x_ref[...]
            for k in range(top_k):
                acc = acc + g_ref[:, k:k + 1] * bufs[slot][k * tm:(k + 1) * tm, :]
            o_ref[...] = acc


def _moe_combine(dest, x2, gate, y, tm, top_k):
    n, d = x2.shape
    steps = n // tm
    assert y.shape[0] >= top_k * tm
    pos3 = dest.reshape(steps, 1, tm * top_k)
    stage = pltpu.VMEM((top_k * tm, d), F32)
    return pl.pallas_call(
        functools.partial(_moe_combine_kernel, tm=tm, top_k=top_k),
        grid=(steps,),
        in_specs=[
            pl.BlockSpec((1, 1, tm * top_k), lambda i: (i, 0, 0), memory_space=pltpu.SMEM),
            pl.BlockSpec((1, 1, tm * top_k), lambda i: (jnp.minimum(i + 1, steps - 1), 0, 0),
                         memory_space=pltpu.SMEM),
            pl.BlockSpec((tm, d), lambda i: (i, 0)),
            pl.BlockSpec((tm, LANE), lambda i: (i, 0)),
            pl.BlockSpec(memory_space=pl.ANY),
        ],
        out_specs=pl.BlockSpec((tm, d), lambda i: (i, 0)),
        out_shape=jax.ShapeDtypeStruct((n, d), F32),
        scratch_shapes=[stage, stage, pltpu.SemaphoreType.DMA(()), pltpu.SemaphoreType.DMA(())],
        compiler_params=_params(("arbitrary",), 2 * _nbytes((tm, d), F32), _nbytes((top_k * tm, d), F32)),
        name="moe_combine",
    )(pos3, pos3, x2, gate, y)


def _moe_schedule(eid, n_exp, rt, n_tiles):
    n, top_k = eid.shape
    a = n * top_k
    eflat = eid.reshape(a)
    onehot = (eflat[:, None] == jnp.arange(n_exp, dtype=I32)[None, :]).astype(I32)
    csum = jnp.cumsum(onehot, axis=0)
    rank = jnp.sum(csum * onehot, axis=1) - 1
    counts = csum[-1]
    nt = (counts + rt - 1) // rt
    tcum = jnp.cumsum(nt)
    tstart = tcum - nt
    n_used = tcum[-1]
    dest = tstart[eflat] * rt + rank
    tid = jnp.arange(n_tiles, dtype=I32)
    tile_blk = jnp.minimum(tid, n_used - 1)
    tile_e = jnp.minimum(jnp.searchsorted(tcum, tile_blk, side="right"), n_exp - 1).astype(I32)
    tok = jnp.arange(a, dtype=I32) // top_k
    slot_tok = jnp.zeros((n_tiles * rt,), I32).at[dest].set(tok)
    return n_used.reshape(1).astype(I32), tile_e, tile_blk.astype(I32), slot_tok, dest.astype(I32)


def _pick(n, pref):
    t = min(n, pref)
    assert n % t == 0
    return t


def kernel(x, mem, g_mix, w_in, g_q_dil, g_k_dil, g_out_dil, g_out_sb, w_out, g_cross, g_mem, w_q_cross, w_kv_cross, g_q_cross, g_k_cross, w_o_cross, g_ffn, w_group, b_group, w_router, b_router, w_exp_gate, w_exp_up, w_exp_down):
    batch, seq, d = x.shape
    n = batch * seq
    n_heads_mix = d // HEAD_DIM
    n_dil = n_heads_mix // 2
    n_sb = n_heads_mix - n_dil
    w_dil, w_sb = n_dil * HEAD_DIM, n_sb * HEAD_DIM
    n_exp = w_exp_gate.shape[1]
    top_k = 2
    scale = 1.0 / math.sqrt(HEAD_DIM)
    slopes = jnp.exp2(-ALIBI_MAX_BIAS * jnp.arange(1, n_dil + 1, dtype=F32) / n_dil)

    tm = _pick(n, 512)
    rt = -(-(9 * n * top_k) // (8 * n_exp * MOE_ROW_ALIGN)) * MOE_ROW_ALIGN
    n_tiles = -(-n * top_k // rt) + n_exp
    kd = max(d // 4, LANE)

    xf = x.reshape(n, d)
    for l in range(g_mix.shape[0]):
        h = _rmsnorm_cast(xf, g_mix[l], tm)
        w_in_b = w_in[l].astype(BF16)
        gain_dil = jnp.concatenate([jnp.tile(g_q_dil[l] * scale, n_dil), jnp.tile(g_k_dil[l], n_dil),
                                    jnp.ones((w_dil,), F32)])
        gain_sb = jnp.concatenate([jnp.full((w_sb,), scale * math.log2(math.e), F32), jnp.ones((2 * w_sb,), F32)])
        tn = _pick(w_dil, 1024)
        tmp = _pick(n, 1024)
        qkv_dil = _proj(h, w_in_b, gain_dil, 0, 3 * w_dil, 2 * w_dil, F32, tmp, tn, "in_proj_dil")
        qkv_sb = _proj(h, w_in_b, gain_sb, 3 * w_dil, 3 * w_sb, 0, BF16, tmp, tn, "in_proj_sb")
        o_dil = _dilated(qkv_dil, slopes, batch, seq, n_dil, _pick(seq, 2048))
        o_sb = _stick_breaking(qkv_sb, batch, seq, n_sb, _pick(seq, 1024), _pick(seq, 256))
        x1 = _outproj(o_dil, o_sb, g_out_dil[l], g_out_sb[l], w_out[l].astype(BF16), xf, tm, _pick(d, 1024))

        kc, vc = _mem_kv(mem, g_mem[l], w_kv_cross[l].astype(BF16), g_k_cross[l])
        wr = jnp.concatenate([w_router[l], w_group[l],
                              jnp.zeros((d, LANE - n_exp - N_GROUPS), F32)], axis=1)
        wr_hi = wr.astype(BF16)
        wr_lo = (wr - wr_hi.astype(F32)).astype(BF16)
        br = jnp.concatenate([b_router[l].reshape(-1), b_group[l],
                              jnp.zeros((LANE - n_exp - N_GROUPS,), F32)]).reshape(1, LANE)
        x2, hp, eid, gate = _cross_router(x1, g_cross[l], w_q_cross[l].astype(BF16), g_q_cross[l] * scale, kc, vc,
                                          w_o_cross[l].astype(BF16), g_ffn[l], wr_hi, wr_lo, br, seq,
                                          _pick(seq, 256))

        n_used, tile_e, tile_blk, slot_tok, dest = _moe_schedule(eid[:, :top_k], n_exp, rt, n_tiles)
        y = _moe_experts(n_used, tile_e, tile_blk, slot_tok, hp, w_exp_gate[l], w_exp_up[l], w_exp_down[l],
                         n, n_tiles, rt, kd, _pick(d, 1024))
        xf = _moe_combine(dest, x2, gate, y, _pick(n, 128), top_k)
    return xf.reshape(batch, seq, d)
```

```python
import functools
import math

import jax
import jax.numpy as jnp
from jax import lax
from jax.experimental import pallas as pl
from jax.experimental.pallas import tpu as pltpu

F32 = jnp.float32
BF16 = jnp.bfloat16
I32 = jnp.int32
U32 = jnp.uint32

LANE = 128
HEAD_DIM = 128
EPS = 1e-6
DIL_PAIRS = ((128, 1), (512, 4), (2048, 16))
DIL_STEPS = 128
DIL_UNROLL = 16
ALIBI_MAX_BIAS = 8.0
N_HEADS_CROSS = 4
N_GROUPS = 8
EXPERTS_PER_GROUP = 8
MOE_ROW_ALIGN = 64
NEG_BIG = -1e30
VMEM_LIMIT_CAP = 60000 * 1024
VMEM_INTERNAL = 12 * 1024 * 1024


def _vmem_limit(*block_bytes):
    return int(min(VMEM_LIMIT_CAP, 2 * sum(block_bytes) + VMEM_INTERNAL))


def _nbytes(shape, dtype):
    return math.prod(shape) * jnp.dtype(dtype).itemsize


def _params(sem, *block_bytes):
    return pltpu.CompilerParams(dimension_semantics=sem, vmem_limit_bytes=_vmem_limit(*block_bytes))


def _rms(x, g):
    return x * lax.rsqrt(jnp.mean(x * x, axis=-1, keepdims=True) + EPS) * g


def _rmsnorm_cast_kernel(x_ref, g_ref, o_ref):
    o_ref[...] = _rms(x_ref[...], g_ref[...]).astype(o_ref.dtype)


def _rmsnorm_cast(x, g, tm):
    n, d = x.shape
    return pl.pallas_call(
        _rmsnorm_cast_kernel,
        grid=(n // tm,),
        in_specs=[pl.BlockSpec((tm, d), lambda i: (i, 0)), pl.BlockSpec((1, d), lambda i: (0, 0))],
        out_specs=pl.BlockSpec((tm, d), lambda i: (i, 0)),
        out_shape=jax.ShapeDtypeStruct((n, d), BF16),
        compiler_params=_params(("arbitrary",), _nbytes((tm, d), F32), _nbytes((tm, d), BF16)),
        name="rmsnorm_cast",
    )(x, g.reshape(1, d))


def _proj_kernel(a_ref, w_ref, g_ref, o_ref, *, n_norm_tiles):
    j = pl.program_id(1)
    acc = jnp.dot(a_ref[...], w_ref[...], preferred_element_type=F32)
    tn = acc.shape[1]

    @pl.when(j < n_norm_tiles)
    def _():
        for h in range(tn // HEAD_DIM):
            sl = slice(h * HEAD_DIM, (h + 1) * HEAD_DIM)
            o_ref[:, sl] = _rms(acc[:, sl], g_ref[:, sl]).astype(o_ref.dtype)

    @pl.when(j >= n_norm_tiles)
    def _():
        o_ref[...] = (acc * g_ref[...]).astype(o_ref.dtype)


def _proj(a, w, gain, col0, ncols, norm_cols, out_dtype, tm, tn, name):
    n, k = a.shape
    assert col0 % tn == 0 and ncols % tn == 0 and norm_cols % tn == 0 and n % tm == 0
    jb = col0 // tn
    return pl.pallas_call(
        functools.partial(_proj_kernel, n_norm_tiles=norm_cols // tn),
        grid=(n // tm, ncols // tn),
        in_specs=[
            pl.BlockSpec((tm, k), lambda i, j: (i, 0)),
            pl.BlockSpec((k, tn), lambda i, j: (0, j + jb)),
            pl.BlockSpec((1, tn), lambda i, j: (0, j)),
        ],
        out_specs=pl.BlockSpec((tm, tn), lambda i, j: (i, j)),
        out_shape=jax.ShapeDtypeStruct((n, ncols), out_dtype),
        compiler_params=_params(("arbitrary", "arbitrary"), _nbytes((tm, k), BF16), _nbytes((k, tn), BF16),
                                _nbytes((tm, tn), F32)),
        name=name,
    )(a, w, gain.reshape(1, ncols))


def _dil_kernel(slope_ref, q_ref, k_ref, v_ref, o_ref, ob0, ob1, ob2, ls0, ls1, ls2, *, tq):
    h = pl.program_id(1)
    t0 = pl.program_id(2) * tq
    slope = slope_ref[h]
    J = DIL_STEPS
    qi = lax.broadcasted_iota(I32, (J, 2 * J), 0)
    ki = lax.broadcasted_iota(I32, (J, 2 * J), 1)
    dist = J + qi - ki
    valid = (dist >= 0) & (dist <= J)
    distf = dist.astype(F32)
    in_cur = ki >= J
    ones_v = jnp.ones((2 * J, HEAD_DIM), BF16)

    for (window, d), ob, ls in zip(DIL_PAIRS, (ob0, ob1, ob2), (ls0, ls1, ls2)):
        assert window // d == J and tq % (d * J) == 0
        bias = jnp.where(valid, (-slope * d) * distf, NEG_BIG)
        shift = d.bit_length() - 1

        def ld(ref, start, d=d):
            if d == 1:
                return ref[pl.ds(start, J), :]
            return ref[pl.ds(start, J, stride=d), :]

        def unit(u, carry, d=d, shift=shift, bias=bias, ob=ob, ls=ls, ld=ld):
            off = (u & (d - 1)) + (u >> shift) * (d * J)
            cur0 = t0 + off
            prev0 = cur0 - d * J
            has_prev = prev0 >= 0
            prev0 = jnp.maximum(prev0, 0)
            qs = ld(q_ref, off).astype(BF16)
            kcat = jnp.concatenate([ld(k_ref, prev0), ld(k_ref, cur0)], axis=0).astype(BF16)
            vcat = jnp.concatenate([ld(v_ref, prev0), ld(v_ref, cur0)], axis=0).astype(BF16)
            s = lax.dot_general(qs, kcat, (((1,), (1,)), ((), ())), preferred_element_type=F32) + bias
            s = jnp.where(jnp.logical_or(in_cur, has_prev), s, NEG_BIG)
            m = jnp.max(s, axis=1, keepdims=True)
            p = jnp.exp(s - m)
            ov = jnp.dot(p.astype(BF16), jnp.concatenate([vcat, ones_v], axis=1), preferred_element_type=F32)
            l = ov[:, HEAD_DIM:]
            o = ov[:, :HEAD_DIM] / l
            lse = m + jnp.log(l)
            if d == 1:
                ob[pl.ds(off, J), :] = o
                ls[pl.ds(off, J), :] = lse
            else:
                ob[pl.ds(off, J, stride=d), :] = o
                ls[pl.ds(off, J, stride=d), :] = lse
            return carry

        lax.fori_loop(0, tq // J, unit, 0, unroll=DIL_UNROLL)

    l0, l1, l2 = ls0[...], ls1[...], ls2[...]
    mx = jnp.maximum(jnp.maximum(l0, l1), l2)
    w0, w1, w2 = jnp.exp(l0 - mx), jnp.exp(l1 - mx), jnp.exp(l2 - mx)
    o_ref[...] = (w0 * ob0[...] + w1 * ob1[...] + w2 * ob2[...]) / (w0 + w1 + w2)


def _dilated(qkv, slopes, batch, seq, n_heads, tq):
    assert seq % tq == 0
    n = batch * seq
    nq = seq // tq
    blk = (tq, HEAD_DIM)
    full = (seq, HEAD_DIM)
    scr = [pltpu.VMEM(blk, F32)] * 6
    return pl.pallas_call(
        functools.partial(_dil_kernel, tq=tq),
        grid=(batch, n_heads, nq),
        in_specs=[
            pl.BlockSpec(memory_space=pltpu.SMEM),
            pl.BlockSpec(blk, lambda b, h, c: (b * nq + c, h)),
            pl.BlockSpec(full, lambda b, h, c: (b, n_heads + h)),
            pl.BlockSpec(full, lambda b, h, c: (b, 2 * n_heads + h)),
        ],
        out_specs=pl.BlockSpec(blk, lambda b, h, c: (b * nq + c, h)),
        out_shape=jax.ShapeDtypeStruct((n, n_heads * HEAD_DIM), F32),
        scratch_shapes=scr,
        compiler_params=_params(("arbitrary",) * 3, 2 * _nbytes(full, F32), 5 * _nbytes(blk, F32)),
        name="dilated_attn",
    )(slopes, qkv, qkv, qkv)


def _sb_kernel(q_ref, k_ref, v_ref, o_ref, *, tq, sub):
    n_chain = tq // sub
    assert sub & (sub - 1) == 0
    kb0 = pl.program_id(2) * n_chain
    row = lax.broadcasted_iota(I32, (sub, sub), 0)
    col = lax.broadcasted_iota(I32, (sub, sub), 1)
    after = (row > col).astype(BF16)
    after2 = jnp.concatenate([after, after], axis=0)
    qrow = lax.broadcasted_iota(I32, (tq, sub), 0)
    causal = lax.broadcasted_iota(I32, (tq, sub), 1) < (qrow & (sub - 1))
    chain_of_row = lax.broadcasted_iota(I32, (tq, 1), 0) >> (sub.bit_length() - 1)
    sign = jnp.uint32(0x80000000)

    def sweep(kbs, drop, acc, diag, first_valid):
        starts = [pl.multiple_of(kb * sub, sub) for kb in kbs]
        z2 = jnp.concatenate(
            [lax.dot_general(q_ref[c * sub:(c + 1) * sub, :], k_ref[pl.ds(starts[c], sub), :],
                             (((1,), (1,)), ((), ())), preferred_element_type=F32) for c in range(n_chain)], axis=0)
        neg_abs = pltpu.bitcast(pltpu.bitcast(z2, U32) | sign, F32)
        sp2 = jnp.maximum(z2, 0.0) + jnp.log2(1.0 + jnp.exp2(neg_abs))
        log_beta2 = z2 - sp2
        if diag:
            sp2 = jnp.where(causal, sp2, 0.0)
        hi = sp2.astype(BF16)
        lo = (sp2 - hi.astype(F32)).astype(BF16)
        later = jnp.dot(jnp.concatenate([hi, lo], axis=1), after2, preferred_element_type=F32)
        a = jnp.exp2(log_beta2 - later)
        if diag:
            a = jnp.where(causal, a, 0.0)
        a = a.astype(BF16)
        pv = jnp.concatenate(
            [jnp.dot(a[c * sub:(c + 1) * sub, :], v_ref[pl.ds(starts[c], sub), :], preferred_element_type=F32)
             for c in range(n_chain)], axis=0)
        scale = jnp.exp2(-drop)
        new_drop = drop + later[:, 0:1] + sp2[:, 0:1]
        if first_valid is not None:
            valid = chain_of_row >= first_valid
            scale = jnp.where(valid, scale, 0.0)
            new_drop = jnp.where(valid, new_drop, drop)
        return new_drop, acc + scale * pv

    def alive(drop):
        return (jnp.max(jnp.exp2(-drop)) > 0.0).astype(I32)

    drop, acc = sweep([kb0 + c for c in range(n_chain)], jnp.zeros((tq, 1), F32),
                      jnp.zeros((tq, HEAD_DIM), F32), True, None)

    def cond(st):
        return jnp.logical_and(st[0] <= kb0 + n_chain - 1, st[1] > 0)

    def body(st):
        g, _, drop, acc = st
        drop, acc = sweep([jnp.maximum(kb0 + c - g, 0) for c in range(n_chain)], drop, acc, False, g - kb0)
        return g + 1, alive(drop), drop, acc

    _, _, _, acc = lax.while_loop(cond, body, (jnp.int32(1), alive(drop), drop, acc))
    o_ref[...] = acc


def _stick_breaking(qkv, batch, seq, n_heads, tq, sub):
    assert seq % tq == 0 and tq % sub == 0
    n = batch * seq
    nq = seq // tq
    blk = (tq, HEAD_DIM)
    full = (seq, HEAD_DIM)
    return pl.pallas_call(
        functools.partial(_sb_kernel, tq=tq, sub=sub),
        grid=(batch, n_heads, nq),
        in_specs=[
            pl.BlockSpec(blk, lambda b, h, i: (b * nq + i, h)),
            pl.BlockSpec(full, lambda b, h, i: (b, n_heads + h)),
            pl.BlockSpec(full, lambda b, h, i: (b, 2 * n_heads + h)),
        ],
        out_specs=pl.BlockSpec(blk, lambda b, h, i: (b * nq + i, h)),
        out_shape=jax.ShapeDtypeStruct((n, n_heads * HEAD_DIM), F32),
        compiler_params=_params(("arbitrary",) * 3, 2 * _nbytes(full, BF16), 2 * _nbytes(blk, F32)),
        name="stick_breaking",
    )(qkv, qkv, qkv)


def _outproj_kernel(od_ref, os_ref, gd_ref, gs_ref, w_ref, x_ref, o_ref, m_scr):
    wd = od_ref.shape[1]

    @pl.when(pl.program_id(1) == 0)
    def _():
        m_scr[:, :wd] = _rms(od_ref[...], gd_ref[...]).astype(BF16)
        m_scr[:, wd:] = _rms(os_ref[...], gs_ref[...]).astype(BF16)

    o_ref[...] = x_ref[...] + jnp.dot(m_scr[...], w_ref[...], preferred_element_type=F32)


def _outproj(o_dil, o_sb, g_dil, g_sb, w, x, tm, tn):
    n, wd = o_dil.shape
    ws = o_sb.shape[1]
    d = x.shape[1]
    return pl.pallas_call(
        _outproj_kernel,
        grid=(n // tm, d // tn),
        in_specs=[
            pl.BlockSpec((tm, wd), lambda i, j: (i, 0)),
            pl.BlockSpec((tm, ws), lambda i, j: (i, 0)),
            pl.BlockSpec((1, wd), lambda i, j: (0, 0)),
            pl.BlockSpec((1, ws), lambda i, j: (0, 0)),
            pl.BlockSpec((wd + ws, tn), lambda i, j: (0, j)),
            pl.BlockSpec((tm, tn), lambda i, j: (i, j)),
        ],
        out_specs=pl.BlockSpec((tm, tn), lambda i, j: (i, j)),
        out_shape=jax.ShapeDtypeStruct((n, d), F32),
        scratch_shapes=[pltpu.VMEM((tm, wd + ws), BF16)],
        compiler_params=_params(("arbitrary", "arbitrary"), _nbytes((tm, wd + ws), F32),
                                _nbytes((wd + ws, tn), BF16), 2 * _nbytes((tm, tn), F32),
                                _nbytes((tm, wd + ws), BF16) // 2),
        name="out_proj",
    )(o_dil, o_sb, g_dil.reshape(1, wd), g_sb.reshape(1, ws), w, x)


def _mem_kv_kernel(m_ref, gm_ref, w_ref, gk_ref, k_ref, v_ref):
    hm = _rms(m_ref[0], gm_ref[...]).astype(BF16)
    kv = jnp.dot(hm, w_ref[...], preferred_element_type=F32)
    wc = k_ref.shape[2]
    for h in range(wc // HEAD_DIM):
        sl = slice(h * HEAD_DIM, (h + 1) * HEAD_DIM)
        k_ref[0, :, sl] = _rms(kv[:, sl], gk_ref[...]).astype(BF16)
    v_ref[0] = kv[:, wc:].astype(BF16)


def _mem_kv(mem, g_mem, w_kv, g_k):
    b, m, d = mem.shape
    wc = w_kv.shape[1] // 2
    out = jax.ShapeDtypeStruct((b, m, wc), BF16)
    return pl.pallas_call(
        _mem_kv_kernel,
        grid=(b,),
        in_specs=[
            pl.BlockSpec((1, m, d), lambda i: (i, 0, 0)),
            pl.BlockSpec((1, d), lambda i: (0, 0)),
            pl.BlockSpec((d, 2 * wc), lambda i: (0, 0)),
            pl.BlockSpec((1, HEAD_DIM), lambda i: (0, 0)),
        ],
        out_specs=[pl.BlockSpec((1, m, wc), lambda i: (i, 0, 0))] * 2,
        out_shape=[out, out],
        compiler_params=_params(("arbitrary",), _nbytes((m, d), F32), _nbytes((d, 2 * wc), BF16)),
        name="mem_kv",
    )(mem, g_mem.reshape(1, d), w_kv, g_k.reshape(1, HEAD_DIM))


def _pack_bf16_pairs(xr):
    half = xr.shape[1] // 2
    lo = pltpu.bitcast(xr[:, :half], U32)
    hi = pltpu.bitcast(xr[:, half:], U32)
    return hi | (lo >> 16)


def _cross_router_kernel(x1_ref, gc_ref, wq_ref, gq_ref, kc_ref, vc_ref, wo_ref, gf_ref, wrh_ref, wrl_ref, br_ref,
                         x2_ref, hp_ref, eid_ref, gate_ref):
    x1 = x1_ref[...]
    h = _rms(x1, gc_ref[...]).astype(BF16)
    q = jnp.dot(h, wq_ref[...], preferred_element_type=F32)
    heads = []
    for hd in range(q.shape[1] // HEAD_DIM):
        sl = slice(hd * HEAD_DIM, (hd + 1) * HEAD_DIM)
        qh = _rms(q[:, sl], gq_ref[...]).astype(BF16)
        s = lax.dot_general(qh, kc_ref[0, :, sl], (((1,), (1,)), ((), ())), preferred_element_type=F32)
        p = jnp.exp(s - jnp.max(s, axis=1, keepdims=True))
        l = jnp.sum(p, axis=1, keepdims=True)
        heads.append(jnp.dot(p.astype(BF16), vc_ref[0, :, sl], preferred_element_type=F32) / l)
    o = jnp.concatenate(heads, axis=1).astype(BF16)
    x2 = x1 + jnp.dot(o, wo_ref[...], preferred_element_type=F32)
    x2_ref[...] = x2

    hf = _rms(x2, gf_ref[...])
    hf_hi = hf.astype(BF16)
    hf_hi32 = hf_hi.astype(F32)
    hf_lo = (hf - hf_hi32).astype(BF16)
    packed = _pack_bf16_pairs(hf_hi32)
    tm = packed.shape[0]
    nb = packed.shape[1] // LANE
    for j in range(nb):
        hp_ref[pl.ds(j, tm, stride=nb), :] = packed[:, j * LANE:(j + 1) * LANE]

    n_exp = N_GROUPS * EXPERTS_PER_GROUP
    lg = (jnp.dot(hf_hi, wrh_ref[...], preferred_element_type=F32)
          + jnp.dot(hf_hi, wrl_ref[...], preferred_element_type=F32)
          + jnp.dot(hf_lo, wrh_ref[...], preferred_element_type=F32)) + br_ref[...]
    lane = lax.broadcasted_iota(I32, lg.shape, 1)
    lanef = lane.astype(F32)
    ninf = -jnp.inf
    is_g = (lane >= n_exp) & (lane < n_exp + N_GROUPS)
    gl = jnp.where(is_g, lg, ninf)
    gmax = jnp.max(gl, axis=1, keepdims=True)
    gidx = jnp.min(jnp.where(gl == gmax, lanef, 1e9), axis=1, keepdims=True) - n_exp
    g_gate = 1.0 / jnp.sum(jnp.exp(gl - gmax), axis=1, keepdims=True)
    in_grp = (lane < n_exp) & ((lane // EXPERTS_PER_GROUP).astype(F32) == gidx)
    el = jnp.where(in_grp, lg, ninf)
    m1 = jnp.max(el, axis=1, keepdims=True)
    i1 = jnp.min(jnp.where(el == m1, lanef, 1e9), axis=1, keepdims=True)
    el2 = jnp.where(lanef == i1, ninf, el)
    m2 = jnp.max(el2, axis=1, keepdims=True)
    i2 = jnp.min(jnp.where(el2 == m2, lanef, 1e9), axis=1, keepdims=True)
    p2 = jnp.exp(m2 - m1)
    den = 1.0 + p2
    eid_ref[...] = jnp.where(lane == 0, i1, jnp.where(lane == 1, i2, 0.0)).astype(I32)
    gate_ref[...] = jnp.where(lane == 0, g_gate / den, jnp.where(lane == 1, g_gate * p2 / den, 0.0))


def _cross_router(x1, g_cross, wq, gq, kc, vc, wo, g_ffn, wr_hi, wr_lo, br, seq, tm):
    n, d = x1.shape
    wc = wq.shape[1]
    m = kc.shape[1]
    tiles_per_batch = seq // tm
    const = lambda i: (0, 0)
    return pl.pallas_call(
        _cross_router_kernel,
        grid=(n // tm,),
        in_specs=[
            pl.BlockSpec((tm, d), lambda i: (i, 0)),
            pl.BlockSpec((1, d), const),
            pl.BlockSpec((d, wc), const),
            pl.BlockSpec((1, HEAD_DIM), const),
            pl.BlockSpec((1, m, wc), lambda i: (i // tiles_per_batch, 0, 0)),
            pl.BlockSpec((1, m, wc), lambda i: (i // tiles_per_batch, 0, 0)),
            pl.BlockSpec((wc, d), const),
            pl.BlockSpec((1, d), const),
            pl.BlockSpec((d, LANE), const),
            pl.BlockSpec((d, LANE), const),
            pl.BlockSpec((1, LANE), const),
        ],
        out_specs=[
            pl.BlockSpec((tm, d), lambda i: (i, 0)),
            pl.BlockSpec((tm * (d // 2 // LANE), LANE), lambda i: (i, 0)),
            pl.BlockSpec((tm, LANE), lambda i: (i, 0)),
            pl.BlockSpec((tm, LANE), lambda i: (i, 0)),
        ],
        out_shape=[
            jax.ShapeDtypeStruct((n, d), F32),
            jax.ShapeDtypeStruct((n * (d // 2 // LANE), LANE), U32),
            jax.ShapeDtypeStruct((n, LANE), I32),
            jax.ShapeDtypeStruct((n, LANE), F32),
        ],
        compiler_params=_params(("arbitrary",), 2 * _nbytes((tm, d), F32), _nbytes((tm, d // 2), U32),
                                2 * _nbytes((d, wc), BF16), 2 * _nbytes((d, LANE), BF16)),
        name="cross_router",
    )(x1, g_cross.reshape(1, d), wq, gq.reshape(1, HEAD_DIM), kc, vc, wo, g_ffn.reshape(1, d), wr_hi, wr_lo, br)


def _weight_copies(pairs, sem, n_split):
    cps = []
    for src, dst in pairs:
        rows = src.shape[0] // n_split
        for i in range(n_split):
            sl = pl.ds(i * rows, rows)
            cps.append((pltpu.make_async_copy(src.at[sl], dst.at[sl], sem), i % 2))
    return cps


def _moe_up_kernel(nused_ref, te_ref, tblk_ref, tok_ref, tok_next_ref, hp_ref, wg_ref, wu_ref, h_ref,
                   x_scr, stage0, stage1, g_acc, u_acc, wg_buf, wu_buf, sem0, sem1, wsem, *, rt, nb, kd):
    t = pl.program_id(0)
    k = pl.program_id(1)
    n_used = nused_ref[0]
    stages, sems = (stage0, stage1), (sem0, sem1)
    half = nb * LANE
    himask = jnp.uint32(0xFFFF0000)

    def start_tile(tok, slot):
        def body(r, c):
            pltpu.make_async_copy(hp_ref.at[pl.ds(tok[0, 0, r] * nb, nb)],
                                  stages[slot].at[pl.ds(r * nb, nb)], sems[slot]).start(priority=1)
            return c

        lax.fori_loop(0, rt, body, 0)

    def put(col, val):
        x_scr[col // kd, :, col % kd:col % kd + LANE] = val.astype(BF16)

    nk = pl.num_programs(1)
    step = t * nk + k
    wslot = step % 2

    def weights(tile, chunk, slot):
        e = te_ref[tile]
        rows = pl.ds(chunk * kd, kd)
        return _weight_copies([(wg_ref.at[e, rows], wg_buf.at[slot]), (wu_ref.at[e, rows], wu_buf.at[slot])],
                              wsem.at[slot], 2)

    @pl.when(t < n_used)
    def _():
        @pl.when(step == 0)
        def _():
            for cp, pri in weights(0, 0, 0):
                cp.start(priority=pri)

        last_chunk = k + 1 == nk
        t_next = jnp.where(last_chunk, t + 1, t)

        @pl.when(t_next < n_used)
        def _():
            for cp, pri in weights(t_next, jnp.where(last_chunk, 0, k + 1), 1 - wslot):
                cp.start(priority=pri)

        for slot in range(2):
            @pl.when(jnp.logical_and(k == 0, t % 2 == slot))
            def _(slot=slot):
                if slot == 0:
                    @pl.when(t == 0)
                    def _():
                        start_tile(tok_ref, 0)

                @pl.when(t + 1 < n_used)
                def _():
                    start_tile(tok_next_ref, 1 - slot)

                pltpu.make_async_copy(hp_ref.at[pl.ds(0, rt * nb)], stages[slot], sems[slot]).wait()
                for j in range(nb):
                    w = stages[slot][pl.ds(j, rt, stride=nb), :]
                    put(j * LANE, pltpu.bitcast(w << 16, F32))
                    put(half + j * LANE, pltpu.bitcast(w & himask, F32))

        for cp, _ in weights(t, k, wslot):
            cp.wait()
        x = x_scr[k]
        g = jnp.dot(x, wg_buf[wslot].astype(BF16), preferred_element_type=F32)
        u = jnp.dot(x, wu_buf[wslot].astype(BF16), preferred_element_type=F32)

        @pl.when(k == 0)
        def _():
            g_acc[...] = g
            u_acc[...] = u

        @pl.when(k > 0)
        def _():
            g_acc[...] += g
            u_acc[...] += u

        @pl.when(k == pl.num_programs(1) - 1)
        def _():
            gg = g_acc[...]
            h_ref[...] = (gg * jax.nn.sigmoid(gg) * u_acc[...]).astype(BF16)


def _moe_down_kernel(nused_ref, te_ref, tblk_ref, h_ref, wd_ref, y_ref, wd_buf, wsem, *, tn):
    t = pl.program_id(0)
    j = pl.program_id(1)
    n_used = nused_ref[0]
    nn = pl.num_programs(1)
    step = t * nn + j
    wslot = step % 2

    def weights(tile, col_tile, slot):
        src = wd_ref.at[te_ref[tile], :, pl.ds(col_tile * tn, tn)]
        return _weight_copies([(src, wd_buf.at[slot])], wsem.at[slot], 2)

    @pl.when(t < n_used)
    def _():
        @pl.when(step == 0)
        def _():
            for cp, pri in weights(0, 0, 0):
                cp.start(priority=pri)

        last_col = j + 1 == nn
        t_next = jnp.where(last_col, t + 1, t)

        @pl.when(t_next < n_used)
        def _():
            for cp, pri in weights(t_next, jnp.where(last_col, 0, j + 1), 1 - wslot):
                cp.start(priority=pri)

        for cp, _ in weights(t, j, wslot):
            cp.wait()
        y_ref[...] = jnp.dot(h_ref[...], wd_buf[wslot].astype(BF16), preferred_element_type=F32)


def _moe_experts(n_used, tile_e, tile_blk, slot_tok, hp, w_gate, w_up, w_down, n, n_tiles, rt, kd, tn):
    n_exp, d, de = w_gate.shape
    nb = hp.shape[0] // n
    assert 2 * nb * LANE == d and kd % LANE == 0 and hp.shape[0] >= rt * nb
    p_rows = n_tiles * rt
    nk = d // kd
    nn = d // tn
    tok3 = slot_tok.reshape(n_tiles, 1, rt)

    def inner(t, k, nused, last):
        return jnp.where(t < nused[0], k, last)

    stage = pltpu.VMEM((rt * nb, LANE), U32)
    h_mid = pl.pallas_call(
        functools.partial(_moe_up_kernel, rt=rt, nb=nb, kd=kd),
        grid_spec=pltpu.PrefetchScalarGridSpec(
            num_scalar_prefetch=3,
            grid=(n_tiles, nk),
            in_specs=[
                pl.BlockSpec((1, 1, rt), lambda t, k, nu, te, tb: (t, 0, 0), memory_space=pltpu.SMEM),
                pl.BlockSpec((1, 1, rt), lambda t, k, nu, te, tb: (jnp.minimum(t + 1, n_tiles - 1), 0, 0),
                             memory_space=pltpu.SMEM),
                pl.BlockSpec(memory_space=pl.ANY),
                pl.BlockSpec(memory_space=pl.ANY),
                pl.BlockSpec(memory_space=pl.ANY),
            ],
            out_specs=pl.BlockSpec((rt, de), lambda t, k, nu, te, tb: (tb[t], 0)),
            scratch_shapes=[pltpu.VMEM((nk, rt, kd), BF16), stage, stage,
                            pltpu.VMEM((rt, de), F32), pltpu.VMEM((rt, de), F32),
                            pltpu.VMEM((2, kd, de), F32), pltpu.VMEM((2, kd, de), F32),
                            pltpu.SemaphoreType.DMA(()), pltpu.SemaphoreType.DMA(()),
                            pltpu.SemaphoreType.DMA((2,))],
        ),
        out_shape=jax.ShapeDtypeStruct((p_rows, de), BF16),
        compiler_params=_params(("arbitrary", "arbitrary"), 2 * _nbytes((kd, de), F32), _nbytes((rt, de), F32),
                                _nbytes((rt * nb, LANE), U32), _nbytes((rt, d), BF16) // 2),
        name="moe_gate_up",
    )(n_used, tile_e, tile_blk, tok3, tok3, hp, w_gate, w_up)

    return pl.pallas_call(
        functools.partial(_moe_down_kernel, tn=tn),
        grid_spec=pltpu.PrefetchScalarGridSpec(
            num_scalar_prefetch=3,
            grid=(n_tiles, nn),
            in_specs=[
                pl.BlockSpec((rt, de), lambda t, j, nu, te, tb: (tb[t], 0)),
                pl.BlockSpec(memory_space=pl.ANY),
            ],
            out_specs=pl.BlockSpec((rt, tn), lambda t, j, nu, te, tb: (tb[t], inner(t, j, nu, nn - 1))),
            scratch_shapes=[pltpu.VMEM((2, de, tn), F32), pltpu.SemaphoreType.DMA((2,))],
        ),
        out_shape=jax.ShapeDtypeStruct((p_rows, d), F32),
        compiler_params=_params(("arbitrary", "arbitrary"), _nbytes((rt, de), BF16), _nbytes((de, tn), F32),
                                _nbytes((rt, tn), F32)),
        name="moe_down",
    )(n_used, tile_e, tile_blk, h_mid, w_down)


def _moe_combine_kernel(pos_ref, pos_next_ref, x_ref, g_ref, y_ref, o_ref, buf0, buf1, sem0, sem1, *, tm, top_k):
    i = pl.program_id(0)
    bufs, sems = (buf0, buf1), (sem0, sem1)

    def start_step(pos, slot):
        def body(r, c):
            for k in range(top_k):
                pltpu.make_async_copy(y_ref.at[pl.ds(pos[0, 0, top_k * r + k], 1)],
                                      bufs[slot].at[pl.ds(k * tm + r, 1)], sems[slot]).start(priority=k % 2)
            return c

        lax.fori_loop(0, tm, body, 0)

    for slot in range(2):
        @pl.when(i % 2 == slot)
        def _(slot=slot):
            if slot == 0:
                @pl.when(i == 0)
                def _():
                    start_step(pos_ref, 0)

            @pl.when(i + 1 < pl.num_programs(0))
            def _():
                start_step(pos_next_ref, 1 - slot)

            pltpu.make_async_copy(y_ref.at[pl.ds(0, top_k * tm)], bufs[slot], sems[slot]).wait()
            acc = x_ref[...]
            for k in range(top_k):
                acc = acc + g_ref[:, k:k + 1] * bufs[slot][k * tm:(k + 1) * tm, :]
            o_ref[...] = acc


def _moe_combine(dest, x2, gate, y, tm, top_k):
    n, d = x2.shape
    steps = n // tm
    assert y.shape[0] >= top_k * tm
    pos3 = dest.reshape(steps, 1, tm * top_k)
    stage = pltpu.VMEM((top_k * tm, d), F32)
    return pl.pallas_call(
        functools.partial(_moe_combine_kernel, tm=tm, top_k=top_k),
        grid=(steps,),
        in_specs=[
            pl.BlockSpec((1, 1, tm * top_k), lambda i: (i, 0, 0), memory_space=pltpu.SMEM),
            pl.BlockSpec((1, 1, tm * top_k), lambda i: (jnp.minimum(i + 1, steps - 1), 0, 0),
                         memory_space=pltpu.SMEM),
            pl.BlockSpec((tm, d), lambda i: (i, 0)),
            pl.BlockSpec((tm, LANE), lambda i: (i, 0)),
            pl.BlockSpec(memory_space=pl.ANY),
        ],
        out_specs=pl.BlockSpec((tm, d), lambda i: (i, 0)),
        out_shape=jax.ShapeDtypeStruct((n, d), F32),
        scratch_shapes=[stage, stage, pltpu.SemaphoreType.DMA(()), pltpu.SemaphoreType.DMA(())],
        compiler_params=_params(("arbitrary",), 2 * _nbytes((tm, d), F32), _nbytes((top_k * tm, d), F32)),
        name="moe_combine",
    )(pos3, pos3, x2, gate, y)


def _moe_schedule(eid, n_exp, rt, n_tiles):
    n, top_k = eid.shape
    a = n * top_k
    eflat = eid.reshape(a)
    onehot = (eflat[:, None] == jnp.arange(n_exp, dtype=I32)[None, :]).astype(I32)
    csum = jnp.cumsum(onehot, axis=0)
    rank = jnp.sum(csum * onehot, axis=1) - 1
    counts = csum[-1]
    nt = (counts + rt - 1) // rt
    tcum = jnp.cumsum(nt)
    tstart = tcum - nt
    n_used = tcum[-1]
    dest = tstart[eflat] * rt + rank
    tid = jnp.arange(n_tiles, dtype=I32)
    tile_blk = jnp.minimum(tid, n_used - 1)
    tile_e = jnp.minimum(jnp.searchsorted(tcum, tile_blk, side="right"), n_exp - 1).astype(I32)
    tok = jnp.arange(a, dtype=I32) // top_k
    slot_tok = jnp.zeros((n_tiles * rt,), I32).at[dest].set(tok)
    return n_used.reshape(1).astype(I32), tile_e, tile_blk.astype(I32), slot_tok, dest.astype(I32)


def _pick(n, pref):
    t = min(n, pref)
    assert n % t == 0
    return t


def kernel(x, mem, g_mix, w_in, g_q_dil, g_k_dil, g_out_dil, g_out_sb, w_out, g_cross, g_mem, w_q_cross, w_kv_cross, g_q_cross, g_k_cross, w_o_cross, g_ffn, w_group, b_group, w_router, b_router, w_exp_gate, w_exp_up, w_exp_down):
    batch, seq, d = x.shape
    n = batch * seq
    n_heads_mix = d // HEAD_DIM
    n_dil = n_heads_mix // 2
    n_sb = n_heads_mix - n_dil
    w_dil, w_sb = n_dil * HEAD_DIM, n_sb * HEAD_DIM
    n_exp = w_exp_gate.shape[1]
    top_k = 2
    scale = 1.0 / math.sqrt(HEAD_DIM)
    slopes = jnp.exp2(-ALIBI_MAX_BIAS * jnp.arange(1, n_dil + 1, dtype=F32) / n_dil)

    tm = _pick(n, 512)
    rt = -(-(9 * n * top_k) // (8 * n_exp * MOE_ROW_ALIGN)) * MOE_ROW_ALIGN
    n_tiles = -(-n * top_k // rt) + n_exp
    kd = max(d // 4, LANE)

    xf = x.reshape(n, d)
    for l in range(g_mix.shape[0]):
        h = _rmsnorm_cast(xf, g_mix[l], tm)
        w_in_b = w_in[l].astype(BF16)
        gain_dil = jnp.concatenate([jnp.tile(g_q_dil[l] * scale, n_dil), jnp.tile(g_k_dil[l], n_dil),
                                    jnp.ones((w_dil,), F32)])
        gain_sb = jnp.concatenate([jnp.full((w_sb,), scale * math.log2(math.e), F32), jnp.ones((2 * w_sb,), F32)])
        tn = _pick(w_dil, 1024)
        tmp = _pick(n, 1024)
        qkv_dil = _proj(h, w_in_b, gain_dil, 0, 3 * w_dil, 2 * w_dil, F32, tmp, tn, "in_proj_dil")
        qkv_sb = _proj(h, w_in_b, gain_sb, 3 * w_dil, 3 * w_sb, 0, BF16, tmp, tn, "in_proj_sb")
        o_dil = _dilated(qkv_dil, slopes, batch, seq, n_dil, _pick(seq, 2048))
        o_sb = _stick_breaking(qkv_sb, batch, seq, n_sb, _pick(seq, 1024), _pick(seq, 256))
        x1 = _outproj(o_dil, o_sb, g_out_dil[l], g_out_sb[l], w_out[l].astype(BF16), xf, tm, _pick(d, 1024))

        kc, vc = _mem_kv(mem, g_mem[l], w_kv_cross[l].astype(BF16), g_k_cross[l])
        wr = jnp.concatenate([w_router[l], w_group[l],
                              jnp.zeros((d, LANE - n_exp - N_GROUPS), F32)], axis=1)
        wr_hi = wr.astype(BF16)
        wr_lo = (wr - wr_hi.astype(F32)).astype(BF16)
        br = jnp.concatenate([b_router[l].reshape(-1), b_group[l],
                              jnp.zeros((LANE - n_exp - N_GROUPS,), F32)]).reshape(1, LANE)
        x2, hp, eid, gate = _cross_router(x1, g_cross[l], w_q_cross[l].astype(BF16), g_q_cross[l] * scale, kc, vc,
                                          w_o_cross[l].astype(BF16), g_ffn[l], wr_hi, wr_lo, br, seq,
                                          _pick(seq, 256))

        n_used, tile_e, tile_blk, slot_tok, dest = _moe_schedule(eid[:, :top_k], n_exp, rt, n_tiles)
        y = _moe_experts(n_used, tile_e, tile_blk, slot_tok, hp, w_exp_gate[l], w_exp_up[l], w_exp_down[l],
                         n, n_tiles, rt, kd, _pick(d, 1024))
        xf = _moe_combine(dest, x2, gate, y, _pick(n, 128), top_k)
    return xf.reshape(batch, seq, d)
```

```python
import functools
import math

import jax
import jax.numpy as jnp
from jax import lax
from jax.experimental import pallas as pl
from jax.experimental.pallas import tpu as pltpu

F32 = jnp.float32
BF16 = jnp.bfloat16
I32 = jnp.int32
U32 = jnp.uint32

LANE = 128
HEAD_DIM = 128
EPS = 1e-6
DIL_PAIRS = ((128, 1), (512, 4), (2048, 16))
DIL_STEPS = 128
DIL_UNROLL = 16
ALIBI_MAX_BIAS = 8.0
N_HEADS_CROSS = 4
N_GROUPS = 8
EXPERTS_PER_GROUP = 8
MOE_ROW_ALIGN = 64
NEG_BIG = -1e30
VMEM_LIMIT_CAP = 60000 * 1024
VMEM_INTERNAL = 12 * 1024 * 1024


def _vmem_limit(*block_bytes):
    return int(min(VMEM_LIMIT_CAP, 2 * sum(block_bytes) + VMEM_INTERNAL))


def _nbytes(shape, dtype):
    return math.prod(shape) * jnp.dtype(dtype).itemsize


def _params(sem, *block_bytes):
    return pltpu.CompilerParams(dimension_semantics=sem, vmem_limit_bytes=_vmem_limit(*block_bytes))


def _rms(x, g):
    return x * lax.rsqrt(jnp.mean(x * x, axis=-1, keepdims=True) + EPS) * g


def _rmsnorm_cast_kernel(x_ref, g_ref, o_ref):
    o_ref[...] = _rms(x_ref[...], g_ref[...]).astype(o_ref.dtype)


def _rmsnorm_cast(x, g, tm):
    n, d = x.shape
    return pl.pallas_call(
        _rmsnorm_cast_kernel,
        grid=(n // tm,),
        in_specs=[pl.BlockSpec((tm, d), lambda i: (i, 0)), pl.BlockSpec((1, d), lambda i: (0, 0))],
        out_specs=pl.BlockSpec((tm, d), lambda i: (i, 0)),
        out_shape=jax.ShapeDtypeStruct((n, d), BF16),
        compiler_params=_params(("arbitrary",), _nbytes((tm, d), F32), _nbytes((tm, d), BF16)),
        name="rmsnorm_cast",
    )(x, g.reshape(1, d))


def _proj_kernel(a_ref, w_ref, g_ref, o_ref, *, n_norm_tiles):
    j = pl.program_id(1)
    acc = jnp.dot(a_ref[...], w_ref[...], preferred_element_type=F32)
    tn = acc.shape[1]

    @pl.when(j < n_norm_tiles)
    def _():
        for h in range(tn // HEAD_DIM):
            sl = slice(h * HEAD_DIM, (h + 1) * HEAD_DIM)
            o_ref[:, sl] = _rms(acc[:, sl], g_ref[:, sl]).astype(o_ref.dtype)

    @pl.when(j >= n_norm_tiles)
    def _():
        o_ref[...] = (acc * g_ref[...]).astype(o_ref.dtype)


def _proj(a, w, gain, col0, ncols, norm_cols, out_dtype, tm, tn, name):
    n, k = a.shape
    assert col0 % tn == 0 and ncols % tn == 0 and norm_cols % tn == 0 and n % tm == 0
    jb = col0 // tn
    return pl.pallas_call(
        functools.partial(_proj_kernel, n_norm_tiles=norm_cols // tn),
        grid=(n // tm, ncols // tn),
        in_specs=[
            pl.BlockSpec((tm, k), lambda i, j: (i, 0)),
            pl.BlockSpec((k, tn), lambda i, j: (0, j + jb)),
            pl.BlockSpec((1, tn), lambda i, j: (0, j)),
        ],
        out_specs=pl.BlockSpec((tm, tn), lambda i, j: (i, j)),
        out_shape=jax.ShapeDtypeStruct((n, ncols), out_dtype),
        compiler_params=_params(("arbitrary", "arbitrary"), _nbytes((tm, k), BF16), _nbytes((k, tn), BF16),
                                _nbytes((tm, tn), F32)),
        name=name,
    )(a, w, gain.reshape(1, ncols))


def _dil_kernel(slope_ref, q_ref, k_ref, v_ref, o_ref, ob0, ob1, ob2, ls0, ls1, ls2, *, tq):
    h = pl.program_id(1)
    t0 = pl.program_id(2) * tq
    slope = slope_ref[h]
    J = DIL_STEPS
    qi = lax.broadcasted_iota(I32, (J, 2 * J), 0)
    ki = lax.broadcasted_iota(I32, (J, 2 * J), 1)
    dist = J + qi - ki
    valid = (dist >= 0) & (dist <= J)
    distf = dist.astype(F32)
    in_cur = ki >= J
    ones_v = jnp.ones((2 * J, HEAD_DIM), BF16)

    for (window, d), ob, ls in zip(DIL_PAIRS, (ob0, ob1, ob2), (ls0, ls1, ls2)):
        assert window // d == J and tq % (d * J) == 0
        bias = jnp.where(valid, (-slope * d) * distf, NEG_BIG)
        shift = d.bit_length() - 1

        def ld(ref, start, d=d):
            if d == 1:
                return ref[pl.ds(start, J), :]
            return ref[pl.ds(start, J, stride=d), :]

        def unit(u, carry, d=d, shift=shift, bias=bias, ob=ob, ls=ls, ld=ld):
            off = (u & (d - 1)) + (u >> shift) * (d * J)
            cur0 = t0 + off
            prev0 = cur0 - d * J
            has_prev = prev0 >= 0
            prev0 = jnp.maximum(prev0, 0)
            qs = ld(q_ref, off).astype(BF16)
            kcat = jnp.concatenate([ld(k_ref, prev0), ld(k_ref, cur0)], axis=0).astype(BF16)
            vcat = jnp.concatenate([ld(v_ref, prev0), ld(v_ref, cur0)], axis=0).astype(BF16)
            s = lax.dot_general(qs, kcat, (((1,), (1,)), ((), ())), preferred_element_type=F32) + bias
            s = jnp.where(jnp.logical_or(in_cur, has_prev), s, NEG_BIG)
            m = jnp.max(s, axis=1, keepdims=True)
            p = jnp.exp(s - m)
            ov = jnp.dot(p.astype(BF16), jnp.concatenate([vcat, ones_v], axis=1), preferred_element_type=F32)
            l = ov[:, HEAD_DIM:]
            o = ov[:, :HEAD_DIM] / l
            lse = m + jnp.log(l)
            if d == 1:
                ob[pl.ds(off, J), :] = o
                ls[pl.ds(off, J), :] = lse
            else:
                ob[pl.ds(off, J, stride=d), :] = o
                ls[pl.ds(off, J, stride=d), :] = lse
            return carry

        lax.fori_loop(0, tq // J, unit, 0, unroll=DIL_UNROLL)

    l0, l1, l2 = ls0[...], ls1[...], ls2[...]
    mx = jnp.maximum(jnp.maximum(l0, l1), l2)
    w0, w1, w2 = jnp.exp(l0 - mx), jnp.exp(l1 - mx), jnp.exp(l2 - mx)
    o_ref[...] = (w0 * ob0[...] + w1 * ob1[...] + w2 * ob2[...]) / (w0 + w1 + w2)


def _dilated(qkv, slopes, batch, seq, n_heads, tq):
    assert seq % tq == 0
    n = batch * seq
    nq = seq // tq
    blk = (tq, HEAD_DIM)
    full = (seq, HEAD_DIM)
    scr = [pltpu.VMEM(blk, F32)] * 6
    return pl.pallas_call(
        functools.partial(_dil_kernel, tq=tq),
        grid=(batch, n_heads, nq),
        in_specs=[
            pl.BlockSpec(memory_space=pltpu.SMEM),
            pl.BlockSpec(blk, lambda b, h, c: (b * nq + c, h)),
            pl.BlockSpec(full, lambda b, h, c: (b, n_heads + h)),
            pl.BlockSpec(full, lambda b, h, c: (b, 2 * n_heads + h)),
        ],
        out_specs=pl.BlockSpec(blk, lambda b, h, c: (b * nq + c, h)),
        out_shape=jax.ShapeDtypeStruct((n, n_heads * HEAD_DIM), F32),
        scratch_shapes=scr,
        compiler_params=_params(("arbitrary",) * 3, 2 * _nbytes(full, F32), 5 * _nbytes(blk, F32)),
        name="dilated_attn",
    )(slopes, qkv, qkv, qkv)


def _sb_kernel(q_ref, k_ref, v_ref, o_ref, *, tq, sub):
    n_chain = tq // sub
    assert sub & (sub - 1) == 0
    kb0 = pl.program_id(2) * n_chain
    row = lax.broadcasted_iota(I32, (sub, sub), 0)
    col = lax.broadcasted_iota(I32, (sub, sub), 1)
    after = (row > col).astype(BF16)
    after2 = jnp.concatenate([after, after], axis=0)
    qrow = lax.broadcasted_iota(I32, (tq, sub), 0)
    causal = lax.broadcasted_iota(I32, (tq, sub), 1) < (qrow & (sub - 1))
    chain_of_row = lax.broadcasted_iota(I32, (tq, 1), 0) >> (sub.bit_length() - 1)
    sign = jnp.uint32(0x80000000)

    def sweep(kbs, drop, acc, diag, first_valid):
        starts = [pl.multiple_of(kb * sub, sub) for kb in kbs]
        z2 = jnp.concatenate(
            [lax.dot_general(q_ref[c * sub:(c + 1) * sub, :], k_ref[pl.ds(starts[c], sub), :],
                             (((1,), (1,)), ((), ())), preferred_element_type=F32) for c in range(n_chain)], axis=0)
        neg_abs = pltpu.bitcast(pltpu.bitcast(z2, U32) | sign, F32)
        sp2 = jnp.maximum(z2, 0.0) + jnp.log2(1.0 + jnp.exp2(neg_abs))
        log_beta2 = z2 - sp2
        if diag:
            sp2 = jnp.where(causal, sp2, 0.0)
        hi = sp2.astype(BF16)
        lo = (sp2 - hi.astype(F32)).astype(BF16)
        later = jnp.dot(jnp.concatenate([hi, lo], axis=1), after2, preferred_element_type=F32)
        a = jnp.exp2(log_beta2 - later)
        if diag:
            a = jnp.where(causal, a, 0.0)
        a = a.astype(BF16)
        pv = jnp.concatenate(
            [jnp.dot(a[c * sub:(c + 1) * sub, :], v_ref[pl.ds(starts[c], sub), :], preferred_element_type=F32)
             for c in range(n_chain)], axis=0)
        scale = jnp.exp2(-drop)
        new_drop = drop + later[:, 0:1] + sp2[:, 0:1]
        if first_valid is not None:
            valid = chain_of_row >= first_valid
            scale = jnp.where(valid, scale, 0.0)
            new_drop = jnp.where(valid, new_drop, drop)
        return new_drop, acc + scale * pv

    def alive(drop):
        return (jnp.max(jnp.exp2(-drop)) > 0.0).astype(I32)

    drop, acc = sweep([kb0 + c for c in range(n_chain)], jnp.zeros((tq, 1), F32),
                      jnp.zeros((tq, HEAD_DIM), F32), True, None)

    def cond(st):
        return jnp.logical_and(st[0] <= kb0 + n_chain - 1, st[1] > 0)

    def body(st):
        g, _, drop, acc = st
        drop, acc = sweep([jnp.maximum(kb0 + c - g, 0) for c in range(n_chain)], drop, acc, False, g - kb0)
        return g + 1, alive(drop), drop, acc

    _, _, _, acc = lax.while_loop(cond, body, (jnp.int32(1), alive(drop), drop, acc))
    o_ref[...] = acc


def _stick_breaking(qkv, batch, seq, n_heads, tq, sub):
    assert seq % tq == 0 and tq % sub == 0
    n = batch * seq
    nq = seq // tq
    blk = (tq, HEAD_DIM)
    full = (seq, HEAD_DIM)
    return pl.pallas_call(
        functools.partial(_sb_kernel, tq=tq, sub=sub),
        grid=(batch, n_heads, nq),
        in_specs=[
            pl.BlockSpec(blk, lambda b, h, i: (b * nq + i, h)),
            pl.BlockSpec(full, lambda b, h, i: (b, n_heads + h)),
            pl.BlockSpec(full, lambda b, h, i: (b, 2 * n_heads + h)),
        ],
        out_specs=pl.BlockSpec(blk, lambda b, h, i: (b * nq + i, h)),
        out_shape=jax.ShapeDtypeStruct((n, n_heads * HEAD_DIM), F32),
        compiler_params=_params(("arbitrary",) * 3, 2 * _nbytes(full, BF16), 2 * _nbytes(blk, F32)),
        name="stick_breaking",
    )(qkv, qkv, qkv)


def _outproj_kernel(od_ref, os_ref, gd_ref, gs_ref, w_ref, x_ref, o_ref, m_scr):
    wd = od_ref.shape[1]

    @pl.when(pl.program_id(1) == 0)
    def _():
        m_scr[:, :wd] = _rms(od_ref[...], gd_ref[...]).astype(BF16)
        m_scr[:, wd:] = _rms(os_ref[...], gs_ref[...]).astype(BF16)

    o_ref[...] = x_ref[...] + jnp.dot(m_scr[...], w_ref[...], preferred_element_type=F32)


def _outproj(o_dil, o_sb, g_dil, g_sb, w, x, tm, tn):
    n, wd = o_dil.shape
    ws = o_sb.shape[1]
    d = x.shape[1]
    return pl.pallas_call(
        _outproj_kernel,
        grid=(n // tm, d // tn),
        in_specs=[
            pl.BlockSpec((tm, wd), lambda i, j: (i, 0)),
            pl.BlockSpec((tm, ws), lambda i, j: (i, 0)),
            pl.BlockSpec((1, wd), lambda i, j: (0, 0)),
            pl.BlockSpec((1, ws), lambda i, j: (0, 0)),
            pl.BlockSpec((wd + ws, tn), lambda i, j: (0, j)),
            pl.BlockSpec((tm, tn), lambda i, j: (i, j)),
        ],
        out_specs=pl.BlockSpec((tm, tn), lambda i, j: (i, j)),
        out_shape=jax.ShapeDtypeStruct((n, d), F32),
        scratch_shapes=[pltpu.VMEM((tm, wd + ws), BF16)],
        compiler_params=_params(("arbitrary", "arbitrary"), _nbytes((tm, wd + ws), F32),
                                _nbytes((wd + ws, tn), BF16), 2 * _nbytes((tm, tn), F32),
                                _nbytes((tm, wd + ws), BF16) // 2),
        name="out_proj",
    )(o_dil, o_sb, g_dil.reshape(1, wd), g_sb.reshape(1, ws), w, x)


def _mem_kv_kernel(m_ref, gm_ref, w_ref, gk_ref, k_ref, v_ref):
    hm = _rms(m_ref[0], gm_ref[...]).astype(BF16)
    kv = jnp.dot(hm, w_ref[...], preferred_element_type=F32)
    wc = k_ref.shape[2]
    for h in range(wc // HEAD_DIM):
        sl = slice(h * HEAD_DIM, (h + 1) * HEAD_DIM)
        k_ref[0, :, sl] = _rms(kv[:, sl], gk_ref[...]).astype(BF16)
    v_ref[0] = kv[:, wc:].astype(BF16)


def _mem_kv(mem, g_mem, w_kv, g_k):
    b, m, d = mem.shape
    wc = w_kv.shape[1] // 2
    out = jax.ShapeDtypeStruct((b, m, wc), BF16)
    return pl.pallas_call(
        _mem_kv_kernel,
        grid=(b,),
        in_specs=[
            pl.BlockSpec((1, m, d), lambda i: (i, 0, 0)),
            pl.BlockSpec((1, d), lambda i: (0, 0)),
            pl.BlockSpec((d, 2 * wc), lambda i: (0, 0)),
            pl.BlockSpec((1, HEAD_DIM), lambda i: (0, 0)),
        ],
        out_specs=[pl.BlockSpec((1, m, wc), lambda i: (i, 0, 0))] * 2,
        out_shape=[out, out],
        compiler_params=_params(("arbitrary",), _nbytes((m, d), F32), _nbytes((d, 2 * wc), BF16)),
        name="mem_kv",
    )(mem, g_mem.reshape(1, d), w_kv, g_k.reshape(1, HEAD_DIM))


def _pack_bf16_pairs(xr):
    half = xr.shape[1] // 2
    lo = pltpu.bitcast(xr[:, :half], U32)
    hi = pltpu.bitcast(xr[:, half:], U32)
    return hi | (lo >> 16)


def _cross_router_kernel(x1_ref, gc_ref, wq_ref, gq_ref, kc_ref, vc_ref, wo_ref, gf_ref, wrh_ref, wrl_ref, br_ref,
                         x2_ref, hp_ref, eid_ref, gate_ref):
    x1 = x1_ref[...]
    h = _rms(x1, gc_ref[...]).astype(BF16)
    q = jnp.dot(h, wq_ref[...], preferred_element_type=F32)
    heads = []
    for hd in range(q.shape[1] // HEAD_DIM):
        sl = slice(hd * HEAD_DIM, (hd + 1) * HEAD_DIM)
        qh = _rms(q[:, sl], gq_ref[...]).astype(BF16)
        s = lax.dot_general(qh, kc_ref[0, :, sl], (((1,), (1,)), ((), ())), preferred_element_type=F32)
        p = jnp.exp(s - jnp.max(s, axis=1, keepdims=True))
        l = jnp.sum(p, axis=1, keepdims=True)
        heads.append(jnp.dot(p.astype(BF16), vc_ref[0, :, sl], preferred_element_type=F32) / l)
    o = jnp.concatenate(heads, axis=1).astype(BF16)
    x2 = x1 + jnp.dot(o, wo_ref[...], preferred_element_type=F32)
    x2_ref[...] = x2

    hf = _rms(x2, gf_ref[...])
    hf_hi = hf.astype(BF16)
    hf_hi32 = hf_hi.astype(F32)
    hf_lo = (hf - hf_hi32).astype(BF16)
    packed = _pack_bf16_pairs(hf_hi32)
    tm = packed.shape[0]
    nb = packed.shape[1] // LANE
    for j in range(nb):
        hp_ref[pl.ds(j, tm, stride=nb), :] = packed[:, j * LANE:(j + 1) * LANE]

    n_exp = N_GROUPS * EXPERTS_PER_GROUP
    lg = (jnp.dot(hf_hi, wrh_ref[...], preferred_element_type=F32)
          + jnp.dot(hf_hi, wrl_ref[...], preferred_element_type=F32)
          + jnp.dot(hf_lo, wrh_ref[...], preferred_element_type=F32)) + br_ref[...]
    lane = lax.broadcasted_iota(I32, lg.shape, 1)
    lanef = lane.astype(F32)
    ninf = -jnp.inf
    is_g = (lane >= n_exp) & (lane < n_exp + N_GROUPS)
    gl = jnp.where(is_g, lg, ninf)
    gmax = jnp.max(gl, axis=1, keepdims=True)
    gidx = jnp.min(jnp.where(gl == gmax, lanef, 1e9), axis=1, keepdims=True) - n_exp
    g_gate = 1.0 / jnp.sum(jnp.exp(gl - gmax), axis=1, keepdims=True)
    in_grp = (lane < n_exp) & ((lane // EXPERTS_PER_GROUP).astype(F32) == gidx)
    el = jnp.where(in_grp, lg, ninf)
    m1 = jnp.max(el, axis=1, keepdims=True)
    i1 = jnp.min(jnp.where(el == m1, lanef, 1e9), axis=1, keepdims=True)
    el2 = jnp.where(lanef == i1, ninf, el)
    m2 = jnp.max(el2, axis=1, keepdims=True)
    i2 = jnp.min(jnp.where(el2 == m2, lanef, 1e9), axis=1, keepdims=True)
    p2 = jnp.exp(m2 - m1)
    den = 1.0 + p2
    eid_ref[...] = jnp.where(lane == 0, i1, jnp.where(lane == 1, i2, 0.0)).astype(I32)
    gate_ref[...] = jnp.where(lane == 0, g_gate / den, jnp.where(lane == 1, g_gate * p2 / den, 0.0))


def _cross_router(x1, g_cross, wq, gq, kc, vc, wo, g_ffn, wr_hi, wr_lo, br, seq, tm):
    n, d = x1.shape
    wc = wq.shape[1]
    m = kc.shape[1]
    tiles_per_batch = seq // tm
    const = lambda i: (0, 0)
    return pl.pallas_call(
        _cross_router_kernel,
        grid=(n // tm,),
        in_specs=[
            pl.BlockSpec((tm, d), lambda i: (i, 0)),
            pl.BlockSpec((1, d), const),
            pl.BlockSpec((d, wc), const),
            pl.BlockSpec((1, HEAD_DIM), const),
            pl.BlockSpec((1, m, wc), lambda i: (i // tiles_per_batch, 0, 0)),
            pl.BlockSpec((1, m, wc), lambda i: (i // tiles_per_batch, 0, 0)),
            pl.BlockSpec((wc, d), const),
            pl.BlockSpec((1, d), const),
            pl.BlockSpec((d, LANE), const),
            pl.BlockSpec((d, LANE), const),
            pl.BlockSpec((1, LANE), const),
        ],
        out_specs=[
            pl.BlockSpec((tm, d), lambda i: (i, 0)),
            pl.BlockSpec((tm * (d // 2 // LANE), LANE), lambda i: (i, 0)),
            pl.BlockSpec((tm, LANE), lambda i: (i, 0)),
            pl.BlockSpec((tm, LANE), lambda i: (i, 0)),
        ],
        out_shape=[
            jax.ShapeDtypeStruct((n, d), F32),
            jax.ShapeDtypeStruct((n * (d // 2 // LANE), LANE), U32),
            jax.ShapeDtypeStruct((n, LANE), I32),
            jax.ShapeDtypeStruct((n, LANE), F32),
        ],
        compiler_params=_params(("arbitrary",), 2 * _nbytes((tm, d), F32), _nbytes((tm, d // 2), U32),
                                2 * _nbytes((d, wc), BF16), 2 * _nbytes((d, LANE), BF16)),
        name="cross_router",
    )(x1, g_cross.reshape(1, d), wq, gq.reshape(1, HEAD_DIM), kc, vc, wo, g_ffn.reshape(1, d), wr_hi, wr_lo, br)


def _moe_up_kernel(nused_ref, te_ref, tblk_ref, tok_ref, tok_next_ref, hp_ref, wg_ref, wu_ref, h_ref,
                   x_scr, stage, g_acc, u_acc, sem, *, rt, nb, kd):
    t = pl.program_id(0)
    k = pl.program_id(1)
    n_used = nused_ref[0]
    half = nb * LANE
    himask = jnp.uint32(0xFFFF0000)

    def start_tile(tok):
        def body(r, c):
            pltpu.make_async_copy(hp_ref.at[pl.ds(tok[0, 0, r] * nb, nb)],
                                  stage.at[pl.ds(r * nb, nb)], sem).start(priority=1)
            return c

        lax.fori_loop(0, rt, body, 0)

    def put(col, val):
        x_scr[col // kd, :, col % kd:col % kd + LANE] = val.astype(BF16)

    @pl.when(t < n_used)
    def _():
        @pl.when(k == 0)
        def _():
            @pl.when(t == 0)
            def _():
                start_tile(tok_ref)

            pltpu.make_async_copy(hp_ref.at[pl.ds(0, rt * nb)], stage, sem).wait()
            for j in range(nb):
                w = stage[pl.ds(j, rt, stride=nb), :]
                put(j * LANE, pltpu.bitcast(w << 16, F32))
                put(half + j * LANE, pltpu.bitcast(w & himask, F32))

            @pl.when(t + 1 < n_used)
            def _():
                start_tile(tok_next_ref)

        x = x_scr[k]
        g = jnp.dot(x, wg_ref[0].astype(BF16), preferred_element_type=F32)
        u = jnp.dot(x, wu_ref[0].astype(BF16), preferred_element_type=F32)

        @pl.when(k == 0)
        def _():
            g_acc[...] = g
            u_acc[...] = u

        @pl.when(k > 0)
        def _():
            g_acc[...] += g
            u_acc[...] += u

        @pl.when(k == pl.num_programs(1) - 1)
        def _():
            gg = g_acc[...]
            h_ref[...] = (gg * jax.nn.sigmoid(gg) * u_acc[...]).astype(BF16)


def _moe_down_kernel(nused_ref, te_ref, tblk_ref, h_ref, wd_ref, y_ref):
    @pl.when(pl.program_id(0) < nused_ref[0])
    def _():
        y_ref[...] = jnp.dot(h_ref[...], wd_ref[0].astype(BF16), preferred_element_type=F32)


def _moe_experts(n_used, tile_e, tile_blk, slot_tok, hp, w_gate, w_up, w_down, n, n_tiles, rt, kd, tn):
    n_exp, d, de = w_gate.shape
    nb = hp.shape[0] // n
    assert 2 * nb * LANE == d and kd % LANE == 0 and hp.shape[0] >= rt * nb
    p_rows = n_tiles * rt
    nk = d // kd
    nn = d // tn
    tok3 = slot_tok.reshape(n_tiles, 1, rt)

    def inner(t, k, nused, last):
        return jnp.where(t < nused[0], k, last)

    h_mid = pl.pallas_call(
        functools.partial(_moe_up_kernel, rt=rt, nb=nb, kd=kd),
        grid_spec=pltpu.PrefetchScalarGridSpec(
            num_scalar_prefetch=3,
            grid=(n_tiles, nk),
            in_specs=[
                pl.BlockSpec((1, 1, rt), lambda t, k, nu, te, tb: (t, 0, 0), memory_space=pltpu.SMEM),
                pl.BlockSpec((1, 1, rt), lambda t, k, nu, te, tb: (jnp.minimum(t + 1, n_tiles - 1), 0, 0),
                             memory_space=pltpu.SMEM),
                pl.BlockSpec(memory_space=pl.ANY),
                pl.BlockSpec((1, kd, de), lambda t, k, nu, te, tb: (te[t], inner(t, k, nu, nk - 1), 0)),
                pl.BlockSpec((1, kd, de), lambda t, k, nu, te, tb: (te[t], inner(t, k, nu, nk - 1), 0)),
            ],
            out_specs=pl.BlockSpec((rt, de), lambda t, k, nu, te, tb: (tb[t], 0)),
            scratch_shapes=[pltpu.VMEM((nk, rt, kd), BF16), pltpu.VMEM((rt * nb, LANE), U32),
                            pltpu.VMEM((rt, de), F32), pltpu.VMEM((rt, de), F32), pltpu.SemaphoreType.DMA(())],
        ),
        out_shape=jax.ShapeDtypeStruct((p_rows, de), BF16),
        compiler_params=_params(("arbitrary", "arbitrary"), 2 * _nbytes((kd, de), F32), _nbytes((rt, de), F32),
                                _nbytes((rt * nb, LANE), U32) // 2, _nbytes((rt, d), BF16) // 2),
        name="moe_gate_up",
    )(n_used, tile_e, tile_blk, tok3, tok3, hp, w_gate, w_up)

    return pl.pallas_call(
        _moe_down_kernel,
        grid_spec=pltpu.PrefetchScalarGridSpec(
            num_scalar_prefetch=3,
            grid=(n_tiles, nn),
            in_specs=[
                pl.BlockSpec((rt, de), lambda t, j, nu, te, tb: (tb[t], 0)),
                pl.BlockSpec((1, de, tn), lambda t, j, nu, te, tb: (te[t], 0, inner(t, j, nu, nn - 1))),
            ],
            out_specs=pl.BlockSpec((rt, tn), lambda t, j, nu, te, tb: (tb[t], inner(t, j, nu, nn - 1))),
        ),
        out_shape=jax.ShapeDtypeStruct((p_rows, d), F32),
        compiler_params=_params(("arbitrary", "arbitrary"), _nbytes((rt, de), BF16), _nbytes((de, tn), F32),
                                _nbytes((rt, tn), F32)),
        name="moe_down",
    )(n_used, tile_e, tile_blk, h_mid, w_down)


def _moe_combine_kernel(pos_ref, pos_next_ref, x_ref, g_ref, y_ref, o_ref, buf0, buf1, sem0, sem1, *, tm, top_k):
    i = pl.program_id(0)
    bufs, sems = (buf0, buf1), (sem0, sem1)

    def start_step(pos, slot):
        def body(r, c):
            for k in range(top_k):
                pltpu.make_async_copy(y_ref.at[pl.ds(pos[0, 0, top_k * r + k], 1)],
                                      bufs[slot].at[pl.ds(k * tm + r, 1)], sems[slot]).start(priority=k % 2)
            return c

        lax.fori_loop(0, tm, body, 0)

    for slot in range(2):
        @pl.when(i % 2 == slot)
        def _(slot=slot):
            if slot == 0:
                @pl.when(i == 0)
                def _():
                    start_step(pos_ref, 0)

            @pl.when(i + 1 < pl.num_programs(0))
            def _():
                start_step(pos_next_ref, 1 - slot)

            pltpu.make_async_copy(y_ref.at[pl.ds(0, top_k * tm)], bufs[slot], sems[slot]).wait()
            acc = x_ref[...]
            for k in range(top_k):
                acc = acc + g_ref[:, k:k + 1] * bufs[slot][k * tm:(k + 1) * tm, :]
            o_ref[...] = acc


def _moe_combine(dest, x2, gate, y, tm, top_k):
    n, d = x2.shape
    steps = n // tm
    assert y.shape[0] >= top_k * tm
    pos3 = dest.reshape(steps, 1, tm * top_k)
    stage = pltpu.VMEM((top_k * tm, d), F32)
    return pl.pallas_call(
        functools.partial(_moe_combine_kernel, tm=tm, top_k=top_k),
        grid=(steps,),
        in_specs=[
            pl.BlockSpec((1, 1, tm * top_k), lambda i: (i, 0, 0), memory_space=pltpu.SMEM),
            pl.BlockSpec((1, 1, tm * top_k), lambda i: (jnp.minimum(i + 1, steps - 1), 0, 0),
                         memory_space=pltpu.SMEM),
            pl.BlockSpec((tm, d), lambda i: (i, 0)),
            pl.BlockSpec((tm, LANE), lambda i: (i, 0)),
            pl.BlockSpec(memory_space=pl.ANY),
        ],
        out_specs=pl.BlockSpec((tm, d), lambda i: (i, 0)),
        out_shape=jax.ShapeDtypeStruct((n, d), F32),
        scratch_shapes=[stage, stage, pltpu.SemaphoreType.DMA(()), pltpu.SemaphoreType.DMA(())],
        compiler_params=_params(("arbitrary",), 2 * _nbytes((tm, d), F32), _nbytes((top_k * tm, d), F32)),
        name="moe_combine",
    )(pos3, pos3, x2, gate, y)


def _moe_schedule(eid, n_exp, rt, n_tiles):
    n, top_k = eid.shape
    a = n * top_k
    eflat = eid.reshape(a)
    onehot = (eflat[:, None] == jnp.arange(n_exp, dtype=I32)[None, :]).astype(I32)
    csum = jnp.cumsum(onehot, axis=0)
    rank = jnp.sum(csum * onehot, axis=1) - 1
    counts = csum[-1]
    nt = (counts + rt - 1) // rt
    tcum = jnp.cumsum(nt)
    tstart = tcum - nt
    n_used = tcum[-1]
    dest = tstart[eflat] * rt + rank
    tid = jnp.arange(n_tiles, dtype=I32)
    tile_blk = jnp.minimum(tid, n_used - 1)
    tile_e = jnp.minimum(jnp.searchsorted(tcum, tile_blk, side="right"), n_exp - 1).astype(I32)
    tok = jnp.arange(a, dtype=I32) // top_k
    slot_tok = jnp.zeros((n_tiles * rt,), I32).at[dest].set(tok)
    return n_used.reshape(1).astype(I32), tile_e, tile_blk.astype(I32), slot_tok, dest.astype(I32)


def _pick(n, pref):
    t = min(n, pref)
    assert n % t == 0
    return t


def kernel(x, mem, g_mix, w_in, g_q_dil, g_k_dil, g_out_dil, g_out_sb, w_out, g_cross, g_mem, w_q_cross, w_kv_cross, g_q_cross, g_k_cross, w_o_cross, g_ffn, w_group, b_group, w_router, b_router, w_exp_gate, w_exp_up, w_exp_down):
    batch, seq, d = x.shape
    n = batch * seq
    n_heads_mix = d // HEAD_DIM
    n_dil = n_heads_mix // 2
    n_sb = n_heads_mix - n_dil
    w_dil, w_sb = n_dil * HEAD_DIM, n_sb * HEAD_DIM
    n_exp = w_exp_gate.shape[1]
    top_k = 2
    scale = 1.0 / math.sqrt(HEAD_DIM)
    slopes = jnp.exp2(-ALIBI_MAX_BIAS * jnp.arange(1, n_dil + 1, dtype=F32) / n_dil)

    tm = _pick(n, 512)
    rt = -(-(9 * n * top_k) // (8 * n_exp * MOE_ROW_ALIGN)) * MOE_ROW_ALIGN
    n_tiles = -(-n * top_k // rt) + n_exp
    kd = max(d // 2, LANE)

    xf = x.reshape(n, d)
    for l in range(g_mix.shape[0]):
        h = _rmsnorm_cast(xf, g_mix[l], tm)
        w_in_b = w_in[l].astype(BF16)
        gain_dil = jnp.concatenate([jnp.tile(g_q_dil[l] * scale, n_dil), jnp.tile(g_k_dil[l], n_dil),
                                    jnp.ones((w_dil,), F32)])
        gain_sb = jnp.concatenate([jnp.full((w_sb,), scale * math.log2(math.e), F32), jnp.ones((2 * w_sb,), F32)])
        tn = _pick(w_dil, 1024)
        tmp = _pick(n, 1024)
        qkv_dil = _proj(h, w_in_b, gain_dil, 0, 3 * w_dil, 2 * w_dil, F32, tmp, tn, "in_proj_dil")
        qkv_sb = _proj(h, w_in_b, gain_sb, 3 * w_dil, 3 * w_sb, 0, BF16, tmp, tn, "in_proj_sb")
        o_dil = _dilated(qkv_dil, slopes, batch, seq, n_dil, _pick(seq, 2048))
        o_sb = _stick_breaking(qkv_sb, batch, seq, n_sb, _pick(seq, 1024), _pick(seq, 256))
        x1 = _outproj(o_dil, o_sb, g_out_dil[l], g_out_sb[l], w_out[l].astype(BF16), xf, tm, _pick(d, 1024))

        kc, vc = _mem_kv(mem, g_mem[l], w_kv_cross[l].astype(BF16), g_k_cross[l])
        wr = jnp.concatenate([w_router[l], w_group[l],
                              jnp.zeros((d, LANE - n_exp - N_GROUPS), F32)], axis=1)
        wr_hi = wr.astype(BF16)
        wr_lo = (wr - wr_hi.astype(F32)).astype(BF16)
        br = jnp.concatenate([b_router[l].reshape(-1), b_group[l],
                              jnp.zeros((LANE - n_exp - N_GROUPS,), F32)]).reshape(1, LANE)
        x2, hp, eid, gate = _cross_router(x1, g_cross[l], w_q_cross[l].astype(BF16), g_q_cross[l] * scale, kc, vc,
                                          w_o_cross[l].astype(BF16), g_ffn[l], wr_hi, wr_lo, br, seq,
                                          _pick(seq, 256))

        n_used, tile_e, tile_blk, slot_tok, dest = _moe_schedule(eid[:, :top_k], n_exp, rt, n_tiles)
        y = _moe_experts(n_used, tile_e, tile_blk, slot_tok, hp, w_exp_gate[l], w_exp_up[l], w_exp_down[l],
                         n, n_tiles, rt, kd, _pick(d, 2048))
        xf = _moe_combine(dest, x2, gate, y, _pick(n, 256), top_k)
    return xf.reshape(batch, seq, d)
```

```python
import functools
import math

import jax
import jax.numpy as jnp
from jax import lax
from jax.experimental import pallas as pl
from jax.experimental.pallas import tpu as pltpu

F32 = jnp.float32
BF16 = jnp.bfloat16
I32 = jnp.int32
U32 = jnp.uint32

LANE = 128
HEAD_DIM = 128
EPS = 1e-6
DIL_PAIRS = ((128, 1), (512, 4), (2048, 16))
DIL_STEPS = 128
DIL_UNROLL = 16
ALIBI_MAX_BIAS = 8.0
N_HEADS_CROSS = 4
N_GROUPS = 8
EXPERTS_PER_GROUP = 8
MOE_ROW_ALIGN = 64
NEG_BIG = -1e30
VMEM_LIMIT_CAP = 60000 * 1024
VMEM_INTERNAL = 12 * 1024 * 1024


def _vmem_limit(*block_bytes):
    return int(min(VMEM_LIMIT_CAP, 2 * sum(block_bytes) + VMEM_INTERNAL))


def _nbytes(shape, dtype):
    return math.prod(shape) * jnp.dtype(dtype).itemsize


def _params(sem, *block_bytes):
    return pltpu.CompilerParams(dimension_semantics=sem, vmem_limit_bytes=_vmem_limit(*block_bytes))


def _rms(x, g):
    return x * lax.rsqrt(jnp.mean(x * x, axis=-1, keepdims=True) + EPS) * g


def _rmsnorm_cast_kernel(x_ref, g_ref, o_ref):
    o_ref[...] = _rms(x_ref[...], g_ref[...]).astype(o_ref.dtype)


def _rmsnorm_cast(x, g, tm):
    n, d = x.shape
    return pl.pallas_call(
        _rmsnorm_cast_kernel,
        grid=(n // tm,),
        in_specs=[pl.BlockSpec((tm, d), lambda i: (i, 0)), pl.BlockSpec((1, d), lambda i: (0, 0))],
        out_specs=pl.BlockSpec((tm, d), lambda i: (i, 0)),
        out_shape=jax.ShapeDtypeStruct((n, d), BF16),
        compiler_params=_params(("arbitrary",), _nbytes((tm, d), F32), _nbytes((tm, d), BF16)),
        name="rmsnorm_cast",
    )(x, g.reshape(1, d))


def _proj_kernel(a_ref, w_ref, g_ref, o_ref, *, n_norm_tiles):
    j = pl.program_id(1)
    acc = jnp.dot(a_ref[...], w_ref[...], preferred_element_type=F32)
    tn = acc.shape[1]

    @pl.when(j < n_norm_tiles)
    def _():
        for h in range(tn // HEAD_DIM):
            sl = slice(h * HEAD_DIM, (h + 1) * HEAD_DIM)
            o_ref[:, sl] = _rms(acc[:, sl], g_ref[:, sl]).astype(o_ref.dtype)

    @pl.when(j >= n_norm_tiles)
    def _():
        o_ref[...] = (acc * g_ref[...]).astype(o_ref.dtype)


def _proj(a, w, gain, col0, ncols, norm_cols, out_dtype, tm, tn, name):
    n, k = a.shape
    assert col0 % tn == 0 and ncols % tn == 0 and norm_cols % tn == 0 and n % tm == 0
    jb = col0 // tn
    return pl.pallas_call(
        functools.partial(_proj_kernel, n_norm_tiles=norm_cols // tn),
        grid=(n // tm, ncols // tn),
        in_specs=[
            pl.BlockSpec((tm, k), lambda i, j: (i, 0)),
            pl.BlockSpec((k, tn), lambda i, j: (0, j + jb)),
            pl.BlockSpec((1, tn), lambda i, j: (0, j)),
        ],
        out_specs=pl.BlockSpec((tm, tn), lambda i, j: (i, j)),
        out_shape=jax.ShapeDtypeStruct((n, ncols), out_dtype),
        compiler_params=_params(("arbitrary", "arbitrary"), _nbytes((tm, k), BF16), _nbytes((k, tn), BF16),
                                _nbytes((tm, tn), F32)),
        name=name,
    )(a, w, gain.reshape(1, ncols))


def _dil_kernel(slope_ref, q_ref, k_ref, v_ref, o_ref, ob0, ob1, ob2, ls0, ls1, ls2, *, tq):
    h = pl.program_id(1)
    t0 = pl.program_id(2) * tq
    slope = slope_ref[h]
    J = DIL_STEPS
    qi = lax.broadcasted_iota(I32, (J, 2 * J), 0)
    ki = lax.broadcasted_iota(I32, (J, 2 * J), 1)
    dist = J + qi - ki
    valid = (dist >= 0) & (dist <= J)
    distf = dist.astype(F32)
    in_cur = ki >= J
    ones_v = jnp.ones((2 * J, HEAD_DIM), BF16)

    for (window, d), ob, ls in zip(DIL_PAIRS, (ob0, ob1, ob2), (ls0, ls1, ls2)):
        assert window // d == J and tq % (d * J) == 0
        bias = jnp.where(valid, (-slope * d) * distf, NEG_BIG)
        shift = d.bit_length() - 1

        def ld(ref, start, d=d):
            if d == 1:
                return ref[pl.ds(start, J), :]
            return ref[pl.ds(start, J, stride=d), :]

        def unit(u, carry, d=d, shift=shift, bias=bias, ob=ob, ls=ls, ld=ld):
            off = (u & (d - 1)) + (u >> shift) * (d * J)
            cur0 = t0 + off
            prev0 = cur0 - d * J
            has_prev = prev0 >= 0
            prev0 = jnp.maximum(prev0, 0)
            qs = ld(q_ref, off).astype(BF16)
            kcat = jnp.concatenate([ld(k_ref, prev0), ld(k_ref, cur0)], axis=0).astype(BF16)
            vcat = jnp.concatenate([ld(v_ref, prev0), ld(v_ref, cur0)], axis=0).astype(BF16)
            s = lax.dot_general(qs, kcat, (((1,), (1,)), ((), ())), preferred_element_type=F32) + bias
            s = jnp.where(jnp.logical_or(in_cur, has_prev), s, NEG_BIG)
            m = jnp.max(s, axis=1, keepdims=True)
            p = jnp.exp(s - m)
            ov = jnp.dot(p.astype(BF16), jnp.concatenate([vcat, ones_v], axis=1), preferred_element_type=F32)
            l = ov[:, HEAD_DIM:]
            o = ov[:, :HEAD_DIM] / l
            lse = m + jnp.log(l)
            if d == 1:
                ob[pl.ds(off, J), :] = o
                ls[pl.ds(off, J), :] = lse
            else:
                ob[pl.ds(off, J, stride=d), :] = o
                ls[pl.ds(off, J, stride=d), :] = lse
            return carry

        lax.fori_loop(0, tq // J, unit, 0, unroll=DIL_UNROLL)

    l0, l1, l2 = ls0[...], ls1[...], ls2[...]
    mx = jnp.maximum(jnp.maximum(l0, l1), l2)
    w0, w1, w2 = jnp.exp(l0 - mx), jnp.exp(l1 - mx), jnp.exp(l2 - mx)
    o_ref[...] = (w0 * ob0[...] + w1 * ob1[...] + w2 * ob2[...]) / (w0 + w1 + w2)


def _dilated(qkv, slopes, batch, seq, n_heads, tq):
    assert seq % tq == 0
    n = batch * seq
    nq = seq // tq
    blk = (tq, HEAD_DIM)
    full = (seq, HEAD_DIM)
    scr = [pltpu.VMEM(blk, F32)] * 6
    return pl.pallas_call(
        functools.partial(_dil_kernel, tq=tq),
        grid=(batch, n_heads, nq),
        in_specs=[
            pl.BlockSpec(memory_space=pltpu.SMEM),
            pl.BlockSpec(blk, lambda b, h, c: (b * nq + c, h)),
            pl.BlockSpec(full, lambda b, h, c: (b, n_heads + h)),
            pl.BlockSpec(full, lambda b, h, c: (b, 2 * n_heads + h)),
        ],
        out_specs=pl.BlockSpec(blk, lambda b, h, c: (b * nq + c, h)),
        out_shape=jax.ShapeDtypeStruct((n, n_heads * HEAD_DIM), F32),
        scratch_shapes=scr,
        compiler_params=_params(("arbitrary",) * 3, 2 * _nbytes(full, F32), 5 * _nbytes(blk, F32)),
        name="dilated_attn",
    )(slopes, qkv, qkv, qkv)


def _sb_kernel(q_ref, k_ref, v_ref, o_ref, *, tq, sub):
    n_chain = tq // sub
    assert sub & (sub - 1) == 0
    kb0 = pl.program_id(2) * n_chain
    row = lax.broadcasted_iota(I32, (sub, sub), 0)
    col = lax.broadcasted_iota(I32, (sub, sub), 1)
    after = (row > col).astype(BF16)
    after2 = jnp.concatenate([after, after], axis=0)
    qrow = lax.broadcasted_iota(I32, (tq, sub), 0)
    causal = lax.broadcasted_iota(I32, (tq, sub), 1) < (qrow & (sub - 1))
    chain_of_row = lax.broadcasted_iota(I32, (tq, 1), 0) >> (sub.bit_length() - 1)
    sign = jnp.uint32(0x80000000)

    def sweep(kbs, drop, acc, diag, first_valid):
        starts = [pl.multiple_of(kb * sub, sub) for kb in kbs]
        z2 = jnp.concatenate(
            [lax.dot_general(q_ref[c * sub:(c + 1) * sub, :], k_ref[pl.ds(starts[c], sub), :],
                             (((1,), (1,)), ((), ())), preferred_element_type=F32) for c in range(n_chain)], axis=0)
        neg_abs = pltpu.bitcast(pltpu.bitcast(z2, U32) | sign, F32)
        sp2 = jnp.maximum(z2, 0.0) + jnp.log2(1.0 + jnp.exp2(neg_abs))
        log_beta2 = z2 - sp2
        if diag:
            sp2 = jnp.where(causal, sp2, 0.0)
        hi = sp2.astype(BF16)
        lo = (sp2 - hi.astype(F32)).astype(BF16)
        later = jnp.dot(jnp.concatenate([hi, lo], axis=1), after2, preferred_element_type=F32)
        a = jnp.exp2(log_beta2 - later)
        if diag:
            a = jnp.where(causal, a, 0.0)
        a = a.astype(BF16)
        pv = jnp.concatenate(
            [jnp.dot(a[c * sub:(c + 1) * sub, :], v_ref[pl.ds(starts[c], sub), :], preferred_element_type=F32)
             for c in range(n_chain)], axis=0)
        scale = jnp.exp2(-drop)
        new_drop = drop + later[:, 0:1] + sp2[:, 0:1]
        if first_valid is not None:
            valid = chain_of_row >= first_valid
            scale = jnp.where(valid, scale, 0.0)
            new_drop = jnp.where(valid, new_drop, drop)
        return new_drop, acc + scale * pv

    def alive(drop, first_valid):
        live = jnp.where(chain_of_row >= first_valid, jnp.exp2(-drop), 0.0)
        return (jnp.max(live) > 0.0).astype(I32)

    drop, acc = sweep([kb0 + c for c in range(n_chain)], jnp.zeros((tq, 1), F32),
                      jnp.zeros((tq, HEAD_DIM), F32), True, None)

    def cond(st):
        return jnp.logical_and(st[0] <= kb0 + n_chain - 1, st[1] > 0)

    def body(st):
        g, _, drop, acc = st
        drop, acc = sweep([jnp.maximum(kb0 + c - g, 0) for c in range(n_chain)], drop, acc, False, g - kb0)
        return g + 1, alive(drop, g + 1 - kb0), drop, acc

    _, _, _, acc = lax.while_loop(cond, body, (jnp.int32(1), alive(drop, 1 - kb0), drop, acc))
    o_ref[...] = acc


def _stick_breaking(qkv, batch, seq, n_heads, tq, sub):
    assert seq % tq == 0 and tq % sub == 0
    n = batch * seq
    nq = seq // tq
    blk = (tq, HEAD_DIM)
    full = (seq, HEAD_DIM)
    return pl.pallas_call(
        functools.partial(_sb_kernel, tq=tq, sub=sub),
        grid=(batch, n_heads, nq),
        in_specs=[
            pl.BlockSpec(blk, lambda b, h, i: (b * nq + i, h)),
            pl.BlockSpec(full, lambda b, h, i: (b, n_heads + h)),
            pl.BlockSpec(full, lambda b, h, i: (b, 2 * n_heads + h)),
        ],
        out_specs=pl.BlockSpec(blk, lambda b, h, i: (b * nq + i, h)),
        out_shape=jax.ShapeDtypeStruct((n, n_heads * HEAD_DIM), F32),
        compiler_params=_params(("arbitrary",) * 3, 2 * _nbytes(full, BF16), 2 * _nbytes(blk, F32)),
        name="stick_breaking",
    )(qkv, qkv, qkv)


def _outproj_kernel(od_ref, os_ref, gd_ref, gs_ref, w_ref, x_ref, o_ref, m_scr):
    wd = od_ref.shape[1]

    @pl.when(pl.program_id(1) == 0)
    def _():
        m_scr[:, :wd] = _rms(od_ref[...], gd_ref[...]).astype(BF16)
        m_scr[:, wd:] = _rms(os_ref[...], gs_ref[...]).astype(BF16)

    o_ref[...] = x_ref[...] + jnp.dot(m_scr[...], w_ref[...], preferred_element_type=F32)


def _outproj(o_dil, o_sb, g_dil, g_sb, w, x, tm, tn):
    n, wd = o_dil.shape
    ws = o_sb.shape[1]
    d = x.shape[1]
    return pl.pallas_call(
        _outproj_kernel,
        grid=(n // tm, d // tn),
        in_specs=[
            pl.BlockSpec((tm, wd), lambda i, j: (i, 0)),
            pl.BlockSpec((tm, ws), lambda i, j: (i, 0)),
            pl.BlockSpec((1, wd), lambda i, j: (0, 0)),
            pl.BlockSpec((1, ws), lambda i, j: (0, 0)),
            pl.BlockSpec((wd + ws, tn), lambda i, j: (0, j)),
            pl.BlockSpec((tm, tn), lambda i, j: (i, j)),
        ],
        out_specs=pl.BlockSpec((tm, tn), lambda i, j: (i, j)),
        out_shape=jax.ShapeDtypeStruct((n, d), F32),
        scratch_shapes=[pltpu.VMEM((tm, wd + ws), BF16)],
        compiler_params=_params(("arbitrary", "arbitrary"), _nbytes((tm, wd + ws), F32),
                                _nbytes((wd + ws, tn), BF16), 2 * _nbytes((tm, tn), F32),
                                _nbytes((tm, wd + ws), BF16) // 2),
        name="out_proj",
    )(o_dil, o_sb, g_dil.reshape(1, wd), g_sb.reshape(1, ws), w, x)


def _mem_kv_kernel(m_ref, gm_ref, w_ref, gk_ref, k_ref, v_ref):
    hm = _rms(m_ref[0], gm_ref[...]).astype(BF16)
    kv = jnp.dot(hm, w_ref[...], preferred_element_type=F32)
    wc = k_ref.shape[2]
    for h in range(wc // HEAD_DIM):
        sl = slice(h * HEAD_DIM, (h + 1) * HEAD_DIM)
        k_ref[0, :, sl] = _rms(kv[:, sl], gk_ref[...]).astype(BF16)
    v_ref[0] = kv[:, wc:].astype(BF16)


def _mem_kv(mem, g_mem, w_kv, g_k):
    b, m, d = mem.shape
    wc = w_kv.shape[1] // 2
    out = jax.ShapeDtypeStruct((b, m, wc), BF16)
    return pl.pallas_call(
        _mem_kv_kernel,
        grid=(b,),
        in_specs=[
            pl.BlockSpec((1, m, d), lambda i: (i, 0, 0)),
            pl.BlockSpec((1, d), lambda i: (0, 0)),
            pl.BlockSpec((d, 2 * wc), lambda i: (0, 0)),
            pl.BlockSpec((1, HEAD_DIM), lambda i: (0, 0)),
        ],
        out_specs=[pl.BlockSpec((1, m, wc), lambda i: (i, 0, 0))] * 2,
        out_shape=[out, out],
        compiler_params=_params(("arbitrary",), _nbytes((m, d), F32), _nbytes((d, 2 * wc), BF16)),
        name="mem_kv",
    )(mem, g_mem.reshape(1, d), w_kv, g_k.reshape(1, HEAD_DIM))


def _pack_bf16_pairs(xr):
    half = xr.shape[1] // 2
    lo = pltpu.bitcast(xr[:, :half], U32)
    hi = pltpu.bitcast(xr[:, half:], U32)
    return hi | (lo >> 16)


def _cross_router_kernel(x1_ref, gc_ref, wq_ref, gq_ref, kc_ref, vc_ref, wo_ref, gf_ref, wrh_ref, wrl_ref, br_ref,
                         x2_ref, hp_ref, eid_ref, gate_ref):
    x1 = x1_ref[...]
    h = _rms(x1, gc_ref[...]).astype(BF16)
    q = jnp.dot(h, wq_ref[...], preferred_element_type=F32)
    heads = []
    for hd in range(q.shape[1] // HEAD_DIM):
        sl = slice(hd * HEAD_DIM, (hd + 1) * HEAD_DIM)
        qh = _rms(q[:, sl], gq_ref[...]).astype(BF16)
        s = lax.dot_general(qh, kc_ref[0, :, sl], (((1,), (1,)), ((), ())), preferred_element_type=F32)
        p = jnp.exp(s - jnp.max(s, axis=1, keepdims=True))
        l = jnp.sum(p, axis=1, keepdims=True)
        heads.append(jnp.dot(p.astype(BF16), vc_ref[0, :, sl], preferred_element_type=F32) / l)
    o = jnp.concatenate(heads, axis=1).astype(BF16)
    x2 = x1 + jnp.dot(o, wo_ref[...], preferred_element_type=F32)
    x2_ref[...] = x2

    hf = _rms(x2, gf_ref[...])
    hf_hi = hf.astype(BF16)
    hf_hi32 = hf_hi.astype(F32)
    hf_lo = (hf - hf_hi32).astype(BF16)
    packed = _pack_bf16_pairs(hf_hi32)
    tm = packed.shape[0]
    nb = packed.shape[1] // LANE
    for j in range(nb):
        hp_ref[pl.ds(j, tm, stride=nb), :] = packed[:, j * LANE:(j + 1) * LANE]

    n_exp = N_GROUPS * EXPERTS_PER_GROUP
    lg = (jnp.dot(hf_hi, wrh_ref[...], preferred_element_type=F32)
          + jnp.dot(hf_hi, wrl_ref[...], preferred_element_type=F32)
          + jnp.dot(hf_lo, wrh_ref[...], preferred_element_type=F32)) + br_ref[...]
    lane = lax.broadcasted_iota(I32, lg.shape, 1)
    lanef = lane.astype(F32)
    ninf = -jnp.inf
    is_g = (lane >= n_exp) & (lane < n_exp + N_GROUPS)
    gl = jnp.where(is_g, lg, ninf)
    gmax = jnp.max(gl, axis=1, keepdims=True)
    gidx = jnp.min(jnp.where(gl == gmax, lanef, 1e9), axis=1, keepdims=True) - n_exp
    g_gate = 1.0 / jnp.sum(jnp.exp(gl - gmax), axis=1, keepdims=True)
    in_grp = (lane < n_exp) & ((lane // EXPERTS_PER_GROUP).astype(F32) == gidx)
    el = jnp.where(in_grp, lg, ninf)
    m1 = jnp.max(el, axis=1, keepdims=True)
    i1 = jnp.min(jnp.where(el == m1, lanef, 1e9), axis=1, keepdims=True)
    el2 = jnp.where(lanef == i1, ninf, el)
    m2 = jnp.max(el2, axis=1, keepdims=True)
    i2 = jnp.min(jnp.where(el2 == m2, lanef, 1e9), axis=1, keepdims=True)
    p2 = jnp.exp(m2 - m1)
    den = 1.0 + p2
    eid_ref[...] = jnp.where(lane == 0, i1, jnp.where(lane == 1, i2, 0.0)).astype(I32)
    gate_ref[...] = jnp.where(lane == 0, g_gate / den, jnp.where(lane == 1, g_gate * p2 / den, 0.0))


def _cross_router(x1, g_cross, wq, gq, kc, vc, wo, g_ffn, wr_hi, wr_lo, br, seq, tm):
    n, d = x1.shape
    wc = wq.shape[1]
    m = kc.shape[1]
    tiles_per_batch = seq // tm
    const = lambda i: (0, 0)
    return pl.pallas_call(
        _cross_router_kernel,
        grid=(n // tm,),
        in_specs=[
            pl.BlockSpec((tm, d), lambda i: (i, 0)),
            pl.BlockSpec((1, d), const),
            pl.BlockSpec((d, wc), const),
            pl.BlockSpec((1, HEAD_DIM), const),
            pl.BlockSpec((1, m, wc), lambda i: (i // tiles_per_batch, 0, 0)),
            pl.BlockSpec((1, m, wc), lambda i: (i // tiles_per_batch, 0, 0)),
            pl.BlockSpec((wc, d), const),
            pl.BlockSpec((1, d), const),
            pl.BlockSpec((d, LANE), const),
            pl.BlockSpec((d, LANE), const),
            pl.BlockSpec((1, LANE), const),
        ],
        out_specs=[
            pl.BlockSpec((tm, d), lambda i: (i, 0)),
            pl.BlockSpec((tm * (d // 2 // LANE), LANE), lambda i: (i, 0)),
            pl.BlockSpec((tm, LANE), lambda i: (i, 0)),
            pl.BlockSpec((tm, LANE), lambda i: (i, 0)),
        ],
        out_shape=[
            jax.ShapeDtypeStruct((n, d), F32),
            jax.ShapeDtypeStruct((n * (d // 2 // LANE), LANE), U32),
            jax.ShapeDtypeStruct((n, LANE), I32),
            jax.ShapeDtypeStruct((n, LANE), F32),
        ],
        compiler_params=_params(("arbitrary",), 2 * _nbytes((tm, d), F32), _nbytes((tm, d // 2), U32),
                                2 * _nbytes((d, wc), BF16), 2 * _nbytes((d, LANE), BF16)),
        name="cross_router",
    )(x1, g_cross.reshape(1, d), wq, gq.reshape(1, HEAD_DIM), kc, vc, wo, g_ffn.reshape(1, d), wr_hi, wr_lo, br)


def _moe_up_kernel(nused_ref, te_ref, tblk_ref, tok_ref, tok_next_ref, hp_ref, wg_ref, wu_ref, h_ref,
                   x_scr, stage, g_acc, u_acc, sem, *, rt, nb, kd):
    t = pl.program_id(0)
    k = pl.program_id(1)
    n_used = nused_ref[0]
    half = nb * LANE
    himask = jnp.uint32(0xFFFF0000)

    def start_tile(tok):
        def body(r, c):
            pltpu.make_async_copy(hp_ref.at[pl.ds(tok[0, 0, r] * nb, nb)],
                                  stage.at[pl.ds(r * nb, nb)], sem).start(priority=1)
            return c

        lax.fori_loop(0, rt, body, 0)

    def put(col, val):
        x_scr[col // kd, :, col % kd:col % kd + LANE] = val.astype(BF16)

    @pl.when(t < n_used)
    def _():
        @pl.when(k == 0)
        def _():
            @pl.when(t == 0)
            def _():
                start_tile(tok_ref)

            pltpu.make_async_copy(hp_ref.at[pl.ds(0, rt * nb)], stage, sem).wait()
            for j in range(nb):
                w = stage[pl.ds(j, rt, stride=nb), :]
                put(j * LANE, pltpu.bitcast(w << 16, F32))
                put(half + j * LANE, pltpu.bitcast(w & himask, F32))

            @pl.when(t + 1 < n_used)
            def _():
                start_tile(tok_next_ref)

        x = x_scr[k]
        g = jnp.dot(x, wg_ref[0].astype(BF16), preferred_element_type=F32)
        u = jnp.dot(x, wu_ref[0].astype(BF16), preferred_element_type=F32)

        @pl.when(k == 0)
        def _():
            g_acc[...] = g
            u_acc[...] = u

        @pl.when(k > 0)
        def _():
            g_acc[...] += g
            u_acc[...] += u

        @pl.when(k == pl.num_programs(1) - 1)
        def _():
            gg = g_acc[...]
            h_ref[...] = (gg * jax.nn.sigmoid(gg) * u_acc[...]).astype(BF16)


def _moe_down_kernel(nused_ref, te_ref, tblk_ref, h_ref, wd_ref, y_ref):
    @pl.when(pl.program_id(0) < nused_ref[0])
    def _():
        y = jnp.dot(h_ref[...], wd_ref[0].astype(BF16), preferred_element_type=F32)
        y_ref[...] = _pack_bf16_pairs(y.astype(BF16).astype(F32))


def _moe_experts(n_used, tile_e, tile_blk, slot_tok, hp, w_gate, w_up, w_down, n, n_tiles, rt, kd):
    n_exp, d, de = w_gate.shape
    nb = hp.shape[0] // n
    assert 2 * nb * LANE == d and kd % LANE == 0 and hp.shape[0] >= rt * nb
    p_rows = n_tiles * rt
    nk = d // kd
    tok3 = slot_tok.reshape(n_tiles, 1, rt)

    def inner(t, k, nused, last):
        return jnp.where(t < nused[0], k, last)

    h_mid = pl.pallas_call(
        functools.partial(_moe_up_kernel, rt=rt, nb=nb, kd=kd),
        grid_spec=pltpu.PrefetchScalarGridSpec(
            num_scalar_prefetch=3,
            grid=(n_tiles, nk),
            in_specs=[
                pl.BlockSpec((1, 1, rt), lambda t, k, nu, te, tb: (t, 0, 0), memory_space=pltpu.SMEM),
                pl.BlockSpec((1, 1, rt), lambda t, k, nu, te, tb: (jnp.minimum(t + 1, n_tiles - 1), 0, 0),
                             memory_space=pltpu.SMEM),
                pl.BlockSpec(memory_space=pl.ANY),
                pl.BlockSpec((1, kd, de), lambda t, k, nu, te, tb: (te[t], inner(t, k, nu, nk - 1), 0)),
                pl.BlockSpec((1, kd, de), lambda t, k, nu, te, tb: (te[t], inner(t, k, nu, nk - 1), 0)),
            ],
            out_specs=pl.BlockSpec((rt, de), lambda t, k, nu, te, tb: (tb[t], 0)),
            scratch_shapes=[pltpu.VMEM((nk, rt, kd), BF16), pltpu.VMEM((rt * nb, LANE), U32),
                            pltpu.VMEM((rt, de), F32), pltpu.VMEM((rt, de), F32), pltpu.SemaphoreType.DMA(())],
        ),
        out_shape=jax.ShapeDtypeStruct((p_rows, de), BF16),
        compiler_params=_params(("arbitrary", "arbitrary"), 2 * _nbytes((kd, de), F32), _nbytes((rt, de), F32),
                                _nbytes((rt * nb, LANE), U32) // 2, _nbytes((rt, d), BF16) // 2),
        name="moe_gate_up",
    )(n_used, tile_e, tile_blk, tok3, tok3, hp, w_gate, w_up)

    return pl.pallas_call(
        _moe_down_kernel,
        grid_spec=pltpu.PrefetchScalarGridSpec(
            num_scalar_prefetch=3,
            grid=(n_tiles,),
            in_specs=[
                pl.BlockSpec((rt, de), lambda t, nu, te, tb: (tb[t], 0)),
                pl.BlockSpec((1, de, d), lambda t, nu, te, tb: (te[t], 0, 0)),
            ],
            out_specs=pl.BlockSpec((rt, d // 2), lambda t, nu, te, tb: (tb[t], 0)),
        ),
        out_shape=jax.ShapeDtypeStruct((p_rows, d // 2), U32),
        compiler_params=_params(("arbitrary",), _nbytes((rt, de), BF16), _nbytes((de, d), F32),
                                _nbytes((rt, d // 2), U32), _nbytes((rt, d), F32) // 2),
        name="moe_down",
    )(n_used, tile_e, tile_blk, h_mid, w_down)


def _moe_combine_kernel(pos_ref, pos_next_ref, x_ref, g_ref, y_ref, o_ref, buf0, buf1, sem0, sem1, *, tm, top_k):
    i = pl.program_id(0)
    bufs, sems = (buf0, buf1), (sem0, sem1)

    def start_step(pos, slot):
        def body(r, c):
            for k in range(top_k):
                pltpu.make_async_copy(y_ref.at[pl.ds(pos[0, 0, top_k * r + k], 1)],
                                      bufs[slot].at[pl.ds(k * tm + r, 1)], sems[slot]).start(priority=k % 2)
            return c

        lax.fori_loop(0, tm, body, 0)

    for slot in range(2):
        @pl.when(i % 2 == slot)
        def _(slot=slot):
            if slot == 0:
                @pl.when(i == 0)
                def _():
                    start_step(pos_ref, 0)

            @pl.when(i + 1 < pl.num_programs(0))
            def _():
                start_step(pos_next_ref, 1 - slot)

            pltpu.make_async_copy(y_ref.at[pl.ds(0, top_k * tm)], bufs[slot], sems[slot]).wait()
            half = x_ref.shape[1] // 2
            lo = x_ref[:, :half]
            hi = x_ref[:, half:]
            for k in range(top_k):
                w = bufs[slot][k * tm:(k + 1) * tm, :]
                g = g_ref[:, k:k + 1]
                lo = lo + g * pltpu.bitcast(w << 16, F32)
                hi = hi + g * pltpu.bitcast(w & jnp.uint32(0xFFFF0000), F32)
            o_ref[:, :half] = lo
            o_ref[:, half:] = hi


def _moe_combine(dest, x2, gate, y, tm, top_k):
    n, d = x2.shape
    steps = n // tm
    assert y.shape[0] >= top_k * tm
    pos3 = dest.reshape(steps, 1, tm * top_k)
    stage = pltpu.VMEM((top_k * tm, d // 2), U32)
    return pl.pallas_call(
        functools.partial(_moe_combine_kernel, tm=tm, top_k=top_k),
        grid=(steps,),
        in_specs=[
            pl.BlockSpec((1, 1, tm * top_k), lambda i: (i, 0, 0), memory_space=pltpu.SMEM),
            pl.BlockSpec((1, 1, tm * top_k), lambda i: (jnp.minimum(i + 1, steps - 1), 0, 0),
                         memory_space=pltpu.SMEM),
            pl.BlockSpec((tm, d), lambda i: (i, 0)),
            pl.BlockSpec((tm, LANE), lambda i: (i, 0)),
            pl.BlockSpec(memory_space=pl.ANY),
        ],
        out_specs=pl.BlockSpec((tm, d), lambda i: (i, 0)),
        out_shape=jax.ShapeDtypeStruct((n, d), F32),
        scratch_shapes=[stage, stage, pltpu.SemaphoreType.DMA(()), pltpu.SemaphoreType.DMA(())],
        compiler_params=_params(("arbitrary",), 2 * _nbytes((tm, d), F32), _nbytes((top_k * tm, d // 2), U32)),
        name="moe_combine",
    )(pos3, pos3, x2, gate, y)


def _moe_schedule(eid, n_exp, rt, n_tiles):
    n, top_k = eid.shape
    a = n * top_k
    eflat = eid.reshape(a)
    onehot = (eflat[:, None] == jnp.arange(n_exp, dtype=I32)[None, :]).astype(I32)
    csum = jnp.cumsum(onehot, axis=0)
    rank = jnp.sum(csum * onehot, axis=1) - 1
    counts = csum[-1]
    nt = (counts + rt - 1) // rt
    tcum = jnp.cumsum(nt)
    tstart = tcum - nt
    n_used = tcum[-1]
    dest = tstart[eflat] * rt + rank
    tid = jnp.arange(n_tiles, dtype=I32)
    tile_blk = jnp.minimum(tid, n_used - 1)
    tile_e = jnp.minimum(jnp.searchsorted(tcum, tile_blk, side="right"), n_exp - 1).astype(I32)
    tok = jnp.arange(a, dtype=I32) // top_k
    slot_tok = jnp.zeros((n_tiles * rt,), I32).at[dest].set(tok)
    return n_used.reshape(1).astype(I32), tile_e, tile_blk.astype(I32), slot_tok, dest.astype(I32)


def _pick(n, pref):
    t = min(n, pref)
    assert n % t == 0
    return t


def kernel(x, mem, g_mix, w_in, g_q_dil, g_k_dil, g_out_dil, g_out_sb, w_out, g_cross, g_mem, w_q_cross, w_kv_cross, g_q_cross, g_k_cross, w_o_cross, g_ffn, w_group, b_group, w_router, b_router, w_exp_gate, w_exp_up, w_exp_down):
    batch, seq, d = x.shape
    n = batch * seq
    n_heads_mix = d // HEAD_DIM
    n_dil = n_heads_mix // 2
    n_sb = n_heads_mix - n_dil
    w_dil, w_sb = n_dil * HEAD_DIM, n_sb * HEAD_DIM
    n_exp = w_exp_gate.shape[1]
    top_k = 2
    scale = 1.0 / math.sqrt(HEAD_DIM)
    slopes = jnp.exp2(-ALIBI_MAX_BIAS * jnp.arange(1, n_dil + 1, dtype=F32) / n_dil)

    tm = _pick(n, 512)
    rt = -(-(9 * n * top_k) // (8 * n_exp * MOE_ROW_ALIGN)) * MOE_ROW_ALIGN
    n_tiles = -(-n * top_k // rt) + n_exp
    kd = max(d // 2, LANE)

    xf = x.reshape(n, d)
    for l in range(g_mix.shape[0]):
        h = _rmsnorm_cast(xf, g_mix[l], tm)
        w_in_b = w_in[l].astype(BF16)
        gain_dil = jnp.concatenate([jnp.tile(g_q_dil[l] * scale, n_dil), jnp.tile(g_k_dil[l], n_dil),
                                    jnp.ones((w_dil,), F32)])
        gain_sb = jnp.concatenate([jnp.full((w_sb,), scale * math.log2(math.e), F32), jnp.ones((2 * w_sb,), F32)])
        tn = _pick(w_dil, 1024)
        tmp = _pick(n, 1024)
        qkv_dil = _proj(h, w_in_b, gain_dil, 0, 3 * w_dil, 2 * w_dil, F32, tmp, tn, "in_proj_dil")
        qkv_sb = _proj(h, w_in_b, gain_sb, 3 * w_dil, 3 * w_sb, 0, BF16, tmp, tn, "in_proj_sb")
        o_dil = _dilated(qkv_dil, slopes, batch, seq, n_dil, _pick(seq, 2048))
        o_sb = _stick_breaking(qkv_sb, batch, seq, n_sb, _pick(seq, 1024), _pick(seq, 256))
        x1 = _outproj(o_dil, o_sb, g_out_dil[l], g_out_sb[l], w_out[l].astype(BF16), xf, tm, _pick(d, 1024))

        kc, vc = _mem_kv(mem, g_mem[l], w_kv_cross[l].astype(BF16), g_k_cross[l])
        wr = jnp.concatenate([w_router[l], w_group[l],
                              jnp.zeros((d, LANE - n_exp - N_GROUPS), F32)], axis=1)
        wr_hi = wr.astype(BF16)
        wr_lo = (wr - wr_hi.astype(F32)).astype(BF16)
        br = jnp.concatenate([b_router[l].reshape(-1), b_group[l],
                              jnp.zeros((LANE - n_exp - N_GROUPS,), F32)]).reshape(1, LANE)
        x2, hp, eid, gate = _cross_router(x1, g_cross[l], w_q_cross[l].astype(BF16), g_q_cross[l] * scale, kc, vc,
                                          w_o_cross[l].astype(BF16), g_ffn[l], wr_hi, wr_lo, br, seq,
                                          _pick(seq, 256))

        n_used, tile_e, tile_blk, slot_tok, dest = _moe_schedule(eid[:, :top_k], n_exp, rt, n_tiles)
        y = _moe_experts(n_used, tile_e, tile_blk, slot_tok, hp, w_exp_gate[l], w_exp_up[l], w_exp_down[l],
                         n, n_tiles, rt, kd)
        xf = _moe_combine(dest, x2, gate, y, _pick(n, 256), top_k)
    return xf.reshape(batch, seq, d)
```

```python
import functools
import math

import jax
import jax.numpy as jnp
from jax import lax
from jax.experimental import pallas as pl
from jax.experimental.pallas import tpu as pltpu

F32 = jnp.float32
BF16 = jnp.bfloat16
I32 = jnp.int32
U32 = jnp.uint32

LANE = 128
HEAD_DIM = 128
EPS = 1e-6
DIL_PAIRS = ((128, 1), (512, 4), (2048, 16))
DIL_STEPS = 128
DIL_UNROLL = 16
ALIBI_MAX_BIAS = 8.0
N_HEADS_CROSS = 4
N_GROUPS = 8
EXPERTS_PER_GROUP = 8
MOE_ROW_ALIGN = 64
NEG_BIG = -1e30
VMEM_LIMIT_CAP = 60000 * 1024
VMEM_INTERNAL = 12 * 1024 * 1024


def _vmem_limit(*block_bytes):
    return int(min(VMEM_LIMIT_CAP, 2 * sum(block_bytes) + VMEM_INTERNAL))


def _nbytes(shape, dtype):
    return math.prod(shape) * jnp.dtype(dtype).itemsize


def _params(sem, *block_bytes):
    return pltpu.CompilerParams(dimension_semantics=sem, vmem_limit_bytes=_vmem_limit(*block_bytes))


def _rms(x, g):
    return x * lax.rsqrt(jnp.mean(x * x, axis=-1, keepdims=True) + EPS) * g


def _rmsnorm_cast_kernel(x_ref, g_ref, o_ref):
    o_ref[...] = _rms(x_ref[...], g_ref[...]).astype(o_ref.dtype)


def _rmsnorm_cast(x, g, tm):
    n, d = x.shape
    return pl.pallas_call(
        _rmsnorm_cast_kernel,
        grid=(n // tm,),
        in_specs=[pl.BlockSpec((tm, d), lambda i: (i, 0)), pl.BlockSpec((1, d), lambda i: (0, 0))],
        out_specs=pl.BlockSpec((tm, d), lambda i: (i, 0)),
        out_shape=jax.ShapeDtypeStruct((n, d), BF16),
        compiler_params=_params(("arbitrary",), _nbytes((tm, d), F32), _nbytes((tm, d), BF16)),
        name="rmsnorm_cast",
    )(x, g.reshape(1, d))


def _proj_kernel(a_ref, w_ref, g_ref, o_ref, *, n_norm_tiles):
    j = pl.program_id(1)
    acc = jnp.dot(a_ref[...], w_ref[...], preferred_element_type=F32)
    tn = acc.shape[1]

    @pl.when(j < n_norm_tiles)
    def _():
        for h in range(tn // HEAD_DIM):
            sl = slice(h * HEAD_DIM, (h + 1) * HEAD_DIM)
            o_ref[:, sl] = _rms(acc[:, sl], g_ref[:, sl]).astype(o_ref.dtype)

    @pl.when(j >= n_norm_tiles)
    def _():
        o_ref[...] = (acc * g_ref[...]).astype(o_ref.dtype)


def _proj(a, w, gain, col0, ncols, norm_cols, out_dtype, tm, tn, name):
    n, k = a.shape
    assert col0 % tn == 0 and ncols % tn == 0 and norm_cols % tn == 0 and n % tm == 0
    jb = col0 // tn
    return pl.pallas_call(
        functools.partial(_proj_kernel, n_norm_tiles=norm_cols // tn),
        grid=(n // tm, ncols // tn),
        in_specs=[
            pl.BlockSpec((tm, k), lambda i, j: (i, 0)),
            pl.BlockSpec((k, tn), lambda i, j: (0, j + jb)),
            pl.BlockSpec((1, tn), lambda i, j: (0, j)),
        ],
        out_specs=pl.BlockSpec((tm, tn), lambda i, j: (i, j)),
        out_shape=jax.ShapeDtypeStruct((n, ncols), out_dtype),
        compiler_params=_params(("arbitrary", "arbitrary"), _nbytes((tm, k), BF16), _nbytes((k, tn), BF16),
                                _nbytes((tm, tn), F32)),
        name=name,
    )(a, w, gain.reshape(1, ncols))


def _dil_kernel(slope_ref, q_ref, k_ref, v_ref, o_ref, ob0, ob1, ob2, ls0, ls1, ls2, *, tq):
    h = pl.program_id(1)
    t0 = pl.program_id(2) * tq
    slope = slope_ref[h]
    J = DIL_STEPS
    qi = lax.broadcasted_iota(I32, (J, 2 * J), 0)
    ki = lax.broadcasted_iota(I32, (J, 2 * J), 1)
    dist = J + qi - ki
    valid = (dist >= 0) & (dist <= J)
    distf = dist.astype(F32)
    in_cur = ki >= J
    ones_v = jnp.ones((2 * J, HEAD_DIM), BF16)

    for (window, d), ob, ls in zip(DIL_PAIRS, (ob0, ob1, ob2), (ls0, ls1, ls2)):
        assert window // d == J and tq % (d * J) == 0
        bias = jnp.where(valid, (-slope * d) * distf, NEG_BIG)
        shift = d.bit_length() - 1

        def ld(ref, start, d=d):
            if d == 1:
                return ref[pl.ds(start, J), :]
            return ref[pl.ds(start, J, stride=d), :]

        def unit(u, carry, d=d, shift=shift, bias=bias, ob=ob, ls=ls, ld=ld):
            off = (u & (d - 1)) + (u >> shift) * (d * J)
            cur0 = t0 + off
            prev0 = cur0 - d * J
            has_prev = prev0 >= 0
            prev0 = jnp.maximum(prev0, 0)
            qs = ld(q_ref, off).astype(BF16)
            kcat = jnp.concatenate([ld(k_ref, prev0), ld(k_ref, cur0)], axis=0).astype(BF16)
            vcat = jnp.concatenate([ld(v_ref, prev0), ld(v_ref, cur0)], axis=0).astype(BF16)
            s = lax.dot_general(qs, kcat, (((1,), (1,)), ((), ())), preferred_element_type=F32) + bias
            s = jnp.where(jnp.logical_or(in_cur, has_prev), s, NEG_BIG)
            m = jnp.max(s, axis=1, keepdims=True)
            p = jnp.exp(s - m)
            ov = jnp.dot(p.astype(BF16), jnp.concatenate([vcat, ones_v], axis=1), preferred_element_type=F32)
            l = ov[:, HEAD_DIM:]
            o = ov[:, :HEAD_DIM] / l
            lse = m + jnp.log(l)
            if d == 1:
                ob[pl.ds(off, J), :] = o
                ls[pl.ds(off, J), :] = lse
            else:
                ob[pl.ds(off, J, stride=d), :] = o
                ls[pl.ds(off, J, stride=d), :] = lse
            return carry

        lax.fori_loop(0, tq // J, unit, 0, unroll=DIL_UNROLL)

    l0, l1, l2 = ls0[...], ls1[...], ls2[...]
    mx = jnp.maximum(jnp.maximum(l0, l1), l2)
    w0, w1, w2 = jnp.exp(l0 - mx), jnp.exp(l1 - mx), jnp.exp(l2 - mx)
    o_ref[...] = ((w0 * ob0[...] + w1 * ob1[...] + w2 * ob2[...]) / (w0 + w1 + w2)).astype(o_ref.dtype)


def _dilated(qkv, slopes, batch, seq, n_heads, tq):
    assert seq % tq == 0
    n = batch * seq
    nq = seq // tq
    blk = (tq, HEAD_DIM)
    full = (seq, HEAD_DIM)
    scr = [pltpu.VMEM(blk, F32)] * 6
    return pl.pallas_call(
        functools.partial(_dil_kernel, tq=tq),
        grid=(batch, n_heads, nq),
        in_specs=[
            pl.BlockSpec(memory_space=pltpu.SMEM),
            pl.BlockSpec(blk, lambda b, h, c: (b * nq + c, h)),
            pl.BlockSpec(full, lambda b, h, c: (b, n_heads + h)),
            pl.BlockSpec(full, lambda b, h, c: (b, 2 * n_heads + h)),
        ],
        out_specs=pl.BlockSpec(blk, lambda b, h, c: (b * nq + c, h)),
        out_shape=jax.ShapeDtypeStruct((n, n_heads * HEAD_DIM), BF16),
        scratch_shapes=scr,
        compiler_params=_params(("arbitrary",) * 3, 2 * _nbytes(full, F32), 5 * _nbytes(blk, F32)),
        name="dilated_attn",
    )(slopes, qkv, qkv, qkv)


def _sb_kernel(q_ref, k_ref, v_ref, o_ref, *, tq, sub):
    n_chain = tq // sub
    assert sub & (sub - 1) == 0
    kb0 = pl.program_id(2) * n_chain
    row = lax.broadcasted_iota(I32, (sub, sub), 0)
    col = lax.broadcasted_iota(I32, (sub, sub), 1)
    after = (row > col).astype(BF16)
    after2 = jnp.concatenate([after, after], axis=0)
    qrow = lax.broadcasted_iota(I32, (tq, sub), 0)
    causal = lax.broadcasted_iota(I32, (tq, sub), 1) < (qrow & (sub - 1))
    chain_of_row = lax.broadcasted_iota(I32, (tq, 1), 0) >> (sub.bit_length() - 1)
    sign = jnp.uint32(0x80000000)

    def sweep(kbs, drop, acc, diag, first_valid):
        starts = [pl.multiple_of(kb * sub, sub) for kb in kbs]
        z2 = jnp.concatenate(
            [lax.dot_general(q_ref[c * sub:(c + 1) * sub, :], k_ref[pl.ds(starts[c], sub), :],
                             (((1,), (1,)), ((), ())), preferred_element_type=F32) for c in range(n_chain)], axis=0)
        neg_abs = pltpu.bitcast(pltpu.bitcast(z2, U32) | sign, F32)
        sp2 = jnp.maximum(z2, 0.0) + jnp.log2(1.0 + jnp.exp2(neg_abs))
        log_beta2 = z2 - sp2
        if diag:
            sp2 = jnp.where(causal, sp2, 0.0)
        hi = sp2.astype(BF16)
        lo = (sp2 - hi.astype(F32)).astype(BF16)
        later = jnp.dot(jnp.concatenate([hi, lo], axis=1), after2, preferred_element_type=F32)
        a = jnp.exp2(log_beta2 - later)
        if diag:
            a = jnp.where(causal, a, 0.0)
        a = a.astype(BF16)
        pv = jnp.concatenate(
            [jnp.dot(a[c * sub:(c + 1) * sub, :], v_ref[pl.ds(starts[c], sub), :], preferred_element_type=F32)
             for c in range(n_chain)], axis=0)
        scale = jnp.exp2(-drop)
        new_drop = drop + later[:, 0:1] + sp2[:, 0:1]
        if first_valid is not None:
            valid = chain_of_row >= first_valid
            scale = jnp.where(valid, scale, 0.0)
            new_drop = jnp.where(valid, new_drop, drop)
        return new_drop, acc + scale * pv

    def alive(drop, first_valid):
        live = jnp.where(chain_of_row >= first_valid, jnp.exp2(-drop), 0.0)
        return (jnp.max(live) > 0.0).astype(I32)

    drop, acc = sweep([kb0 + c for c in range(n_chain)], jnp.zeros((tq, 1), F32),
                      jnp.zeros((tq, HEAD_DIM), F32), True, None)

    def cond(st):
        return jnp.logical_and(st[0] <= kb0 + n_chain - 1, st[1] > 0)

    def body(st):
        g, _, drop, acc = st
        drop, acc = sweep([jnp.maximum(kb0 + c - g, 0) for c in range(n_chain)], drop, acc, False, g - kb0)
        return g + 1, alive(drop, g + 1 - kb0), drop, acc

    _, _, _, acc = lax.while_loop(cond, body, (jnp.int32(1), alive(drop, 1 - kb0), drop, acc))
    o_ref[...] = acc.astype(o_ref.dtype)


def _stick_breaking(qkv, batch, seq, n_heads, tq, sub):
    assert seq % tq == 0 and tq % sub == 0
    n = batch * seq
    nq = seq // tq
    blk = (tq, HEAD_DIM)
    full = (seq, HEAD_DIM)
    return pl.pallas_call(
        functools.partial(_sb_kernel, tq=tq, sub=sub),
        grid=(batch, n_heads, nq),
        in_specs=[
            pl.BlockSpec(blk, lambda b, h, i: (b * nq + i, h)),
            pl.BlockSpec(full, lambda b, h, i: (b, n_heads + h)),
            pl.BlockSpec(full, lambda b, h, i: (b, 2 * n_heads + h)),
        ],
        out_specs=pl.BlockSpec(blk, lambda b, h, i: (b * nq + i, h)),
        out_shape=jax.ShapeDtypeStruct((n, n_heads * HEAD_DIM), BF16),
        compiler_params=_params(("arbitrary",) * 3, 2 * _nbytes(full, BF16), 2 * _nbytes(blk, F32),
                                4 * _nbytes((tq, sub), F32)),
        name="stick_breaking",
    )(qkv, qkv, qkv)


def _outproj_kernel(od_ref, os_ref, gd_ref, gs_ref, w_ref, x_ref, o_ref, m_scr):
    wd = od_ref.shape[1]

    @pl.when(pl.program_id(1) == 0)
    def _():
        m_scr[:, :wd] = _rms(od_ref[...].astype(F32), gd_ref[...]).astype(BF16)
        m_scr[:, wd:] = _rms(os_ref[...].astype(F32), gs_ref[...]).astype(BF16)

    o_ref[...] = x_ref[...] + jnp.dot(m_scr[...], w_ref[...], preferred_element_type=F32)


def _outproj(o_dil, o_sb, g_dil, g_sb, w, x, tm, tn):
    n, wd = o_dil.shape
    ws = o_sb.shape[1]
    d = x.shape[1]
    return pl.pallas_call(
        _outproj_kernel,
        grid=(n // tm, d // tn),
        in_specs=[
            pl.BlockSpec((tm, wd), lambda i, j: (i, 0)),
            pl.BlockSpec((tm, ws), lambda i, j: (i, 0)),
            pl.BlockSpec((1, wd), lambda i, j: (0, 0)),
            pl.BlockSpec((1, ws), lambda i, j: (0, 0)),
            pl.BlockSpec((wd + ws, tn), lambda i, j: (0, j)),
            pl.BlockSpec((tm, tn), lambda i, j: (i, j)),
        ],
        out_specs=pl.BlockSpec((tm, tn), lambda i, j: (i, j)),
        out_shape=jax.ShapeDtypeStruct((n, d), F32),
        scratch_shapes=[pltpu.VMEM((tm, wd + ws), BF16)],
        compiler_params=_params(("arbitrary", "arbitrary"), _nbytes((tm, wd + ws), BF16),
                                _nbytes((wd + ws, tn), BF16), 2 * _nbytes((tm, tn), F32),
                                _nbytes((tm, wd + ws), BF16) // 2),
        name="out_proj",
    )(o_dil, o_sb, g_dil.reshape(1, wd), g_sb.reshape(1, ws), w, x)


def _mem_kv_kernel(m_ref, gm_ref, w_ref, gk_ref, k_ref, v_ref):
    hm = _rms(m_ref[0], gm_ref[...]).astype(BF16)
    kv = jnp.dot(hm, w_ref[...], preferred_element_type=F32)
    wc = k_ref.shape[2]
    for h in range(wc // HEAD_DIM):
        sl = slice(h * HEAD_DIM, (h + 1) * HEAD_DIM)
        k_ref[0, :, sl] = _rms(kv[:, sl], gk_ref[...]).astype(BF16)
    v_ref[0] = kv[:, wc:].astype(BF16)


def _mem_kv(mem, g_mem, w_kv, g_k):
    b, m, d = mem.shape
    wc = w_kv.shape[1] // 2
    out = jax.ShapeDtypeStruct((b, m, wc), BF16)
    return pl.pallas_call(
        _mem_kv_kernel,
        grid=(b,),
        in_specs=[
            pl.BlockSpec((1, m, d), lambda i: (i, 0, 0)),
            pl.BlockSpec((1, d), lambda i: (0, 0)),
            pl.BlockSpec((d, 2 * wc), lambda i: (0, 0)),
            pl.BlockSpec((1, HEAD_DIM), lambda i: (0, 0)),
        ],
        out_specs=[pl.BlockSpec((1, m, wc), lambda i: (i, 0, 0))] * 2,
        out_shape=[out, out],
        compiler_params=_params(("arbitrary",), _nbytes((m, d), F32), _nbytes((d, 2 * wc), BF16)),
        name="mem_kv",
    )(mem, g_mem.reshape(1, d), w_kv, g_k.reshape(1, HEAD_DIM))


def _pack_bf16_pairs(xr):
    half = xr.shape[1] // 2
    lo = pltpu.bitcast(xr[:, :half], U32)
    hi = pltpu.bitcast(xr[:, half:], U32)
    return hi | (lo >> 16)


def _cross_router_kernel(x1_ref, gc_ref, wq_ref, gq_ref, kc_ref, vc_ref, wo_ref, gf_ref, wrh_ref, wrl_ref, br_ref,
                         x2_ref, hp_ref, eid_ref, gate_ref):
    x1 = x1_ref[...]
    h = _rms(x1, gc_ref[...]).astype(BF16)
    q = jnp.dot(h, wq_ref[...], preferred_element_type=F32)
    heads = []
    for hd in range(q.shape[1] // HEAD_DIM):
        sl = slice(hd * HEAD_DIM, (hd + 1) * HEAD_DIM)
        qh = _rms(q[:, sl], gq_ref[...]).astype(BF16)
        s = lax.dot_general(qh, kc_ref[0, :, sl], (((1,), (1,)), ((), ())), preferred_element_type=F32)
        p = jnp.exp(s - jnp.max(s, axis=1, keepdims=True))
        l = jnp.sum(p, axis=1, keepdims=True)
        heads.append(jnp.dot(p.astype(BF16), vc_ref[0, :, sl], preferred_element_type=F32) / l)
    o = jnp.concatenate(heads, axis=1).astype(BF16)
    x2 = x1 + jnp.dot(o, wo_ref[...], preferred_element_type=F32)
    x2_ref[...] = x2

    hf = _rms(x2, gf_ref[...])
    hf_hi = hf.astype(BF16)
    hf_hi32 = hf_hi.astype(F32)
    hf_lo = (hf - hf_hi32).astype(BF16)
    packed = _pack_bf16_pairs(hf_hi32)
    tm = packed.shape[0]
    nb = packed.shape[1] // LANE
    for j in range(nb):
        hp_ref[pl.ds(j, tm, stride=nb), :] = packed[:, j * LANE:(j + 1) * LANE]

    n_exp = N_GROUPS * EXPERTS_PER_GROUP
    lg = (jnp.dot(hf_hi, wrh_ref[...], preferred_element_type=F32)
          + jnp.dot(hf_hi, wrl_ref[...], preferred_element_type=F32)
          + jnp.dot(hf_lo, wrh_ref[...], preferred_element_type=F32)) + br_ref[...]
    lane = lax.broadcasted_iota(I32, lg.shape, 1)
    lanef = lane.astype(F32)
    ninf = -jnp.inf
    is_g = (lane >= n_exp) & (lane < n_exp + N_GROUPS)
    gl = jnp.where(is_g, lg, ninf)
    gmax = jnp.max(gl, axis=1, keepdims=True)
    gidx = jnp.min(jnp.where(gl == gmax, lanef, 1e9), axis=1, keepdims=True) - n_exp
    g_gate = 1.0 / jnp.sum(jnp.exp(gl - gmax), axis=1, keepdims=True)
    in_grp = (lane < n_exp) & ((lane // EXPERTS_PER_GROUP).astype(F32) == gidx)
    el = jnp.where(in_grp, lg, ninf)
    m1 = jnp.max(el, axis=1, keepdims=True)
    i1 = jnp.min(jnp.where(el == m1, lanef, 1e9), axis=1, keepdims=True)
    el2 = jnp.where(lanef == i1, ninf, el)
    m2 = jnp.max(el2, axis=1, keepdims=True)
    i2 = jnp.min(jnp.where(el2 == m2, lanef, 1e9), axis=1, keepdims=True)
    p2 = jnp.exp(m2 - m1)
    den = 1.0 + p2
    eid_ref[...] = jnp.where(lane == 0, i1, jnp.where(lane == 1, i2, 0.0)).astype(I32)
    gate_ref[...] = jnp.where(lane == 0, g_gate / den, jnp.where(lane == 1, g_gate * p2 / den, 0.0))


def _cross_router(x1, g_cross, wq, gq, kc, vc, wo, g_ffn, wr_hi, wr_lo, br, seq, tm):
    n, d = x1.shape
    wc = wq.shape[1]
    m = kc.shape[1]
    tiles_per_batch = seq // tm
    const = lambda i: (0, 0)
    return pl.pallas_call(
        _cross_router_kernel,
        grid=(n // tm,),
        in_specs=[
            pl.BlockSpec((tm, d), lambda i: (i, 0)),
            pl.BlockSpec((1, d), const),
            pl.BlockSpec((d, wc), const),
            pl.BlockSpec((1, HEAD_DIM), const),
            pl.BlockSpec((1, m, wc), lambda i: (i // tiles_per_batch, 0, 0)),
            pl.BlockSpec((1, m, wc), lambda i: (i // tiles_per_batch, 0, 0)),
            pl.BlockSpec((wc, d), const),
            pl.BlockSpec((1, d), const),
            pl.BlockSpec((d, LANE), const),
            pl.BlockSpec((d, LANE), const),
            pl.BlockSpec((1, LANE), const),
        ],
        out_specs=[
            pl.BlockSpec((tm, d), lambda i: (i, 0)),
            pl.BlockSpec((tm * (d // 2 // LANE), LANE), lambda i: (i, 0)),
            pl.BlockSpec((tm, LANE), lambda i: (i, 0)),
            pl.BlockSpec((tm, LANE), lambda i: (i, 0)),
        ],
        out_shape=[
            jax.ShapeDtypeStruct((n, d), F32),
            jax.ShapeDtypeStruct((n * (d // 2 // LANE), LANE), U32),
            jax.ShapeDtypeStruct((n, LANE), I32),
            jax.ShapeDtypeStruct((n, LANE), F32),
        ],
        compiler_params=_params(("arbitrary",), 2 * _nbytes((tm, d), F32), _nbytes((tm, d // 2), U32),
                                2 * _nbytes((d, wc), BF16), 2 * _nbytes((d, LANE), BF16)),
        name="cross_router",
    )(x1, g_cross.reshape(1, d), wq, gq.reshape(1, HEAD_DIM), kc, vc, wo, g_ffn.reshape(1, d), wr_hi, wr_lo, br)


def _moe_up_kernel(nused_ref, te_ref, tblk_ref, tok_ref, tok_next_ref, hp_ref, wg_ref, wu_ref, h_ref,
                   x_scr, stage, g_acc, u_acc, sem, *, rt, nb, kd):
    t = pl.program_id(0)
    k = pl.program_id(1)
    n_used = nused_ref[0]
    half = nb * LANE
    himask = jnp.uint32(0xFFFF0000)

    def start_tile(tok):
        def body(r, c):
            pltpu.make_async_copy(hp_ref.at[pl.ds(tok[0, 0, r] * nb, nb)],
                                  stage.at[pl.ds(r * nb, nb)], sem).start(priority=1)
            return c

        lax.fori_loop(0, rt, body, 0)

    def put(col, val):
        x_scr[col // kd, :, col % kd:col % kd + LANE] = val.astype(BF16)

    @pl.when(t < n_used)
    def _():
        @pl.when(k == 0)
        def _():
            @pl.when(t == 0)
            def _():
                start_tile(tok_ref)

            pltpu.make_async_copy(hp_ref.at[pl.ds(0, rt * nb)], stage, sem).wait()
            for j in range(nb):
                w = stage[pl.ds(j, rt, stride=nb), :]
                put(j * LANE, pltpu.bitcast(w << 16, F32))
                put(half + j * LANE, pltpu.bitcast(w & himask, F32))

            @pl.when(t + 1 < n_used)
            def _():
                start_tile(tok_next_ref)

        x = x_scr[k]
        g = jnp.dot(x, wg_ref[0].astype(BF16), preferred_element_type=F32)
        u = jnp.dot(x, wu_ref[0].astype(BF16), preferred_element_type=F32)

        @pl.when(k == 0)
        def _():
            g_acc[...] = g
            u_acc[...] = u

        @pl.when(k > 0)
        def _():
            g_acc[...] += g
            u_acc[...] += u

        @pl.when(k == pl.num_programs(1) - 1)
        def _():
            gg = g_acc[...]
            h_ref[...] = (gg * jax.nn.sigmoid(gg) * u_acc[...]).astype(BF16)


def _moe_down_kernel(nused_ref, te_ref, tblk_ref, h_ref, wd_ref, y_ref):
    @pl.when(pl.program_id(0) < nused_ref[0])
    def _():
        y = jnp.dot(h_ref[...], wd_ref[0].astype(BF16), preferred_element_type=F32)
        y_ref[...] = _pack_bf16_pairs(y.astype(BF16).astype(F32))


def _moe_experts(n_used, tile_e, tile_blk, slot_tok, hp, w_gate, w_up, w_down, n, n_tiles, rt, kd):
    n_exp, d, de = w_gate.shape
    nb = hp.shape[0] // n
    assert 2 * nb * LANE == d and kd % LANE == 0 and hp.shape[0] >= rt * nb
    p_rows = n_tiles * rt
    nk = d // kd
    tok3 = slot_tok.reshape(n_tiles, 1, rt)

    def inner(t, k, nused, last):
        return jnp.where(t < nused[0], k, last)

    h_mid = pl.pallas_call(
        functools.partial(_moe_up_kernel, rt=rt, nb=nb, kd=kd),
        grid_spec=pltpu.PrefetchScalarGridSpec(
            num_scalar_prefetch=3,
            grid=(n_tiles, nk),
            in_specs=[
                pl.BlockSpec((1, 1, rt), lambda t, k, nu, te, tb: (t, 0, 0), memory_space=pltpu.SMEM),
                pl.BlockSpec((1, 1, rt), lambda t, k, nu, te, tb: (jnp.minimum(t + 1, n_tiles - 1), 0, 0),
                             memory_space=pltpu.SMEM),
                pl.BlockSpec(memory_space=pl.ANY),
                pl.BlockSpec((1, kd, de), lambda t, k, nu, te, tb: (te[t], inner(t, k, nu, nk - 1), 0)),
                pl.BlockSpec((1, kd, de), lambda t, k, nu, te, tb: (te[t], inner(t, k, nu, nk - 1), 0)),
            ],
            out_specs=pl.BlockSpec((rt, de), lambda t, k, nu, te, tb: (tb[t], 0)),
            scratch_shapes=[pltpu.VMEM((nk, rt, kd), BF16), pltpu.VMEM((rt * nb, LANE), U32),
                            pltpu.VMEM((rt, de), F32), pltpu.VMEM((rt, de), F32), pltpu.SemaphoreType.DMA(())],
        ),
        out_shape=jax.ShapeDtypeStruct((p_rows, de), BF16),
        compiler_params=_params(("arbitrary", "arbitrary"), 2 * _nbytes((kd, de), F32), _nbytes((rt, de), F32),
                                _nbytes((rt * nb, LANE), U32) // 2, _nbytes((rt, d), BF16) // 2),
        name="moe_gate_up",
    )(n_used, tile_e, tile_blk, tok3, tok3, hp, w_gate, w_up)

    return pl.pallas_call(
        _moe_down_kernel,
        grid_spec=pltpu.PrefetchScalarGridSpec(
            num_scalar_prefetch=3,
            grid=(n_tiles,),
            in_specs=[
                pl.BlockSpec((rt, de), lambda t, nu, te, tb: (tb[t], 0)),
                pl.BlockSpec((1, de, d), lambda t, nu, te, tb: (te[t], 0, 0)),
            ],
            out_specs=pl.BlockSpec((rt, d // 2), lambda t, nu, te, tb: (tb[t], 0)),
        ),
        out_shape=jax.ShapeDtypeStruct((p_rows, d // 2), U32),
        compiler_params=_params(("arbitrary",), _nbytes((rt, de), BF16), _nbytes((de, d), F32),
                                _nbytes((rt, d // 2), U32), _nbytes((rt, d), F32) // 2),
        name="moe_down",
    )(n_used, tile_e, tile_blk, h_mid, w_down)


def _moe_combine_kernel(pos_ref, pos_next_ref, x_ref, g_ref, y_ref, o_ref, buf0, buf1, sem0, sem1, *, tm, top_k):
    i = pl.program_id(0)
    bufs, sems = (buf0, buf1), (sem0, sem1)

    def start_step(pos, slot):
        def body(r, c):
            for k in range(top_k):
                pltpu.make_async_copy(y_ref.at[pl.ds(pos[0, 0, top_k * r + k], 1)],
                                      bufs[slot].at[pl.ds(k * tm + r, 1)], sems[slot]).start(priority=k % 2)
            return c

        lax.fori_loop(0, tm, body, 0, unroll=8)

    for slot in range(2):
        @pl.when(i % 2 == slot)
        def _(slot=slot):
            if slot == 0:
                @pl.when(i == 0)
                def _():
                    start_step(pos_ref, 0)

            @pl.when(i + 1 < pl.num_programs(0))
            def _():
                start_step(pos_next_ref, 1 - slot)

            pltpu.make_async_copy(y_ref.at[pl.ds(0, top_k * tm)], bufs[slot], sems[slot]).wait()
            half = x_ref.shape[1] // 2
            lo = x_ref[:, :half]
            hi = x_ref[:, half:]
            for k in range(top_k):
                w = bufs[slot][k * tm:(k + 1) * tm, :]
                g = g_ref[:, k:k + 1]
                lo = lo + g * pltpu.bitcast(w << 16, F32)
                hi = hi + g * pltpu.bitcast(w & jnp.uint32(0xFFFF0000), F32)
            o_ref[:, :half] = lo
            o_ref[:, half:] = hi


def _moe_combine(dest, x2, gate, y, tm, top_k):
    n, d = x2.shape
    steps = n // tm
    assert y.shape[0] >= top_k * tm
    pos3 = dest.reshape(steps, 1, tm * top_k)
    stage = pltpu.VMEM((top_k * tm, d // 2), U32)
    return pl.pallas_call(
        functools.partial(_moe_combine_kernel, tm=tm, top_k=top_k),
        grid=(steps,),
        in_specs=[
            pl.BlockSpec((1, 1, tm * top_k), lambda i: (i, 0, 0), memory_space=pltpu.SMEM),
            pl.BlockSpec((1, 1, tm * top_k), lambda i: (jnp.minimum(i + 1, steps - 1), 0, 0),
                         memory_space=pltpu.SMEM),
            pl.BlockSpec((tm, d), lambda i: (i, 0)),
            pl.BlockSpec((tm, LANE), lambda i: (i, 0)),
            pl.BlockSpec(memory_space=pl.ANY),
        ],
        out_specs=pl.BlockSpec((tm, d), lambda i: (i, 0)),
        out_shape=jax.ShapeDtypeStruct((n, d), F32),
        scratch_shapes=[stage, stage, pltpu.SemaphoreType.DMA(()), pltpu.SemaphoreType.DMA(())],
        compiler_params=_params(("arbitrary",), 2 * _nbytes((tm, d), F32), _nbytes((top_k * tm, d // 2), U32)),
        name="moe_combine",
    )(pos3, pos3, x2, gate, y)


def _moe_schedule(eid, n_exp, rt, n_tiles):
    n, top_k = eid.shape
    a = n * top_k
    eflat = eid.reshape(a)
    onehot = (eflat[:, None] == jnp.arange(n_exp, dtype=I32)[None, :]).astype(I32)
    csum = jnp.cumsum(onehot, axis=0)
    rank = jnp.sum(csum * onehot, axis=1) - 1
    counts = csum[-1]
    nt = (counts + rt - 1) // rt
    tcum = jnp.cumsum(nt)
    tstart = tcum - nt
    n_used = tcum[-1]
    dest = tstart[eflat] * rt + rank
    tid = jnp.arange(n_tiles, dtype=I32)
    tile_blk = jnp.minimum(tid, n_used - 1)
    tile_e = jnp.minimum(jnp.searchsorted(tcum, tile_blk, side="right"), n_exp - 1).astype(I32)
    tok = jnp.arange(a, dtype=I32) // top_k
    slot_tok = jnp.zeros((n_tiles * rt,), I32).at[dest].set(tok)
    return n_used.reshape(1).astype(I32), tile_e, tile_blk.astype(I32), slot_tok, dest.astype(I32)


def _pick(n, pref):
    t = min(n, pref)
    assert n % t == 0
    return t


def kernel(x, mem, g_mix, w_in, g_q_dil, g_k_dil, g_out_dil, g_out_sb, w_out, g_cross, g_mem, w_q_cross, w_kv_cross, g_q_cross, g_k_cross, w_o_cross, g_ffn, w_group, b_group, w_router, b_router, w_exp_gate, w_exp_up, w_exp_down):
    batch, seq, d = x.shape
    n = batch * seq
    n_heads_mix = d // HEAD_DIM
    n_dil = n_heads_mix // 2
    n_sb = n_heads_mix - n_dil
    w_dil, w_sb = n_dil * HEAD_DIM, n_sb * HEAD_DIM
    n_exp = w_exp_gate.shape[1]
    top_k = 2
    scale = 1.0 / math.sqrt(HEAD_DIM)
    slopes = jnp.exp2(-ALIBI_MAX_BIAS * jnp.arange(1, n_dil + 1, dtype=F32) / n_dil)

    tm = _pick(n, 512)
    rt = -(-(9 * n * top_k) // (8 * n_exp * MOE_ROW_ALIGN)) * MOE_ROW_ALIGN
    n_tiles = -(-n * top_k // rt) + n_exp
    kd = max(d // 2, LANE)

    xf = x.reshape(n, d)
    for l in range(g_mix.shape[0]):
        h = _rmsnorm_cast(xf, g_mix[l], tm)
        w_in_b = w_in[l].astype(BF16)
        gain_dil = jnp.concatenate([jnp.tile(g_q_dil[l] * scale, n_dil), jnp.tile(g_k_dil[l], n_dil),
                                    jnp.ones((w_dil,), F32)])
        gain_sb = jnp.concatenate([jnp.full((w_sb,), scale * math.log2(math.e), F32), jnp.ones((2 * w_sb,), F32)])
        tn = _pick(w_dil, 1024)
        tmp = _pick(n, 1024)
        qkv_dil = _proj(h, w_in_b, gain_dil, 0, 3 * w_dil, 2 * w_dil, F32, tmp, tn, "in_proj_dil")
        qkv_sb = _proj(h, w_in_b, gain_sb, 3 * w_dil, 3 * w_sb, 0, BF16, tmp, tn, "in_proj_sb")
        o_dil = _dilated(qkv_dil, slopes, batch, seq, n_dil, _pick(seq, 2048))
        o_sb = _stick_breaking(qkv_sb, batch, seq, n_sb, _pick(seq, 2048), _pick(seq, 256))
        x1 = _outproj(o_dil, o_sb, g_out_dil[l], g_out_sb[l], w_out[l].astype(BF16), xf, _pick(n, 1024), _pick(d, 512))

        kc, vc = _mem_kv(mem, g_mem[l], w_kv_cross[l].astype(BF16), g_k_cross[l])
        wr = jnp.concatenate([w_router[l], w_group[l],
                              jnp.zeros((d, LANE - n_exp - N_GROUPS), F32)], axis=1)
        wr_hi = wr.astype(BF16)
        wr_lo = (wr - wr_hi.astype(F32)).astype(BF16)
        br = jnp.concatenate([b_router[l].reshape(-1), b_group[l],
                              jnp.zeros((LANE - n_exp - N_GROUPS,), F32)]).reshape(1, LANE)
        x2, hp, eid, gate = _cross_router(x1, g_cross[l], w_q_cross[l].astype(BF16), g_q_cross[l] * scale, kc, vc,
                                          w_o_cross[l].astype(BF16), g_ffn[l], wr_hi, wr_lo, br, seq,
                                          _pick(seq, 256))

        n_used, tile_e, tile_blk, slot_tok, dest = _moe_schedule(eid[:, :top_k], n_exp, rt, n_tiles)
        y = _moe_experts(n_used, tile_e, tile_blk, slot_tok, hp, w_exp_gate[l], w_exp_up[l], w_exp_down[l],
                         n, n_tiles, rt, kd)
        xf = _moe_combine(dest, x2, gate, y, _pick(n, 256), top_k)
    return xf.reshape(batch, seq, d)
```

```python
import functools
import math

import jax
import jax.numpy as jnp
from jax import lax
from jax.experimental import pallas as pl
from jax.experimental.pallas import tpu as pltpu

F32 = jnp.float32
BF16 = jnp.bfloat16
I32 = jnp.int32
U32 = jnp.uint32

LANE = 128
HEAD_DIM = 128
EPS = 1e-6
DIL_PAIRS = ((128, 1), (512, 4), (2048, 16))
DIL_STEPS = 128
DIL_UNROLL = 16
ALIBI_MAX_BIAS = 8.0
N_GROUPS = 8
EXPERTS_PER_GROUP = 8
MOE_ROW_ALIGN = 64
NEG_BIG = -1e30
VMEM_LIMIT_CAP = 60000 * 1024
VMEM_INTERNAL = 12 * 1024 * 1024
LANE_NONE = 1e9


def _vmem_limit(*block_bytes):
    return int(min(VMEM_LIMIT_CAP, 2 * sum(block_bytes) + VMEM_INTERNAL))


def _nbytes(shape, dtype):
    return math.prod(shape) * jnp.dtype(dtype).itemsize


def _params(sem, *block_bytes):
    return pltpu.CompilerParams(dimension_semantics=sem, vmem_limit_bytes=_vmem_limit(*block_bytes))


def _rms(x, g):
    return x * lax.rsqrt(jnp.mean(x * x, axis=-1, keepdims=True) + EPS) * g


def _rmsnorm_cast_kernel(x_ref, g_ref, o_ref):
    o_ref[...] = _rms(x_ref[...], g_ref[...]).astype(o_ref.dtype)


def _rmsnorm_cast(x, g, tm):
    n, d = x.shape
    return pl.pallas_call(
        _rmsnorm_cast_kernel,
        grid=(n // tm,),
        in_specs=[pl.BlockSpec((tm, d), lambda i: (i, 0)), pl.BlockSpec((1, d), lambda i: (0, 0))],
        out_specs=pl.BlockSpec((tm, d), lambda i: (i, 0)),
        out_shape=jax.ShapeDtypeStruct((n, d), BF16),
        compiler_params=_params(("arbitrary",), _nbytes((tm, d), F32), _nbytes((tm, d), BF16)),
        name="rmsnorm_cast",
    )(x, g.reshape(1, d))


def _proj_kernel(a_ref, w_ref, g_ref, o_ref, *, n_norm_tiles):
    j = pl.program_id(1)
    acc = jnp.dot(a_ref[...], w_ref[...], preferred_element_type=F32)
    tn = acc.shape[1]

    @pl.when(j < n_norm_tiles)
    def _():
        for h in range(tn // HEAD_DIM):
            sl = slice(h * HEAD_DIM, (h + 1) * HEAD_DIM)
            o_ref[:, sl] = _rms(acc[:, sl], g_ref[:, sl]).astype(o_ref.dtype)

    @pl.when(j >= n_norm_tiles)
    def _():
        o_ref[...] = (acc * g_ref[...]).astype(o_ref.dtype)


def _proj(a, w, gain, col0, ncols, norm_cols, out_dtype, tm, tn, name):
    n, k = a.shape
    assert col0 % tn == 0 and ncols % tn == 0 and norm_cols % tn == 0 and n % tm == 0
    jb = col0 // tn
    return pl.pallas_call(
        functools.partial(_proj_kernel, n_norm_tiles=norm_cols // tn),
        grid=(n // tm, ncols // tn),
        in_specs=[
            pl.BlockSpec((tm, k), lambda i, j: (i, 0)),
            pl.BlockSpec((k, tn), lambda i, j: (0, j + jb)),
            pl.BlockSpec((1, tn), lambda i, j: (0, j)),
        ],
        out_specs=pl.BlockSpec((tm, tn), lambda i, j: (i, j)),
        out_shape=jax.ShapeDtypeStruct((n, ncols), out_dtype),
        compiler_params=_params(("arbitrary", "arbitrary"), _nbytes((tm, k), BF16), _nbytes((k, tn), BF16),
                                _nbytes((tm, tn), F32)),
        name=name,
    )(a, w, gain.reshape(1, ncols))


def _dil_kernel(slope_ref, q_ref, k_ref, v_ref, o_ref, ob0, ob1, ob2, ls0, ls1, ls2, *, tq):
    h = pl.program_id(1)
    t0 = pl.program_id(2) * tq
    slope = slope_ref[h]
    J = DIL_STEPS
    qi = lax.broadcasted_iota(I32, (J, 2 * J), 0)
    ki = lax.broadcasted_iota(I32, (J, 2 * J), 1)
    dist = J + qi - ki
    valid = (dist >= 0) & (dist <= J)
    distf = dist.astype(F32)
    in_cur = ki >= J
    ones_v = jnp.ones((2 * J, HEAD_DIM), BF16)

    for (window, d), ob, ls in zip(DIL_PAIRS, (ob0, ob1, ob2), (ls0, ls1, ls2)):
        assert window // d == J and tq % (d * J) == 0
        bias = jnp.where(valid, (-slope * d) * distf, NEG_BIG)
        shift = d.bit_length() - 1

        def ld(ref, start, d=d):
            if d == 1:
                return ref[pl.ds(start, J), :]
            return ref[pl.ds(start, J, stride=d), :]

        def unit(u, carry, d=d, shift=shift, bias=bias, ob=ob, ls=ls, ld=ld):
            off = (u & (d - 1)) + (u >> shift) * (d * J)
            cur0 = t0 + off
            prev0 = cur0 - d * J
            has_prev = prev0 >= 0
            prev0 = jnp.maximum(prev0, 0)
            qs = ld(q_ref, off).astype(BF16)
            kcat = jnp.concatenate([ld(k_ref, prev0), ld(k_ref, cur0)], axis=0).astype(BF16)
            vcat = jnp.concatenate([ld(v_ref, prev0), ld(v_ref, cur0)], axis=0).astype(BF16)
            s = lax.dot_general(qs, kcat, (((1,), (1,)), ((), ())), preferred_element_type=F32) + bias
            s = jnp.where(jnp.logical_or(in_cur, has_prev), s, NEG_BIG)
            m = jnp.max(s, axis=1, keepdims=True)
            p = jnp.exp(s - m)
            ov = jnp.dot(p.astype(BF16), jnp.concatenate([vcat, ones_v], axis=1), preferred_element_type=F32)
            l = ov[:, HEAD_DIM:]
            o = ov[:, :HEAD_DIM] / l
            lse = m + jnp.log(l)
            if d == 1:
                ob[pl.ds(off, J), :] = o
                ls[pl.ds(off, J), :] = lse
            else:
                ob[pl.ds(off, J, stride=d), :] = o
                ls[pl.ds(off, J, stride=d), :] = lse
            return carry

        lax.fori_loop(0, tq // J, unit, 0, unroll=DIL_UNROLL)

    l0, l1, l2 = ls0[...], ls1[...], ls2[...]
    mx = jnp.maximum(jnp.maximum(l0, l1), l2)
    w0, w1, w2 = jnp.exp(l0 - mx), jnp.exp(l1 - mx), jnp.exp(l2 - mx)
    o_ref[...] = ((w0 * ob0[...] + w1 * ob1[...] + w2 * ob2[...]) / (w0 + w1 + w2)).astype(o_ref.dtype)


def _dilated(qkv, slopes, batch, seq, n_heads, tq):
    assert seq % tq == 0
    n = batch * seq
    nq = seq // tq
    blk = (tq, HEAD_DIM)
    full = (seq, HEAD_DIM)
    scr = [pltpu.VMEM(blk, F32)] * 6
    return pl.pallas_call(
        functools.partial(_dil_kernel, tq=tq),
        grid=(batch, n_heads, nq),
        in_specs=[
            pl.BlockSpec(memory_space=pltpu.SMEM),
            pl.BlockSpec(blk, lambda b, h, c: (b * nq + c, h)),
            pl.BlockSpec(full, lambda b, h, c: (b, n_heads + h)),
            pl.BlockSpec(full, lambda b, h, c: (b, 2 * n_heads + h)),
        ],
        out_specs=pl.BlockSpec(blk, lambda b, h, c: (b * nq + c, h)),
        out_shape=jax.ShapeDtypeStruct((n, n_heads * HEAD_DIM), BF16),
        scratch_shapes=scr,
        compiler_params=_params(("arbitrary",) * 3, 2 * _nbytes(full, F32), 5 * _nbytes(blk, F32)),
        name="dilated_attn",
    )(slopes, qkv, qkv, qkv)


def _sb_kernel(q_ref, k_ref, v_ref, o_ref, *, tq, sub):
    n_chain = tq // sub
    assert sub & (sub - 1) == 0
    kb0 = pl.program_id(2) * n_chain
    row = lax.broadcasted_iota(I32, (sub, sub), 0)
    col = lax.broadcasted_iota(I32, (sub, sub), 1)
    after = (row > col).astype(BF16)
    after2 = jnp.concatenate([after, after], axis=0)
    qrow = lax.broadcasted_iota(I32, (tq, sub), 0)
    causal = lax.broadcasted_iota(I32, (tq, sub), 1) < (qrow & (sub - 1))
    chain_of_row = lax.broadcasted_iota(I32, (tq, 1), 0) >> (sub.bit_length() - 1)
    sign = jnp.uint32(0x80000000)

    def sweep(kbs, drop, acc, diag, first_valid):
        starts = [pl.multiple_of(kb * sub, sub) for kb in kbs]
        z2 = jnp.concatenate(
            [lax.dot_general(q_ref[c * sub:(c + 1) * sub, :], k_ref[pl.ds(starts[c], sub), :],
                             (((1,), (1,)), ((), ())), preferred_element_type=F32) for c in range(n_chain)], axis=0)
        neg_abs = pltpu.bitcast(pltpu.bitcast(z2, U32) | sign, F32)
        sp2 = jnp.maximum(z2, 0.0) + jnp.log2(1.0 + jnp.exp2(neg_abs))
        log_beta2 = z2 - sp2
        if diag:
            sp2 = jnp.where(causal, sp2, 0.0)
        hi = sp2.astype(BF16)
        lo = (sp2 - hi.astype(F32)).astype(BF16)
        later = jnp.dot(jnp.concatenate([hi, lo], axis=1), after2, preferred_element_type=F32)
        a = jnp.exp2(log_beta2 - later)
        if diag:
            a = jnp.where(causal, a, 0.0)
        a = a.astype(BF16)
        pv = jnp.concatenate(
            [jnp.dot(a[c * sub:(c + 1) * sub, :], v_ref[pl.ds(starts[c], sub), :], preferred_element_type=F32)
             for c in range(n_chain)], axis=0)
        scale = jnp.exp2(-drop)
        new_drop = drop + later[:, 0:1] + sp2[:, 0:1]
        if first_valid is not None:
            valid = chain_of_row >= first_valid
            scale = jnp.where(valid, scale, 0.0)
            new_drop = jnp.where(valid, new_drop, drop)
        return new_drop, acc + scale * pv

    def alive(drop, first_valid):
        live = jnp.where(chain_of_row >= first_valid, jnp.exp2(-drop), 0.0)
        return (jnp.max(live) > 0.0).astype(I32)

    drop, acc = sweep([kb0 + c for c in range(n_chain)], jnp.zeros((tq, 1), F32),
                      jnp.zeros((tq, HEAD_DIM), F32), True, None)

    def cond(st):
        return jnp.logical_and(st[0] <= kb0 + n_chain - 1, st[1] > 0)

    def body(st):
        g, _, drop, acc = st
        drop, acc = sweep([jnp.maximum(kb0 + c - g, 0) for c in range(n_chain)], drop, acc, False, g - kb0)
        return g + 1, alive(drop, g + 1 - kb0), drop, acc

    _, _, _, acc = lax.while_loop(cond, body, (jnp.int32(1), alive(drop, 1 - kb0), drop, acc))
    o_ref[...] = acc.astype(o_ref.dtype)


def _stick_breaking(qkv, batch, seq, n_heads, tq, sub):
    assert seq % tq == 0 and tq % sub == 0
    n = batch * seq
    nq = seq // tq
    blk = (tq, HEAD_DIM)
    full = (seq, HEAD_DIM)
    return pl.pallas_call(
        functools.partial(_sb_kernel, tq=tq, sub=sub),
        grid=(batch, n_heads, nq),
        in_specs=[
            pl.BlockSpec(blk, lambda b, h, i: (b * nq + i, h)),
            pl.BlockSpec(full, lambda b, h, i: (b, n_heads + h)),
            pl.BlockSpec(full, lambda b, h, i: (b, 2 * n_heads + h)),
        ],
        out_specs=pl.BlockSpec(blk, lambda b, h, i: (b * nq + i, h)),
        out_shape=jax.ShapeDtypeStruct((n, n_heads * HEAD_DIM), BF16),
        compiler_params=_params(("arbitrary",) * 3, 2 * _nbytes(full, BF16), 2 * _nbytes(blk, F32),
                                4 * _nbytes((tq, sub), F32)),
        name="stick_breaking",
    )(qkv, qkv, qkv)


def _outproj_kernel(od_ref, os_ref, gd_ref, gs_ref, w_ref, x_ref, o_ref, m_scr):
    wd = od_ref.shape[1]

    @pl.when(pl.program_id(1) == 0)
    def _():
        m_scr[:, :wd] = _rms(od_ref[...].astype(F32), gd_ref[...]).astype(BF16)
        m_scr[:, wd:] = _rms(os_ref[...].astype(F32), gs_ref[...]).astype(BF16)

    o_ref[...] = x_ref[...] + jnp.dot(m_scr[...], w_ref[...], preferred_element_type=F32)


def _outproj(o_dil, o_sb, g_dil, g_sb, w, x, tm, tn):
    n, wd = o_dil.shape
    ws = o_sb.shape[1]
    d = x.shape[1]
    return pl.pallas_call(
        _outproj_kernel,
        grid=(n // tm, d // tn),
        in_specs=[
            pl.BlockSpec((tm, wd), lambda i, j: (i, 0)),
            pl.BlockSpec((tm, ws), lambda i, j: (i, 0)),
            pl.BlockSpec((1, wd), lambda i, j: (0, 0)),
            pl.BlockSpec((1, ws), lambda i, j: (0, 0)),
            pl.BlockSpec((wd + ws, tn), lambda i, j: (0, j)),
            pl.BlockSpec((tm, tn), lambda i, j: (i, j)),
        ],
        out_specs=pl.BlockSpec((tm, tn), lambda i, j: (i, j)),
        out_shape=jax.ShapeDtypeStruct((n, d), F32),
        scratch_shapes=[pltpu.VMEM((tm, wd + ws), BF16)],
        compiler_params=_params(("arbitrary", "arbitrary"), _nbytes((tm, wd + ws), BF16),
                                _nbytes((wd + ws, tn), BF16), 2 * _nbytes((tm, tn), F32),
                                _nbytes((tm, wd + ws), BF16) // 2),
        name="out_proj",
    )(o_dil, o_sb, g_dil.reshape(1, wd), g_sb.reshape(1, ws), w, x)


def _mem_kv_kernel(m_ref, gm_ref, w_ref, gk_ref, k_ref, v_ref):
    hm = _rms(m_ref[0], gm_ref[...]).astype(BF16)
    kv = jnp.dot(hm, w_ref[...], preferred_element_type=F32)
    wc = k_ref.shape[2]
    for h in range(wc // HEAD_DIM):
        sl = slice(h * HEAD_DIM, (h + 1) * HEAD_DIM)
        k_ref[0, :, sl] = _rms(kv[:, sl], gk_ref[...]).astype(BF16)
    v_ref[0] = kv[:, wc:].astype(BF16)


def _mem_kv(mem, g_mem, w_kv, g_k):
    b, m, d = mem.shape
    wc = w_kv.shape[1] // 2
    out = jax.ShapeDtypeStruct((b, m, wc), BF16)
    return pl.pallas_call(
        _mem_kv_kernel,
        grid=(b,),
        in_specs=[
            pl.BlockSpec((1, m, d), lambda i: (i, 0, 0)),
            pl.BlockSpec((1, d), lambda i: (0, 0)),
            pl.BlockSpec((d, 2 * wc), lambda i: (0, 0)),
            pl.BlockSpec((1, HEAD_DIM), lambda i: (0, 0)),
        ],
        out_specs=[pl.BlockSpec((1, m, wc), lambda i: (i, 0, 0))] * 2,
        out_shape=[out, out],
        compiler_params=_params(("arbitrary",), _nbytes((m, d), F32), _nbytes((d, 2 * wc), BF16)),
        name="mem_kv",
    )(mem, g_mem.reshape(1, d), w_kv, g_k.reshape(1, HEAD_DIM))


def _pack_bf16_pairs(xr):
    half = xr.shape[1] // 2
    lo = pltpu.bitcast(xr[:, :half], U32)
    hi = pltpu.bitcast(xr[:, half:], U32)
    return hi | (lo >> 16)


def _cross_router_kernel(x1_ref, gc_ref, wq_ref, gq_ref, kc_ref, vc_ref, wo_ref, gf_ref, wrh_ref, wrl_ref, br_ref,
                         x2_ref, hp_ref, eid_ref, gate_ref):
    x1 = x1_ref[...]
    h = _rms(x1, gc_ref[...]).astype(BF16)
    q = jnp.dot(h, wq_ref[...], preferred_element_type=F32)
    heads = []
    for hd in range(q.shape[1] // HEAD_DIM):
        sl = slice(hd * HEAD_DIM, (hd + 1) * HEAD_DIM)
        qh = _rms(q[:, sl], gq_ref[...]).astype(BF16)
        s = lax.dot_general(qh, kc_ref[0, :, sl], (((1,), (1,)), ((), ())), preferred_element_type=F32)
        p = jnp.exp(s - jnp.max(s, axis=1, keepdims=True))
        l = jnp.sum(p, axis=1, keepdims=True)
        heads.append(jnp.dot(p.astype(BF16), vc_ref[0, :, sl], preferred_element_type=F32) / l)
    o = jnp.concatenate(heads, axis=1).astype(BF16)
    x2 = x1 + jnp.dot(o, wo_ref[...], preferred_element_type=F32)
    x2_ref[...] = x2

    hf = _rms(x2, gf_ref[...])
    hf_hi = hf.astype(BF16)
    hf_hi32 = hf_hi.astype(F32)
    hf_lo = (hf - hf_hi32).astype(BF16)
    packed = _pack_bf16_pairs(hf_hi32)
    tm = packed.shape[0]
    nb = packed.shape[1] // LANE
    for j in range(nb):
        hp_ref[pl.ds(j, tm, stride=nb), :] = packed[:, j * LANE:(j + 1) * LANE]

    n_exp = N_GROUPS * EXPERTS_PER_GROUP
    lg = (jnp.dot(hf_hi, wrh_ref[...], preferred_element_type=F32)
          + jnp.dot(hf_hi, wrl_ref[...], preferred_element_type=F32)
          + jnp.dot(hf_lo, wrh_ref[...], preferred_element_type=F32)) + br_ref[...]
    lane = lax.broadcasted_iota(I32, lg.shape, 1)
    lanef = lane.astype(F32)
    ninf = -jnp.inf
    is_g = (lane >= n_exp) & (lane < n_exp + N_GROUPS)
    gl = jnp.where(is_g, lg, ninf)
    gmax = jnp.max(gl, axis=1, keepdims=True)
    gidx = jnp.min(jnp.where(gl == gmax, lanef, LANE_NONE), axis=1, keepdims=True) - n_exp
    g_gate = 1.0 / jnp.sum(jnp.exp(gl - gmax), axis=1, keepdims=True)
    in_grp = (lane < n_exp) & ((lane // EXPERTS_PER_GROUP).astype(F32) == gidx)
    el = jnp.where(in_grp, lg, ninf)
    m1 = jnp.max(el, axis=1, keepdims=True)
    i1 = jnp.min(jnp.where(el == m1, lanef, LANE_NONE), axis=1, keepdims=True)
    el2 = jnp.where(lanef == i1, ninf, el)
    m2 = jnp.max(el2, axis=1, keepdims=True)
    i2 = jnp.min(jnp.where(el2 == m2, lanef, LANE_NONE), axis=1, keepdims=True)
    p2 = jnp.exp(m2 - m1)
    den = 1.0 + p2
    eid_ref[...] = jnp.where(lane == 0, i1, jnp.where(lane == 1, i2, 0.0)).astype(I32)
    gate_ref[...] = jnp.where(lane == 0, g_gate / den, jnp.where(lane == 1, g_gate * p2 / den, 0.0))


def _cross_router(x1, g_cross, wq, gq, kc, vc, wo, g_ffn, wr_hi, wr_lo, br, seq, tm):
    n, d = x1.shape
    wc = wq.shape[1]
    m = kc.shape[1]
    tiles_per_batch = seq // tm
    const = lambda i: (0, 0)
    return pl.pallas_call(
        _cross_router_kernel,
        grid=(n // tm,),
        in_specs=[
            pl.BlockSpec((tm, d), lambda i: (i, 0)),
            pl.BlockSpec((1, d), const),
            pl.BlockSpec((d, wc), const),
            pl.BlockSpec((1, HEAD_DIM), const),
            pl.BlockSpec((1, m, wc), lambda i: (i // tiles_per_batch, 0, 0)),
            pl.BlockSpec((1, m, wc), lambda i: (i // tiles_per_batch, 0, 0)),
            pl.BlockSpec((wc, d), const),
            pl.BlockSpec((1, d), const),
            pl.BlockSpec((d, LANE), const),
            pl.BlockSpec((d, LANE), const),
            pl.BlockSpec((1, LANE), const),
        ],
        out_specs=[
            pl.BlockSpec((tm, d), lambda i: (i, 0)),
            pl.BlockSpec((tm * (d // 2 // LANE), LANE), lambda i: (i, 0)),
            pl.BlockSpec((tm, LANE), lambda i: (i, 0)),
            pl.BlockSpec((tm, LANE), lambda i: (i, 0)),
        ],
        out_shape=[
            jax.ShapeDtypeStruct((n, d), F32),
            jax.ShapeDtypeStruct((n * (d // 2 // LANE), LANE), U32),
            jax.ShapeDtypeStruct((n, LANE), I32),
            jax.ShapeDtypeStruct((n, LANE), F32),
        ],
        compiler_params=_params(("arbitrary",), 2 * _nbytes((tm, d), F32), _nbytes((tm, d // 2), U32),
                                2 * _nbytes((d, wc), BF16), 2 * _nbytes((d, LANE), BF16)),
        name="cross_router",
    )(x1, g_cross.reshape(1, d), wq, gq.reshape(1, HEAD_DIM), kc, vc, wo, g_ffn.reshape(1, d), wr_hi, wr_lo, br)


def _moe_up_kernel(nused_ref, te_ref, tblk_ref, tok_ref, tok_next_ref, hp_ref, wg_ref, wu_ref, h_ref,
                   x_scr, stage, g_acc, u_acc, sem, *, rt, nb, kd):
    t = pl.program_id(0)
    k = pl.program_id(1)
    n_used = nused_ref[0]
    half = nb * LANE
    himask = jnp.uint32(0xFFFF0000)

    def start_tile(tok):
        def body(r, c):
            pltpu.make_async_copy(hp_ref.at[pl.ds(tok[0, 0, r] * nb, nb)],
                                  stage.at[pl.ds(r * nb, nb)], sem).start(priority=1)
            return c

        lax.fori_loop(0, rt, body, 0)

    def put(col, val):
        x_scr[col // kd, :, col % kd:col % kd + LANE] = val.astype(BF16)

    @pl.when(t < n_used)
    def _():
        @pl.when(k == 0)
        def _():
            @pl.when(t == 0)
            def _():
                start_tile(tok_ref)

            pltpu.make_async_copy(hp_ref.at[pl.ds(0, rt * nb)], stage, sem).wait()
            for j in range(nb):
                w = stage[pl.ds(j, rt, stride=nb), :]
                put(j * LANE, pltpu.bitcast(w << 16, F32))
                put(half + j * LANE, pltpu.bitcast(w & himask, F32))

            @pl.when(t + 1 < n_used)
            def _():
                start_tile(tok_next_ref)

        x = x_scr[k]
        g = jnp.dot(x, wg_ref[0].astype(BF16), preferred_element_type=F32)
        u = jnp.dot(x, wu_ref[0].astype(BF16), preferred_element_type=F32)

        @pl.when(k == 0)
        def _():
            g_acc[...] = g
            u_acc[...] = u

        @pl.when(k > 0)
        def _():
            g_acc[...] += g
            u_acc[...] += u

        @pl.when(k == pl.num_programs(1) - 1)
        def _():
            gg = g_acc[...]
            h_ref[...] = (gg * jax.nn.sigmoid(gg) * u_acc[...]).astype(BF16)


def _moe_down_kernel(nused_ref, te_ref, tblk_ref, h_ref, wd_ref, y_ref):
    @pl.when(pl.program_id(0) < nused_ref[0])
    def _():
        y = jnp.dot(h_ref[...], wd_ref[0].astype(BF16), preferred_element_type=F32)
        y_ref[...] = _pack_bf16_pairs(y.astype(BF16).astype(F32))


def _moe_experts(n_used, tile_e, tile_blk, slot_tok, hp, w_gate, w_up, w_down, n, n_tiles, rt, kd):
    n_exp, d, de = w_gate.shape
    nb = hp.shape[0] // n
    assert 2 * nb * LANE == d and kd % LANE == 0 and hp.shape[0] >= rt * nb
    p_rows = n_tiles * rt
    nk = d // kd
    tok3 = slot_tok.reshape(n_tiles, 1, rt)

    def inner(t, k, nused, last):
        return jnp.where(t < nused[0], k, last)

    h_mid = pl.pallas_call(
        functools.partial(_moe_up_kernel, rt=rt, nb=nb, kd=kd),
        grid_spec=pltpu.PrefetchScalarGridSpec(
            num_scalar_prefetch=3,
            grid=(n_tiles, nk),
            in_specs=[
                pl.BlockSpec((1, 1, rt), lambda t, k, nu, te, tb: (t, 0, 0), memory_space=pltpu.SMEM),
                pl.BlockSpec((1, 1, rt), lambda t, k, nu, te, tb: (jnp.minimum(t + 1, n_tiles - 1), 0, 0),
                             memory_space=pltpu.SMEM),
                pl.BlockSpec(memory_space=pl.ANY),
                pl.BlockSpec((1, kd, de), lambda t, k, nu, te, tb: (te[t], inner(t, k, nu, nk - 1), 0)),
                pl.BlockSpec((1, kd, de), lambda t, k, nu, te, tb: (te[t], inner(t, k, nu, nk - 1), 0)),
            ],
            out_specs=pl.BlockSpec((rt, de), lambda t, k, nu, te, tb: (tb[t], 0)),
            scratch_shapes=[pltpu.VMEM((nk, rt, kd), BF16), pltpu.VMEM((rt * nb, LANE), U32),
                            pltpu.VMEM((rt, de), F32), pltpu.VMEM((rt, de), F32), pltpu.SemaphoreType.DMA(())],
        ),
        out_shape=jax.ShapeDtypeStruct((p_rows, de), BF16),
        compiler_params=_params(("arbitrary", "arbitrary"), 2 * _nbytes((kd, de), F32), _nbytes((rt, de), F32),
                                _nbytes((rt * nb, LANE), U32) // 2, _nbytes((rt, d), BF16) // 2),
        name="moe_gate_up",
    )(n_used, tile_e, tile_blk, tok3, tok3, hp, w_gate, w_up)

    return pl.pallas_call(
        _moe_down_kernel,
        grid_spec=pltpu.PrefetchScalarGridSpec(
            num_scalar_prefetch=3,
            grid=(n_tiles,),
            in_specs=[
                pl.BlockSpec((rt, de), lambda t, nu, te, tb: (tb[t], 0)),
                pl.BlockSpec((1, de, d), lambda t, nu, te, tb: (te[t], 0, 0)),
            ],
            out_specs=pl.BlockSpec((rt, d // 2), lambda t, nu, te, tb: (tb[t], 0)),
        ),
        out_shape=jax.ShapeDtypeStruct((p_rows, d // 2), U32),
        compiler_params=_params(("arbitrary",), _nbytes((rt, de), BF16), _nbytes((de, d), F32),
                                _nbytes((rt, d // 2), U32), _nbytes((rt, d), F32) // 2),
        name="moe_down",
    )(n_used, tile_e, tile_blk, h_mid, w_down)


def _moe_combine_kernel(pos_ref, pos_next_ref, x_ref, g_ref, y_ref, o_ref, buf0, buf1, sem0, sem1, *, tm, top_k):
    i = pl.program_id(0)
    bufs, sems = (buf0, buf1), (sem0, sem1)

    def start_step(pos, slot):
        def body(r, c):
            for k in range(top_k):
                pltpu.make_async_copy(y_ref.at[pl.ds(pos[0, 0, top_k * r + k], 1)],
                                      bufs[slot].at[pl.ds(k * tm + r, 1)], sems[slot]).start(priority=k % 2)
            return c

        lax.fori_loop(0, tm, body, 0, unroll=8)

    for slot in range(2):
        @pl.when(i % 2 == slot)
        def _(slot=slot):
            if slot == 0:
                @pl.when(i == 0)
                def _():
                    start_step(pos_ref, 0)

            @pl.when(i + 1 < pl.num_programs(0))
            def _():
                start_step(pos_next_ref, 1 - slot)

            pltpu.make_async_copy(y_ref.at[pl.ds(0, top_k * tm)], bufs[slot], sems[slot]).wait()
            half = x_ref.shape[1] // 2
            lo = x_ref[:, :half]
            hi = x_ref[:, half:]
            for k in range(top_k):
                w = bufs[slot][k * tm:(k + 1) * tm, :]
                g = g_ref[:, k:k + 1]
                lo = lo + g * pltpu.bitcast(w << 16, F32)
                hi = hi + g * pltpu.bitcast(w & jnp.uint32(0xFFFF0000), F32)
            o_ref[:, :half] = lo
            o_ref[:, half:] = hi


def _moe_combine(dest, x2, gate, y, tm, top_k):
    n, d = x2.shape
    steps = n // tm
    assert y.shape[0] >= top_k * tm
    pos3 = dest.reshape(steps, 1, tm * top_k)
    stage = pltpu.VMEM((top_k * tm, d // 2), U32)
    return pl.pallas_call(
        functools.partial(_moe_combine_kernel, tm=tm, top_k=top_k),
        grid=(steps,),
        in_specs=[
            pl.BlockSpec((1, 1, tm * top_k), lambda i: (i, 0, 0), memory_space=pltpu.SMEM),
            pl.BlockSpec((1, 1, tm * top_k), lambda i: (jnp.minimum(i + 1, steps - 1), 0, 0),
                         memory_space=pltpu.SMEM),
            pl.BlockSpec((tm, d), lambda i: (i, 0)),
            pl.BlockSpec((tm, LANE), lambda i: (i, 0)),
            pl.BlockSpec(memory_space=pl.ANY),
        ],
        out_specs=pl.BlockSpec((tm, d), lambda i: (i, 0)),
        out_shape=jax.ShapeDtypeStruct((n, d), F32),
        scratch_shapes=[stage, stage, pltpu.SemaphoreType.DMA(()), pltpu.SemaphoreType.DMA(())],
        compiler_params=_params(("arbitrary",), 2 * _nbytes((tm, d), F32), _nbytes((top_k * tm, d // 2), U32)),
        name="moe_combine",
    )(pos3, pos3, x2, gate, y)


def _moe_schedule(eid, n_exp, rt, n_tiles):
    n, top_k = eid.shape
    a = n * top_k
    eflat = eid.reshape(a)
    onehot = (eflat[:, None] == jnp.arange(n_exp, dtype=I32)[None, :]).astype(I32)
    csum = jnp.cumsum(onehot, axis=0)
    rank = jnp.sum(csum * onehot, axis=1) - 1
    counts = csum[-1]
    nt = (counts + rt - 1) // rt
    tcum = jnp.cumsum(nt)
    tstart = tcum - nt
    n_used = tcum[-1]
    dest = tstart[eflat] * rt + rank
    tid = jnp.arange(n_tiles, dtype=I32)
    tile_blk = jnp.minimum(tid, n_used - 1)
    tile_e = jnp.minimum(jnp.searchsorted(tcum, tile_blk, side="right"), n_exp - 1).astype(I32)
    tok = jnp.arange(a, dtype=I32) // top_k
    slot_tok = jnp.zeros((n_tiles * rt,), I32).at[dest].set(tok, unique_indices=True, mode="promise_in_bounds")
    return n_used.reshape(1).astype(I32), tile_e, tile_blk.astype(I32), slot_tok, dest.astype(I32)


def _pick(n, pref):
    t = min(n, pref)
    assert n % t == 0
    return t


def kernel(x, mem, g_mix, w_in, g_q_dil, g_k_dil, g_out_dil, g_out_sb, w_out, g_cross, g_mem, w_q_cross, w_kv_cross, g_q_cross, g_k_cross, w_o_cross, g_ffn, w_group, b_group, w_router, b_router, w_exp_gate, w_exp_up, w_exp_down):
    batch, seq, d = x.shape
    n = batch * seq
    n_heads_mix = d // HEAD_DIM
    n_dil = n_heads_mix // 2
    n_sb = n_heads_mix - n_dil
    w_dil, w_sb = n_dil * HEAD_DIM, n_sb * HEAD_DIM
    n_exp = w_exp_gate.shape[1]
    top_k = 2
    scale = 1.0 / math.sqrt(HEAD_DIM)
    slopes = jnp.exp2(-ALIBI_MAX_BIAS * jnp.arange(1, n_dil + 1, dtype=F32) / n_dil)

    tm = _pick(n, 512)
    rt = -(-(9 * n * top_k) // (8 * n_exp * MOE_ROW_ALIGN)) * MOE_ROW_ALIGN
    n_tiles = -(-n * top_k // rt) + n_exp
    kd = max(d // 2, LANE)

    xf = x.reshape(n, d)
    for l in range(g_mix.shape[0]):
        h = _rmsnorm_cast(xf, g_mix[l], tm)
        w_in_b = w_in[l].astype(BF16)
        gain_dil = jnp.concatenate([jnp.tile(g_q_dil[l] * scale, n_dil), jnp.tile(g_k_dil[l], n_dil),
                                    jnp.ones((w_dil,), F32)])
        gain_sb = jnp.concatenate([jnp.full((w_sb,), scale * math.log2(math.e), F32), jnp.ones((2 * w_sb,), F32)])
        tn = _pick(w_dil, 1024)
        tmp = _pick(n, 1024)
        qkv_dil = _proj(h, w_in_b, gain_dil, 0, 3 * w_dil, 2 * w_dil, F32, tmp, tn, "in_proj_dil")
        qkv_sb = _proj(h, w_in_b, gain_sb, 3 * w_dil, 3 * w_sb, 0, BF16, tmp, tn, "in_proj_sb")
        o_dil = _dilated(qkv_dil, slopes, batch, seq, n_dil, _pick(seq, 2048))
        o_sb = _stick_breaking(qkv_sb, batch, seq, n_sb, _pick(seq, 2048), _pick(seq, 256))
        x1 = _outproj(o_dil, o_sb, g_out_dil[l], g_out_sb[l], w_out[l].astype(BF16), xf, _pick(n, 1024), _pick(d, 512))

        kc, vc = _mem_kv(mem, g_mem[l], w_kv_cross[l].astype(BF16), g_k_cross[l])
        wr = jnp.concatenate([w_router[l], w_group[l],
                              jnp.zeros((d, LANE - n_exp - N_GROUPS), F32)], axis=1)
        wr_hi = wr.astype(BF16)
        wr_lo = (wr - wr_hi.astype(F32)).astype(BF16)
        br = jnp.concatenate([b_router[l].reshape(-1), b_group[l],
                              jnp.zeros((LANE - n_exp - N_GROUPS,), F32)]).reshape(1, LANE)
        x2, hp, eid, gate = _cross_router(x1, g_cross[l], w_q_cross[l].astype(BF16), g_q_cross[l] * scale, kc, vc,
                                          w_o_cross[l].astype(BF16), g_ffn[l], wr_hi, wr_lo, br, seq,
                                          _pick(seq, 256))

        n_used, tile_e, tile_blk, slot_tok, dest = _moe_schedule(eid[:, :top_k], n_exp, rt, n_tiles)
        y = _moe_experts(n_used, tile_e, tile_blk, slot_tok, hp, w_exp_gate[l], w_exp_up[l], w_exp_down[l],
                         n, n_tiles, rt, kd)
        xf = _moe_combine(dest, x2, gate, y, _pick(n, 256), top_k)
    return xf.reshape(batch, seq, d)
```

```python
import functools
import math

import jax
import jax.numpy as jnp
from jax import lax
from jax.experimental import pallas as pl
from jax.experimental.pallas import tpu as pltpu

F32 = jnp.float32
BF16 = jnp.bfloat16
I32 = jnp.int32
U32 = jnp.uint32

LANE = 128
HEAD_DIM = 128
EPS = 1e-6
DIL_PAIRS = ((128, 1), (512, 4), (2048, 16))
DIL_STEPS = 128
DIL_UNROLL = 16
ALIBI_MAX_BIAS = 8.0
N_GROUPS = 8
EXPERTS_PER_GROUP = 8
MOE_ROW_ALIGN = 64
NEG_BIG = -1e30
VMEM_LIMIT_CAP = 60000 * 1024
VMEM_INTERNAL = 12 * 1024 * 1024
LANE_NONE = 1e9


def _vmem_limit(*block_bytes):
    return int(min(VMEM_LIMIT_CAP, 2 * sum(block_bytes) + VMEM_INTERNAL))


def _nbytes(shape, dtype):
    return math.prod(shape) * jnp.dtype(dtype).itemsize


def _params(sem, *block_bytes):
    return pltpu.CompilerParams(dimension_semantics=sem, vmem_limit_bytes=_vmem_limit(*block_bytes))


def _rms(x, g):
    return x * lax.rsqrt(jnp.mean(x * x, axis=-1, keepdims=True) + EPS) * g


def _rmsnorm_cast_kernel(x_ref, g_ref, o_ref):
    o_ref[...] = _rms(x_ref[...], g_ref[...]).astype(o_ref.dtype)


def _rmsnorm_cast(x, g, tm):
    n, d = x.shape
    return pl.pallas_call(
        _rmsnorm_cast_kernel,
        grid=(n // tm,),
        in_specs=[pl.BlockSpec((tm, d), lambda i: (i, 0)), pl.BlockSpec((1, d), lambda i: (0, 0))],
        out_specs=pl.BlockSpec((tm, d), lambda i: (i, 0)),
        out_shape=jax.ShapeDtypeStruct((n, d), BF16),
        compiler_params=_params(("arbitrary",), _nbytes((tm, d), F32), _nbytes((tm, d), BF16)),
        name="rmsnorm_cast",
    )(x, g.reshape(1, d))


def _proj_kernel(a_ref, w_ref, g_ref, o_ref, *, n_norm_tiles):
    j = pl.program_id(1)
    acc = jnp.dot(a_ref[...], w_ref[...], preferred_element_type=F32)
    tn = acc.shape[1]

    @pl.when(j < n_norm_tiles)
    def _():
        for h in range(tn // HEAD_DIM):
            sl = slice(h * HEAD_DIM, (h + 1) * HEAD_DIM)
            o_ref[:, sl] = _rms(acc[:, sl], g_ref[:, sl]).astype(o_ref.dtype)

    @pl.when(j >= n_norm_tiles)
    def _():
        o_ref[...] = (acc * g_ref[...]).astype(o_ref.dtype)


def _proj(a, w, gain, col0, ncols, norm_cols, out_dtype, tm, tn, name):
    n, k = a.shape
    assert col0 % tn == 0 and ncols % tn == 0 and norm_cols % tn == 0 and n % tm == 0
    jb = col0 // tn
    return pl.pallas_call(
        functools.partial(_proj_kernel, n_norm_tiles=norm_cols // tn),
        grid=(n // tm, ncols // tn),
        in_specs=[
            pl.BlockSpec((tm, k), lambda i, j: (i, 0)),
            pl.BlockSpec((k, tn), lambda i, j: (0, j + jb)),
            pl.BlockSpec((1, tn), lambda i, j: (0, j)),
        ],
        out_specs=pl.BlockSpec((tm, tn), lambda i, j: (i, j)),
        out_shape=jax.ShapeDtypeStruct((n, ncols), out_dtype),
        compiler_params=_params(("arbitrary", "arbitrary"), _nbytes((tm, k), BF16), _nbytes((k, tn), BF16),
                                _nbytes((tm, tn), F32)),
        name=name,
    )(a, w, gain.reshape(1, ncols))


def _dil_kernel(slope_ref, q_ref, k_ref, v_ref, o_ref, ob0, ob1, ob2, ls0, ls1, ls2, *, tq):
    h = pl.program_id(1)
    t0 = pl.program_id(2) * tq
    slope = slope_ref[h]
    J = DIL_STEPS
    qi = lax.broadcasted_iota(I32, (J, 2 * J), 0)
    ki = lax.broadcasted_iota(I32, (J, 2 * J), 1)
    dist = J + qi - ki
    valid = (dist >= 0) & (dist <= J)
    distf = dist.astype(F32)
    in_cur = ki >= J
    ones_v = jnp.ones((2 * J, HEAD_DIM), BF16)

    for (window, d), ob, ls in zip(DIL_PAIRS, (ob0, ob1, ob2), (ls0, ls1, ls2)):
        assert window // d == J and tq % (d * J) == 0
        bias = jnp.where(valid, (-slope * d) * distf, NEG_BIG)
        shift = d.bit_length() - 1

        def ld(ref, start, d=d):
            if d == 1:
                return ref[pl.ds(start, J), :]
            return ref[pl.ds(start, J, stride=d), :]

        def unit(u, carry, d=d, shift=shift, bias=bias, ob=ob, ls=ls, ld=ld):
            off = (u & (d - 1)) + (u >> shift) * (d * J)
            cur0 = t0 + off
            prev0 = cur0 - d * J
            has_prev = prev0 >= 0
            prev0 = jnp.maximum(prev0, 0)
            qs = ld(q_ref, off).astype(BF16)
            kcat = jnp.concatenate([ld(k_ref, prev0), ld(k_ref, cur0)], axis=0).astype(BF16)
            vcat = jnp.concatenate([ld(v_ref, prev0), ld(v_ref, cur0)], axis=0).astype(BF16)
            s = lax.dot_general(qs, kcat, (((1,), (1,)), ((), ())), preferred_element_type=F32) + bias
            s = jnp.where(jnp.logical_or(in_cur, has_prev), s, NEG_BIG)
            m = jnp.max(s, axis=1, keepdims=True)
            p = jnp.exp(s - m)
            ov = jnp.dot(p.astype(BF16), jnp.concatenate([vcat, ones_v], axis=1), preferred_element_type=F32)
            l = ov[:, HEAD_DIM:]
            o = ov[:, :HEAD_DIM] / l
            lse = m + jnp.log(l)
            if d == 1:
                ob[pl.ds(off, J), :] = o
                ls[pl.ds(off, J), :] = lse
            else:
                ob[pl.ds(off, J, stride=d), :] = o
                ls[pl.ds(off, J, stride=d), :] = lse
            return carry

        lax.fori_loop(0, tq // J, unit, 0, unroll=DIL_UNROLL)

    l0, l1, l2 = ls0[...], ls1[...], ls2[...]
    mx = jnp.maximum(jnp.maximum(l0, l1), l2)
    w0, w1, w2 = jnp.exp(l0 - mx), jnp.exp(l1 - mx), jnp.exp(l2 - mx)
    o_ref[...] = ((w0 * ob0[...] + w1 * ob1[...] + w2 * ob2[...]) / (w0 + w1 + w2)).astype(o_ref.dtype)


def _dilated(qkv, slopes, batch, seq, n_heads, tq):
    assert seq % tq == 0
    n = batch * seq
    nq = seq // tq
    blk = (tq, HEAD_DIM)
    full = (seq, HEAD_DIM)
    scr = [pltpu.VMEM(blk, F32)] * 6
    return pl.pallas_call(
        functools.partial(_dil_kernel, tq=tq),
        grid=(batch, n_heads, nq),
        in_specs=[
            pl.BlockSpec(memory_space=pltpu.SMEM),
            pl.BlockSpec(blk, lambda b, h, c: (b * nq + c, h)),
            pl.BlockSpec(full, lambda b, h, c: (b, n_heads + h)),
            pl.BlockSpec(full, lambda b, h, c: (b, 2 * n_heads + h)),
        ],
        out_specs=pl.BlockSpec(blk, lambda b, h, c: (b * nq + c, h)),
        out_shape=jax.ShapeDtypeStruct((n, n_heads * HEAD_DIM), BF16),
        scratch_shapes=scr,
        compiler_params=_params(("arbitrary",) * 3, 2 * _nbytes(full, F32), 5 * _nbytes(blk, F32)),
        name="dilated_attn",
    )(slopes, qkv, qkv, qkv)


def _sb_kernel(q_ref, k_ref, v_ref, o_ref, *, tq, sub):
    n_chain = tq // sub
    assert sub & (sub - 1) == 0
    kb0 = pl.program_id(2) * n_chain
    row = lax.broadcasted_iota(I32, (sub, sub), 0)
    col = lax.broadcasted_iota(I32, (sub, sub), 1)
    after = (row > col).astype(BF16)
    after2 = jnp.concatenate([after, after], axis=0)
    qrow = lax.broadcasted_iota(I32, (tq, sub), 0)
    causal = lax.broadcasted_iota(I32, (tq, sub), 1) < (qrow & (sub - 1))
    chain_of_row = lax.broadcasted_iota(I32, (tq, 1), 0) >> (sub.bit_length() - 1)
    sign = jnp.uint32(0x80000000)

    def sweep(kbs, drop, acc, diag, first_valid):
        starts = [pl.multiple_of(kb * sub, sub) for kb in kbs]
        z2 = jnp.concatenate(
            [lax.dot_general(q_ref[c * sub:(c + 1) * sub, :], k_ref[pl.ds(starts[c], sub), :],
                             (((1,), (1,)), ((), ())), preferred_element_type=F32) for c in range(n_chain)], axis=0)
        neg_abs = pltpu.bitcast(pltpu.bitcast(z2, U32) | sign, F32)
        sp2 = jnp.maximum(z2, 0.0) + jnp.log2(1.0 + jnp.exp2(neg_abs))
        log_beta2 = z2 - sp2
        if diag:
            sp2 = jnp.where(causal, sp2, 0.0)
        hi = sp2.astype(BF16)
        lo = (sp2 - hi.astype(F32)).astype(BF16)
        later = jnp.dot(jnp.concatenate([hi, lo], axis=1), after2, preferred_element_type=F32)
        a = jnp.exp2(log_beta2 - later)
        if diag:
            a = jnp.where(causal, a, 0.0)
        a = a.astype(BF16)
        pv = jnp.concatenate(
            [jnp.dot(a[c * sub:(c + 1) * sub, :], v_ref[pl.ds(starts[c], sub), :], preferred_element_type=F32)
             for c in range(n_chain)], axis=0)
        scale = jnp.exp2(-drop)
        new_drop = drop + later[:, 0:1] + sp2[:, 0:1]
        if first_valid is not None:
            valid = chain_of_row >= first_valid
            scale = jnp.where(valid, scale, 0.0)
            new_drop = jnp.where(valid, new_drop, drop)
        return new_drop, acc + scale * pv

    def alive(drop, first_valid):
        live = jnp.where(chain_of_row >= first_valid, jnp.exp2(-drop), 0.0)
        return (jnp.max(live) > 0.0).astype(I32)

    drop, acc = sweep([kb0 + c for c in range(n_chain)], jnp.zeros((tq, 1), F32),
                      jnp.zeros((tq, HEAD_DIM), F32), True, None)

    def cond(st):
        return jnp.logical_and(st[0] <= kb0 + n_chain - 1, st[1] > 0)

    def body(st):
        g, _, drop, acc = st
        drop, acc = sweep([jnp.maximum(kb0 + c - g, 0) for c in range(n_chain)], drop, acc, False, g - kb0)
        return g + 1, alive(drop, g + 1 - kb0), drop, acc

    _, _, _, acc = lax.while_loop(cond, body, (jnp.int32(1), alive(drop, 1 - kb0), drop, acc))
    o_ref[...] = acc.astype(o_ref.dtype)


def _stick_breaking(qkv, batch, seq, n_heads, tq, sub):
    assert seq % tq == 0 and tq % sub == 0
    n = batch * seq
    nq = seq // tq
    blk = (tq, HEAD_DIM)
    full = (seq, HEAD_DIM)
    return pl.pallas_call(
        functools.partial(_sb_kernel, tq=tq, sub=sub),
        grid=(batch, n_heads, nq),
        in_specs=[
            pl.BlockSpec(blk, lambda b, h, i: (b * nq + i, h)),
            pl.BlockSpec(full, lambda b, h, i: (b, n_heads + h)),
            pl.BlockSpec(full, lambda b, h, i: (b, 2 * n_heads + h)),
        ],
        out_specs=pl.BlockSpec(blk, lambda b, h, i: (b * nq + i, h)),
        out_shape=jax.ShapeDtypeStruct((n, n_heads * HEAD_DIM), BF16),
        compiler_params=_params(("arbitrary",) * 3, 2 * _nbytes(full, BF16), 2 * _nbytes(blk, F32),
                                4 * _nbytes((tq, sub), F32)),
        name="stick_breaking",
    )(qkv, qkv, qkv)


def _outproj_kernel(od_ref, os_ref, gd_ref, gs_ref, w_ref, x_ref, o_ref, m_scr):
    wd = od_ref.shape[1]

    @pl.when(pl.program_id(1) == 0)
    def _():
        m_scr[:, :wd] = _rms(od_ref[...].astype(F32), gd_ref[...]).astype(BF16)
        m_scr[:, wd:] = _rms(os_ref[...].astype(F32), gs_ref[...]).astype(BF16)

    o_ref[...] = x_ref[...] + jnp.dot(m_scr[...], w_ref[...], preferred_element_type=F32)


def _outproj(o_dil, o_sb, g_dil, g_sb, w, x, tm, tn):
    n, wd = o_dil.shape
    ws = o_sb.shape[1]
    d = x.shape[1]
    return pl.pallas_call(
        _outproj_kernel,
        grid=(n // tm, d // tn),
        in_specs=[
            pl.BlockSpec((tm, wd), lambda i, j: (i, 0)),
            pl.BlockSpec((tm, ws), lambda i, j: (i, 0)),
            pl.BlockSpec((1, wd), lambda i, j: (0, 0)),
            pl.BlockSpec((1, ws), lambda i, j: (0, 0)),
            pl.BlockSpec((wd + ws, tn), lambda i, j: (0, j)),
            pl.BlockSpec((tm, tn), lambda i, j: (i, j)),
        ],
        out_specs=pl.BlockSpec((tm, tn), lambda i, j: (i, j)),
        out_shape=jax.ShapeDtypeStruct((n, d), F32),
        scratch_shapes=[pltpu.VMEM((tm, wd + ws), BF16)],
        compiler_params=_params(("arbitrary", "arbitrary"), _nbytes((tm, wd + ws), BF16),
                                _nbytes((wd + ws, tn), BF16), 2 * _nbytes((tm, tn), F32),
                                _nbytes((tm, wd + ws), BF16) // 2),
        name="out_proj",
    )(o_dil, o_sb, g_dil.reshape(1, wd), g_sb.reshape(1, ws), w, x)


def _mem_kv_kernel(m_ref, gm_ref, w_ref, gk_ref, k_ref, v_ref):
    hm = _rms(m_ref[0], gm_ref[...]).astype(BF16)
    kv = jnp.dot(hm, w_ref[...], preferred_element_type=F32)
    wc = k_ref.shape[2]
    for h in range(wc // HEAD_DIM):
        sl = slice(h * HEAD_DIM, (h + 1) * HEAD_DIM)
        k_ref[0, :, sl] = _rms(kv[:, sl], gk_ref[...]).astype(BF16)
    v_ref[0] = kv[:, wc:].astype(BF16)


def _mem_kv(mem, g_mem, w_kv, g_k):
    b, m, d = mem.shape
    wc = w_kv.shape[1] // 2
    out = jax.ShapeDtypeStruct((b, m, wc), BF16)
    return pl.pallas_call(
        _mem_kv_kernel,
        grid=(b,),
        in_specs=[
            pl.BlockSpec((1, m, d), lambda i: (i, 0, 0)),
            pl.BlockSpec((1, d), lambda i: (0, 0)),
            pl.BlockSpec((d, 2 * wc), lambda i: (0, 0)),
            pl.BlockSpec((1, HEAD_DIM), lambda i: (0, 0)),
        ],
        out_specs=[pl.BlockSpec((1, m, wc), lambda i: (i, 0, 0))] * 2,
        out_shape=[out, out],
        compiler_params=_params(("arbitrary",), _nbytes((m, d), F32), _nbytes((d, 2 * wc), BF16)),
        name="mem_kv",
    )(mem, g_mem.reshape(1, d), w_kv, g_k.reshape(1, HEAD_DIM))


def _pack_bf16_pairs(xr):
    half = xr.shape[1] // 2
    lo = pltpu.bitcast(xr[:, :half], U32)
    hi = pltpu.bitcast(xr[:, half:], U32)
    return hi | (lo >> 16)


def _cross_router_kernel(x1_ref, gc_ref, wq_ref, gq_ref, kc_ref, vc_ref, wo_ref, gf_ref, wrh_ref, wrl_ref, br_ref,
                         x2_ref, hp_ref, eid_ref, gate_ref):
    x1 = x1_ref[...]
    h = _rms(x1, gc_ref[...]).astype(BF16)
    q = jnp.dot(h, wq_ref[...], preferred_element_type=F32)
    heads = []
    for hd in range(q.shape[1] // HEAD_DIM):
        sl = slice(hd * HEAD_DIM, (hd + 1) * HEAD_DIM)
        qh = _rms(q[:, sl], gq_ref[...]).astype(BF16)
        s = lax.dot_general(qh, kc_ref[0, :, sl], (((1,), (1,)), ((), ())), preferred_element_type=F32)
        p = jnp.exp(s - jnp.max(s, axis=1, keepdims=True))
        l = jnp.sum(p, axis=1, keepdims=True)
        heads.append(jnp.dot(p.astype(BF16), vc_ref[0, :, sl], preferred_element_type=F32) / l)
    o = jnp.concatenate(heads, axis=1).astype(BF16)
    x2 = x1 + jnp.dot(o, wo_ref[...], preferred_element_type=F32)
    x2_ref[...] = x2

    hf = _rms(x2, gf_ref[...])
    hf_hi = hf.astype(BF16)
    hf_hi32 = hf_hi.astype(F32)
    hf_lo = (hf - hf_hi32).astype(BF16)
    packed = _pack_bf16_pairs(hf_hi32)
    tm = packed.shape[0]
    nb = packed.shape[1] // LANE
    for j in range(nb):
        hp_ref[pl.ds(j, tm, stride=nb), :] = packed[:, j * LANE:(j + 1) * LANE]

    n_exp = N_GROUPS * EXPERTS_PER_GROUP
    lg = (jnp.dot(hf_hi, wrh_ref[...], preferred_element_type=F32)
          + jnp.dot(hf_hi, wrl_ref[...], preferred_element_type=F32)
          + jnp.dot(hf_lo, wrh_ref[...], preferred_element_type=F32)) + br_ref[...]
    lane = lax.broadcasted_iota(I32, lg.shape, 1)
    lanef = lane.astype(F32)
    ninf = -jnp.inf
    is_g = (lane >= n_exp) & (lane < n_exp + N_GROUPS)
    gl = jnp.where(is_g, lg, ninf)
    gmax = jnp.max(gl, axis=1, keepdims=True)
    gidx = jnp.min(jnp.where(gl == gmax, lanef, LANE_NONE), axis=1, keepdims=True) - n_exp
    g_gate = 1.0 / jnp.sum(jnp.exp(gl - gmax), axis=1, keepdims=True)
    in_grp = (lane < n_exp) & ((lane // EXPERTS_PER_GROUP).astype(F32) == gidx)
    el = jnp.where(in_grp, lg, ninf)
    m1 = jnp.max(el, axis=1, keepdims=True)
    i1 = jnp.min(jnp.where(el == m1, lanef, LANE_NONE), axis=1, keepdims=True)
    el2 = jnp.where(lanef == i1, ninf, el)
    m2 = jnp.max(el2, axis=1, keepdims=True)
    i2 = jnp.min(jnp.where(el2 == m2, lanef, LANE_NONE), axis=1, keepdims=True)
    p2 = jnp.exp(m2 - m1)
    den = 1.0 + p2
    eid_ref[...] = jnp.where(lane == 0, i1, jnp.where(lane == 1, i2, 0.0)).astype(I32)
    gate_ref[...] = jnp.where(lane == 0, g_gate / den, jnp.where(lane == 1, g_gate * p2 / den, 0.0))


def _cross_router(x1, g_cross, wq, gq, kc, vc, wo, g_ffn, wr_hi, wr_lo, br, seq, tm):
    n, d = x1.shape
    wc = wq.shape[1]
    m = kc.shape[1]
    tiles_per_batch = seq // tm
    const = lambda i: (0, 0)
    return pl.pallas_call(
        _cross_router_kernel,
        grid=(n // tm,),
        in_specs=[
            pl.BlockSpec((tm, d), lambda i: (i, 0)),
            pl.BlockSpec((1, d), const),
            pl.BlockSpec((d, wc), const),
            pl.BlockSpec((1, HEAD_DIM), const),
            pl.BlockSpec((1, m, wc), lambda i: (i // tiles_per_batch, 0, 0)),
            pl.BlockSpec((1, m, wc), lambda i: (i // tiles_per_batch, 0, 0)),
            pl.BlockSpec((wc, d), const),
            pl.BlockSpec((1, d), const),
            pl.BlockSpec((d, LANE), const),
            pl.BlockSpec((d, LANE), const),
            pl.BlockSpec((1, LANE), const),
        ],
        out_specs=[
            pl.BlockSpec((tm, d), lambda i: (i, 0)),
            pl.BlockSpec((tm * (d // 2 // LANE), LANE), lambda i: (i, 0)),
            pl.BlockSpec((tm, LANE), lambda i: (i, 0)),
            pl.BlockSpec((tm, LANE), lambda i: (i, 0)),
        ],
        out_shape=[
            jax.ShapeDtypeStruct((n, d), F32),
            jax.ShapeDtypeStruct((n * (d // 2 // LANE), LANE), U32),
            jax.ShapeDtypeStruct((n, LANE), I32),
            jax.ShapeDtypeStruct((n, LANE), F32),
        ],
        compiler_params=_params(("arbitrary",), 2 * _nbytes((tm, d), F32), _nbytes((tm, d // 2), U32),
                                2 * _nbytes((d, wc), BF16), 2 * _nbytes((d, LANE), BF16)),
        name="cross_router",
    )(x1, g_cross.reshape(1, d), wq, gq.reshape(1, HEAD_DIM), kc, vc, wo, g_ffn.reshape(1, d), wr_hi, wr_lo, br)


def _moe_up_kernel(nused_ref, te_ref, tblk_ref, tok_ref, tok_next_ref, hp_ref, wg_ref, wu_ref, h_ref,
                   x_scr, stage, g_acc, u_acc, sem, *, rt, nb, kd):
    t = pl.program_id(0)
    k = pl.program_id(1)
    n_used = nused_ref[0]
    half = nb * LANE
    himask = jnp.uint32(0xFFFF0000)

    def start_tile(tok):
        def body(i, c):
            for pri in range(2):
                r = 2 * i + pri
                pltpu.make_async_copy(hp_ref.at[pl.ds(tok[0, 0, r] * nb, nb)],
                                      stage.at[pl.ds(r * nb, nb)], sem).start(priority=pri)
            return c

        lax.fori_loop(0, rt // 2, body, 0)

    def put(col, val):
        x_scr[col // kd, :, col % kd:col % kd + LANE] = val.astype(BF16)

    @pl.when(t < n_used)
    def _():
        @pl.when(k == 0)
        def _():
            @pl.when(t == 0)
            def _():
                start_tile(tok_ref)

            pltpu.make_async_copy(hp_ref.at[pl.ds(0, rt * nb)], stage, sem).wait()
            for j in range(nb):
                w = stage[pl.ds(j, rt, stride=nb), :]
                put(j * LANE, pltpu.bitcast(w << 16, F32))
                put(half + j * LANE, pltpu.bitcast(w & himask, F32))

            @pl.when(t + 1 < n_used)
            def _():
                start_tile(tok_next_ref)

        x = x_scr[k]
        g = jnp.dot(x, wg_ref[0].astype(BF16), preferred_element_type=F32)
        u = jnp.dot(x, wu_ref[0].astype(BF16), preferred_element_type=F32)

        @pl.when(k == 0)
        def _():
            g_acc[...] = g
            u_acc[...] = u

        @pl.when(k > 0)
        def _():
            g_acc[...] += g
            u_acc[...] += u

        @pl.when(k == pl.num_programs(1) - 1)
        def _():
            gg = g_acc[...]
            h_ref[...] = (gg * jax.nn.sigmoid(gg) * u_acc[...]).astype(BF16)


def _moe_down_kernel(nused_ref, te_ref, tblk_ref, h_ref, wd_ref, y_ref):
    @pl.when(pl.program_id(0) < nused_ref[0])
    def _():
        y = jnp.dot(h_ref[...], wd_ref[0].astype(BF16), preferred_element_type=F32)
        y_ref[...] = _pack_bf16_pairs(y.astype(BF16).astype(F32))


def _moe_experts(n_used, tile_e, tile_blk, slot_tok, hp, w_gate, w_up, w_down, n, n_tiles, rt, kd):
    n_exp, d, de = w_gate.shape
    nb = hp.shape[0] // n
    assert 2 * nb * LANE == d and kd % LANE == 0 and hp.shape[0] >= rt * nb
    p_rows = n_tiles * rt
    nk = d // kd
    tok3 = slot_tok.reshape(n_tiles, 1, rt)

    def inner(t, k, nused, last):
        return jnp.where(t < nused[0], k, last)

    h_mid = pl.pallas_call(
        functools.partial(_moe_up_kernel, rt=rt, nb=nb, kd=kd),
        grid_spec=pltpu.PrefetchScalarGridSpec(
            num_scalar_prefetch=3,
            grid=(n_tiles, nk),
            in_specs=[
                pl.BlockSpec((1, 1, rt), lambda t, k, nu, te, tb: (t, 0, 0), memory_space=pltpu.SMEM),
                pl.BlockSpec((1, 1, rt), lambda t, k, nu, te, tb: (jnp.minimum(t + 1, n_tiles - 1), 0, 0),
                             memory_space=pltpu.SMEM),
                pl.BlockSpec(memory_space=pl.ANY),
                pl.BlockSpec((1, kd, de), lambda t, k, nu, te, tb: (te[t], inner(t, k, nu, nk - 1), 0)),
                pl.BlockSpec((1, kd, de), lambda t, k, nu, te, tb: (te[t], inner(t, k, nu, nk - 1), 0)),
            ],
            out_specs=pl.BlockSpec((rt, de), lambda t, k, nu, te, tb: (tb[t], 0)),
            scratch_shapes=[pltpu.VMEM((nk, rt, kd), BF16), pltpu.VMEM((rt * nb, LANE), U32),
                            pltpu.VMEM((rt, de), F32), pltpu.VMEM((rt, de), F32), pltpu.SemaphoreType.DMA(())],
        ),
        out_shape=jax.ShapeDtypeStruct((p_rows, de), BF16),
        compiler_params=_params(("arbitrary", "arbitrary"), 2 * _nbytes((kd, de), F32), _nbytes((rt, de), F32),
                                _nbytes((rt * nb, LANE), U32) // 2, _nbytes((rt, d), BF16) // 2),
        name="moe_gate_up",
    )(n_used, tile_e, tile_blk, tok3, tok3, hp, w_gate, w_up)

    return pl.pallas_call(
        _moe_down_kernel,
        grid_spec=pltpu.PrefetchScalarGridSpec(
            num_scalar_prefetch=3,
            grid=(n_tiles,),
            in_specs=[
                pl.BlockSpec((rt, de), lambda t, nu, te, tb: (tb[t], 0)),
                pl.BlockSpec((1, de, d), lambda t, nu, te, tb: (te[t], 0, 0)),
            ],
            out_specs=pl.BlockSpec((rt, d // 2), lambda t, nu, te, tb: (tb[t], 0)),
        ),
        out_shape=jax.ShapeDtypeStruct((p_rows, d // 2), U32),
        compiler_params=_params(("arbitrary",), _nbytes((rt, de), BF16), _nbytes((de, d), F32),
                                _nbytes((rt, d // 2), U32), _nbytes((rt, d), F32) // 2),
        name="moe_down",
    )(n_used, tile_e, tile_blk, h_mid, w_down)


def _moe_combine_kernel(pos_ref, pos_next_ref, x_ref, g_ref, y_ref, o_ref, buf0, buf1, sem0, sem1, *, tm, top_k):
    i = pl.program_id(0)
    bufs, sems = (buf0, buf1), (sem0, sem1)

    def start_step(pos, slot):
        def body(r, c):
            for k in range(top_k):
                pltpu.make_async_copy(y_ref.at[pl.ds(pos[0, 0, top_k * r + k], 1)],
                                      bufs[slot].at[pl.ds(k * tm + r, 1)], sems[slot]).start(priority=k % 2)
            return c

        lax.fori_loop(0, tm, body, 0, unroll=8)

    for slot in range(2):
        @pl.when(i % 2 == slot)
        def _(slot=slot):
            if slot == 0:
                @pl.when(i == 0)
                def _():
                    start_step(pos_ref, 0)

            @pl.when(i + 1 < pl.num_programs(0))
            def _():
                start_step(pos_next_ref, 1 - slot)

            pltpu.make_async_copy(y_ref.at[pl.ds(0, top_k * tm)], bufs[slot], sems[slot]).wait()
            half = x_ref.shape[1] // 2
            lo = x_ref[:, :half]
            hi = x_ref[:, half:]
            for k in range(top_k):
                w = bufs[slot][k * tm:(k + 1) * tm, :]
                g = g_ref[:, k:k + 1]
                lo = lo + g * pltpu.bitcast(w << 16, F32)
                hi = hi + g * pltpu.bitcast(w & jnp.uint32(0xFFFF0000), F32)
            o_ref[:, :half] = lo
            o_ref[:, half:] = hi


def _moe_combine(dest, x2, gate, y, tm, top_k):
    n, d = x2.shape
    steps = n // tm
    assert y.shape[0] >= top_k * tm
    pos3 = dest.reshape(steps, 1, tm * top_k)
    stage = pltpu.VMEM((top_k * tm, d // 2), U32)
    return pl.pallas_call(
        functools.partial(_moe_combine_kernel, tm=tm, top_k=top_k),
        grid=(steps,),
        in_specs=[
            pl.BlockSpec((1, 1, tm * top_k), lambda i: (i, 0, 0), memory_space=pltpu.SMEM),
            pl.BlockSpec((1, 1, tm * top_k), lambda i: (jnp.minimum(i + 1, steps - 1), 0, 0),
                         memory_space=pltpu.SMEM),
            pl.BlockSpec((tm, d), lambda i: (i, 0)),
            pl.BlockSpec((tm, LANE), lambda i: (i, 0)),
            pl.BlockSpec(memory_space=pl.ANY),
        ],
        out_specs=pl.BlockSpec((tm, d), lambda i: (i, 0)),
        out_shape=jax.ShapeDtypeStruct((n, d), F32),
        scratch_shapes=[stage, stage, pltpu.SemaphoreType.DMA(()), pltpu.SemaphoreType.DMA(())],
        compiler_params=_params(("arbitrary",), 2 * _nbytes((tm, d), F32), _nbytes((top_k * tm, d // 2), U32)),
        name="moe_combine",
    )(pos3, pos3, x2, gate, y)


def _moe_schedule(eid, n_exp, rt, n_tiles):
    n, top_k = eid.shape
    a = n * top_k
    eflat = eid.reshape(a)
    onehot = (eflat[:, None] == jnp.arange(n_exp, dtype=I32)[None, :]).astype(I32)
    csum = jnp.cumsum(onehot, axis=0)
    rank = jnp.sum(csum * onehot, axis=1) - 1
    counts = csum[-1]
    nt = (counts + rt - 1) // rt
    tcum = jnp.cumsum(nt)
    tstart = tcum - nt
    n_used = tcum[-1]
    dest = tstart[eflat] * rt + rank
    tid = jnp.arange(n_tiles, dtype=I32)
    tile_blk = jnp.minimum(tid, n_used - 1)
    tile_e = jnp.minimum(jnp.searchsorted(tcum, tile_blk, side="right"), n_exp - 1).astype(I32)
    tok = jnp.arange(a, dtype=I32) // top_k
    slot_tok = jnp.zeros((n_tiles * rt,), I32).at[dest].set(tok, unique_indices=True, mode="promise_in_bounds")
    return n_used.reshape(1).astype(I32), tile_e, tile_blk.astype(I32), slot_tok, dest.astype(I32)


def _pick(n, pref):
    t = min(n, pref)
    assert n % t == 0
    return t


def kernel(x, mem, g_mix, w_in, g_q_dil, g_k_dil, g_out_dil, g_out_sb, w_out, g_cross, g_mem, w_q_cross, w_kv_cross, g_q_cross, g_k_cross, w_o_cross, g_ffn, w_group, b_group, w_router, b_router, w_exp_gate, w_exp_up, w_exp_down):
    batch, seq, d = x.shape
    n = batch * seq
    n_heads_mix = d // HEAD_DIM
    n_dil = n_heads_mix // 2
    n_sb = n_heads_mix - n_dil
    w_dil, w_sb = n_dil * HEAD_DIM, n_sb * HEAD_DIM
    n_exp = w_exp_gate.shape[1]
    top_k = 2
    scale = 1.0 / math.sqrt(HEAD_DIM)
    slopes = jnp.exp2(-ALIBI_MAX_BIAS * jnp.arange(1, n_dil + 1, dtype=F32) / n_dil)

    tm = _pick(n, 512)
    rt = -(-(9 * n * top_k) // (8 * n_exp * MOE_ROW_ALIGN)) * MOE_ROW_ALIGN
    n_tiles = -(-n * top_k // rt) + n_exp
    kd = max(d // 2, LANE)

    xf = x.reshape(n, d)
    for l in range(g_mix.shape[0]):
        h = _rmsnorm_cast(xf, g_mix[l], tm)
        w_in_b = w_in[l].astype(BF16)
        gain_dil = jnp.concatenate([jnp.tile(g_q_dil[l] * scale, n_dil), jnp.tile(g_k_dil[l], n_dil),
                                    jnp.ones((w_dil,), F32)])
        gain_sb = jnp.concatenate([jnp.full((w_sb,), scale * math.log2(math.e), F32), jnp.ones((2 * w_sb,), F32)])
        tn = _pick(w_dil, 1024)
        tmp = _pick(n, 1024)
        qkv_dil = _proj(h, w_in_b, gain_dil, 0, 3 * w_dil, 2 * w_dil, F32, tmp, tn, "in_proj_dil")
        qkv_sb = _proj(h, w_in_b, gain_sb, 3 * w_dil, 3 * w_sb, 0, BF16, tmp, tn, "in_proj_sb")
        o_dil = _dilated(qkv_dil, slopes, batch, seq, n_dil, _pick(seq, 2048))
        o_sb = _stick_breaking(qkv_sb, batch, seq, n_sb, _pick(seq, 2048), _pick(seq, 256))
        x1 = _outproj(o_dil, o_sb, g_out_dil[l], g_out_sb[l], w_out[l].astype(BF16), xf, _pick(n, 1024), _pick(d, 512))

        kc, vc = _mem_kv(mem, g_mem[l], w_kv_cross[l].astype(BF16), g_k_cross[l])
        wr = jnp.concatenate([w_router[l], w_group[l],
                              jnp.zeros((d, LANE - n_exp - N_GROUPS), F32)], axis=1)
        wr_hi = wr.astype(BF16)
        wr_lo = (wr - wr_hi.astype(F32)).astype(BF16)
        br = jnp.concatenate([b_router[l].reshape(-1), b_group[l],
                              jnp.zeros((LANE - n_exp - N_GROUPS,), F32)]).reshape(1, LANE)
        x2, hp, eid, gate = _cross_router(x1, g_cross[l], w_q_cross[l].astype(BF16), g_q_cross[l] * scale, kc, vc,
                                          w_o_cross[l].astype(BF16), g_ffn[l], wr_hi, wr_lo, br, seq,
                                          _pick(seq, 256))

        n_used, tile_e, tile_blk, slot_tok, dest = _moe_schedule(eid[:, :top_k], n_exp, rt, n_tiles)
        y = _moe_experts(n_used, tile_e, tile_blk, slot_tok, hp, w_exp_gate[l], w_exp_up[l], w_exp_down[l],
                         n, n_tiles, rt, kd)
        xf = _moe_combine(dest, x2, gate, y, _pick(n, 256), top_k)
    return xf.reshape(batch, seq, d)
```

```python
import functools
import math

import jax
import jax.numpy as jnp
from jax import lax
from jax.experimental import pallas as pl
from jax.experimental.pallas import tpu as pltpu

F32 = jnp.float32
BF16 = jnp.bfloat16
I32 = jnp.int32
U32 = jnp.uint32

LANE = 128
HEAD_DIM = 128
EPS = 1e-6
DIL_PAIRS = ((128, 1), (512, 4), (2048, 16))
DIL_STEPS = 128
DIL_UNROLL = 16
ALIBI_MAX_BIAS = 8.0
N_GROUPS = 8
EXPERTS_PER_GROUP = 8
MOE_ROW_ALIGN = 64
NEG_BIG = -1e30
VMEM_LIMIT_CAP = 60000 * 1024
VMEM_INTERNAL = 12 * 1024 * 1024
LANE_NONE = 1e9


def _vmem_limit(*block_bytes):
    return int(min(VMEM_LIMIT_CAP, 2 * sum(block_bytes) + VMEM_INTERNAL))


def _nbytes(shape, dtype):
    return math.prod(shape) * jnp.dtype(dtype).itemsize


def _params(sem, *block_bytes):
    return pltpu.CompilerParams(dimension_semantics=sem, vmem_limit_bytes=_vmem_limit(*block_bytes))


def _rms(x, g):
    return x * lax.rsqrt(jnp.mean(x * x, axis=-1, keepdims=True) + EPS) * g


def _rmsnorm_cast_kernel(x_ref, g_ref, o_ref):
    o_ref[...] = _rms(x_ref[...], g_ref[...]).astype(o_ref.dtype)


def _rmsnorm_cast(x, g, tm):
    n, d = x.shape
    return pl.pallas_call(
        _rmsnorm_cast_kernel,
        grid=(n // tm,),
        in_specs=[pl.BlockSpec((tm, d), lambda i: (i, 0)), pl.BlockSpec((1, d), lambda i: (0, 0))],
        out_specs=pl.BlockSpec((tm, d), lambda i: (i, 0)),
        out_shape=jax.ShapeDtypeStruct((n, d), BF16),
        compiler_params=_params(("arbitrary",), _nbytes((tm, d), F32), _nbytes((tm, d), BF16)),
        name="rmsnorm_cast",
    )(x, g.reshape(1, d))


def _proj_kernel(a_ref, w_ref, g_ref, o_ref, *, n_norm_tiles):
    j = pl.program_id(1)
    acc = jnp.dot(a_ref[...], w_ref[...], preferred_element_type=F32)
    tn = acc.shape[1]

    @pl.when(j < n_norm_tiles)
    def _():
        for h in range(tn // HEAD_DIM):
            sl = slice(h * HEAD_DIM, (h + 1) * HEAD_DIM)
            o_ref[:, sl] = _rms(acc[:, sl], g_ref[:, sl]).astype(o_ref.dtype)

    @pl.when(j >= n_norm_tiles)
    def _():
        o_ref[...] = (acc * g_ref[...]).astype(o_ref.dtype)


def _proj(a, w, gain, col0, ncols, norm_cols, out_dtype, tm, tn, name):
    n, k = a.shape
    assert col0 % tn == 0 and ncols % tn == 0 and norm_cols % tn == 0 and n % tm == 0
    jb = col0 // tn
    return pl.pallas_call(
        functools.partial(_proj_kernel, n_norm_tiles=norm_cols // tn),
        grid=(n // tm, ncols // tn),
        in_specs=[
            pl.BlockSpec((tm, k), lambda i, j: (i, 0)),
            pl.BlockSpec((k, tn), lambda i, j: (0, j + jb)),
            pl.BlockSpec((1, tn), lambda i, j: (0, j)),
        ],
        out_specs=pl.BlockSpec((tm, tn), lambda i, j: (i, j)),
        out_shape=jax.ShapeDtypeStruct((n, ncols), out_dtype),
        compiler_params=_params(("arbitrary", "arbitrary"), _nbytes((tm, k), BF16), _nbytes((k, tn), BF16),
                                _nbytes((tm, tn), F32)),
        name=name,
    )(a, w, gain.reshape(1, ncols))


def _dil_kernel(slope_ref, q_ref, k_ref, v_ref, o_ref, ob0, ob1, ob2, ls0, ls1, ls2, *, tq):
    h = pl.program_id(1)
    t0 = pl.program_id(2) * tq
    slope = slope_ref[h]
    J = DIL_STEPS
    qi = lax.broadcasted_iota(I32, (J, 2 * J), 0)
    ki = lax.broadcasted_iota(I32, (J, 2 * J), 1)
    dist = J + qi - ki
    valid = (dist >= 0) & (dist <= J)
    distf = dist.astype(F32)
    in_cur = ki >= J
    ones_v = jnp.ones((2 * J, HEAD_DIM), BF16)

    for (window, d), ob, ls in zip(DIL_PAIRS, (ob0, ob1, ob2), (ls0, ls1, ls2)):
        assert window // d == J and tq % (d * J) == 0
        bias = jnp.where(valid, (-slope * d) * distf, NEG_BIG)
        shift = d.bit_length() - 1

        def ld(ref, start, d=d):
            if d == 1:
                return ref[pl.ds(start, J), :]
            return ref[pl.ds(start, J, stride=d), :]

        def unit(u, carry, d=d, shift=shift, bias=bias, ob=ob, ls=ls, ld=ld):
            off = (u & (d - 1)) + (u >> shift) * (d * J)
            cur0 = t0 + off
            prev0 = cur0 - d * J
            has_prev = prev0 >= 0
            prev0 = jnp.maximum(prev0, 0)
            qs = ld(q_ref, off).astype(BF16)
            kcat = jnp.concatenate([ld(k_ref, prev0), ld(k_ref, cur0)], axis=0).astype(BF16)
            vcat = jnp.concatenate([ld(v_ref, prev0), ld(v_ref, cur0)], axis=0).astype(BF16)
            s = lax.dot_general(qs, kcat, (((1,), (1,)), ((), ())), preferred_element_type=F32) + bias
            s = jnp.where(jnp.logical_or(in_cur, has_prev), s, NEG_BIG)
            m = jnp.max(s, axis=1, keepdims=True)
            p = jnp.exp(s - m)
            ov = jnp.dot(p.astype(BF16), jnp.concatenate([vcat, ones_v], axis=1), preferred_element_type=F32)
            l = ov[:, HEAD_DIM:]
            o = ov[:, :HEAD_DIM] / l
            lse = m + jnp.log(l)
            if d == 1:
                ob[pl.ds(off, J), :] = o
                ls[pl.ds(off, J), :] = lse
            else:
                ob[pl.ds(off, J, stride=d), :] = o
                ls[pl.ds(off, J, stride=d), :] = lse
            return carry

        lax.fori_loop(0, tq // J, unit, 0, unroll=DIL_UNROLL)

    l0, l1, l2 = ls0[...], ls1[...], ls2[...]
    mx = jnp.maximum(jnp.maximum(l0, l1), l2)
    w0, w1, w2 = jnp.exp(l0 - mx), jnp.exp(l1 - mx), jnp.exp(l2 - mx)
    o_ref[...] = ((w0 * ob0[...] + w1 * ob1[...] + w2 * ob2[...]) / (w0 + w1 + w2)).astype(o_ref.dtype)


def _dilated(qkv, slopes, batch, seq, n_heads, tq):
    assert seq % tq == 0
    n = batch * seq
    nq = seq // tq
    blk = (tq, HEAD_DIM)
    full = (seq, HEAD_DIM)
    scr = [pltpu.VMEM(blk, F32)] * 6
    return pl.pallas_call(
        functools.partial(_dil_kernel, tq=tq),
        grid=(batch, n_heads, nq),
        in_specs=[
            pl.BlockSpec(memory_space=pltpu.SMEM),
            pl.BlockSpec(blk, lambda b, h, c: (b * nq + c, h)),
            pl.BlockSpec(full, lambda b, h, c: (b, n_heads + h)),
            pl.BlockSpec(full, lambda b, h, c: (b, 2 * n_heads + h)),
        ],
        out_specs=pl.BlockSpec(blk, lambda b, h, c: (b * nq + c, h)),
        out_shape=jax.ShapeDtypeStruct((n, n_heads * HEAD_DIM), BF16),
        scratch_shapes=scr,
        compiler_params=_params(("arbitrary",) * 3, 2 * _nbytes(full, F32), 5 * _nbytes(blk, F32)),
        name="dilated_attn",
    )(slopes, qkv, qkv, qkv)


def _sb_kernel(q_ref, k_ref, v_ref, o_ref, *, tq, sub):
    n_chain = tq // sub
    assert sub & (sub - 1) == 0
    kb0 = pl.program_id(2) * n_chain
    row = lax.broadcasted_iota(I32, (sub, sub), 0)
    col = lax.broadcasted_iota(I32, (sub, sub), 1)
    after = (row > col).astype(BF16)
    after2 = jnp.concatenate([after, after], axis=0)
    qrow = lax.broadcasted_iota(I32, (tq, sub), 0)
    causal = lax.broadcasted_iota(I32, (tq, sub), 1) < (qrow & (sub - 1))
    chain_of_row = lax.broadcasted_iota(I32, (tq, 1), 0) >> (sub.bit_length() - 1)
    sign = jnp.uint32(0x80000000)

    def sweep(kbs, drop, acc, diag, first_valid):
        starts = [pl.multiple_of(kb * sub, sub) for kb in kbs]
        z2 = jnp.concatenate(
            [lax.dot_general(q_ref[c * sub:(c + 1) * sub, :], k_ref[pl.ds(starts[c], sub), :],
                             (((1,), (1,)), ((), ())), preferred_element_type=F32) for c in range(n_chain)], axis=0)
        neg_abs = pltpu.bitcast(pltpu.bitcast(z2, U32) | sign, F32)
        sp2 = jnp.maximum(z2, 0.0) + jnp.log2(1.0 + jnp.exp2(neg_abs))
        log_beta2 = z2 - sp2
        if diag:
            sp2 = jnp.where(causal, sp2, 0.0)
        hi = sp2.astype(BF16)
        lo = (sp2 - hi.astype(F32)).astype(BF16)
        later = jnp.dot(jnp.concatenate([hi, lo], axis=1), after2, preferred_element_type=F32)
        a = jnp.exp2(log_beta2 - later)
        if diag:
            a = jnp.where(causal, a, 0.0)
        a = a.astype(BF16)
        pv = jnp.concatenate(
            [jnp.dot(a[c * sub:(c + 1) * sub, :], v_ref[pl.ds(starts[c], sub), :], preferred_element_type=F32)
             for c in range(n_chain)], axis=0)
        scale = jnp.exp2(-drop)
        new_drop = drop + later[:, 0:1] + sp2[:, 0:1]
        if first_valid is not None:
            valid = chain_of_row >= first_valid
            scale = jnp.where(valid, scale, 0.0)
            new_drop = jnp.where(valid, new_drop, drop)
        return new_drop, acc + scale * pv

    def alive(drop, first_valid):
        live = jnp.where(chain_of_row >= first_valid, jnp.exp2(-drop), 0.0)
        return (jnp.max(live) > 0.0).astype(I32)

    drop, acc = sweep([kb0 + c for c in range(n_chain)], jnp.zeros((tq, 1), F32),
                      jnp.zeros((tq, HEAD_DIM), F32), True, None)

    def cond(st):
        return jnp.logical_and(st[0] <= kb0 + n_chain - 1, st[1] > 0)

    def body(st):
        g, _, drop, acc = st
        drop, acc = sweep([jnp.maximum(kb0 + c - g, 0) for c in range(n_chain)], drop, acc, False, g - kb0)
        return g + 1, alive(drop, g + 1 - kb0), drop, acc

    _, _, _, acc = lax.while_loop(cond, body, (jnp.int32(1), alive(drop, 1 - kb0), drop, acc))
    o_ref[...] = acc.astype(o_ref.dtype)


def _stick_breaking(qkv, batch, seq, n_heads, tq, sub):
    assert seq % tq == 0 and tq % sub == 0
    n = batch * seq
    nq = seq // tq
    blk = (tq, HEAD_DIM)
    full = (seq, HEAD_DIM)
    return pl.pallas_call(
        functools.partial(_sb_kernel, tq=tq, sub=sub),
        grid=(batch, n_heads, nq),
        in_specs=[
            pl.BlockSpec(blk, lambda b, h, i: (b * nq + i, h)),
            pl.BlockSpec(full, lambda b, h, i: (b, n_heads + h)),
            pl.BlockSpec(full, lambda b, h, i: (b, 2 * n_heads + h)),
        ],
        out_specs=pl.BlockSpec(blk, lambda b, h, i: (b * nq + i, h)),
        out_shape=jax.ShapeDtypeStruct((n, n_heads * HEAD_DIM), BF16),
        compiler_params=_params(("arbitrary",) * 3, 2 * _nbytes(full, BF16), 2 * _nbytes(blk, F32),
                                4 * _nbytes((tq, sub), F32)),
        name="stick_breaking",
    )(qkv, qkv, qkv)


def _outproj_kernel(od_ref, os_ref, gd_ref, gs_ref, w_ref, x_ref, o_ref, m_scr):
    wd = od_ref.shape[1]

    @pl.when(pl.program_id(1) == 0)
    def _():
        m_scr[:, :wd] = _rms(od_ref[...].astype(F32), gd_ref[...]).astype(BF16)
        m_scr[:, wd:] = _rms(os_ref[...].astype(F32), gs_ref[...]).astype(BF16)

    o_ref[...] = x_ref[...] + jnp.dot(m_scr[...], w_ref[...], preferred_element_type=F32)


def _outproj(o_dil, o_sb, g_dil, g_sb, w, x, tm, tn):
    n, wd = o_dil.shape
    ws = o_sb.shape[1]
    d = x.shape[1]
    return pl.pallas_call(
        _outproj_kernel,
        grid=(n // tm, d // tn),
        in_specs=[
            pl.BlockSpec((tm, wd), lambda i, j: (i, 0)),
            pl.BlockSpec((tm, ws), lambda i, j: (i, 0)),
            pl.BlockSpec((1, wd), lambda i, j: (0, 0)),
            pl.BlockSpec((1, ws), lambda i, j: (0, 0)),
            pl.BlockSpec((wd + ws, tn), lambda i, j: (0, j)),
            pl.BlockSpec((tm, tn), lambda i, j: (i, j)),
        ],
        out_specs=pl.BlockSpec((tm, tn), lambda i, j: (i, j)),
        out_shape=jax.ShapeDtypeStruct((n, d), F32),
        scratch_shapes=[pltpu.VMEM((tm, wd + ws), BF16)],
        compiler_params=_params(("arbitrary", "arbitrary"), _nbytes((tm, wd + ws), BF16),
                                _nbytes((wd + ws, tn), BF16), 2 * _nbytes((tm, tn), F32),
                                _nbytes((tm, wd + ws), BF16) // 2),
        name="out_proj",
    )(o_dil, o_sb, g_dil.reshape(1, wd), g_sb.reshape(1, ws), w, x)


def _mem_kv_kernel(m_ref, gm_ref, w_ref, gk_ref, k_ref, v_ref):
    hm = _rms(m_ref[0], gm_ref[...]).astype(BF16)
    kv = jnp.dot(hm, w_ref[...], preferred_element_type=F32)
    wc = k_ref.shape[2]
    for h in range(wc // HEAD_DIM):
        sl = slice(h * HEAD_DIM, (h + 1) * HEAD_DIM)
        k_ref[0, :, sl] = _rms(kv[:, sl], gk_ref[...]).astype(BF16)
    v_ref[0] = kv[:, wc:].astype(BF16)


def _mem_kv(mem, g_mem, w_kv, g_k):
    b, m, d = mem.shape
    wc = w_kv.shape[1] // 2
    out = jax.ShapeDtypeStruct((b, m, wc), BF16)
    return pl.pallas_call(
        _mem_kv_kernel,
        grid=(b,),
        in_specs=[
            pl.BlockSpec((1, m, d), lambda i: (i, 0, 0)),
            pl.BlockSpec((1, d), lambda i: (0, 0)),
            pl.BlockSpec((d, 2 * wc), lambda i: (0, 0)),
            pl.BlockSpec((1, HEAD_DIM), lambda i: (0, 0)),
        ],
        out_specs=[pl.BlockSpec((1, m, wc), lambda i: (i, 0, 0))] * 2,
        out_shape=[out, out],
        compiler_params=_params(("arbitrary",), _nbytes((m, d), F32), _nbytes((d, 2 * wc), BF16)),
        name="mem_kv",
    )(mem, g_mem.reshape(1, d), w_kv, g_k.reshape(1, HEAD_DIM))


def _pack_bf16_pairs(xr):
    half = xr.shape[1] // 2
    lo = pltpu.bitcast(xr[:, :half], U32)
    hi = pltpu.bitcast(xr[:, half:], U32)
    return hi | (lo >> 16)


def _cross_router_kernel(x1_ref, gc_ref, wq_ref, gq_ref, kc_ref, vc_ref, wo_ref, gf_ref, wrh_ref, wrl_ref, br_ref,
                         x2_ref, hp_ref, eid_ref, gate_ref):
    x1 = x1_ref[...]
    h = _rms(x1, gc_ref[...]).astype(BF16)
    q = jnp.dot(h, wq_ref[...], preferred_element_type=F32)
    heads = []
    for hd in range(q.shape[1] // HEAD_DIM):
        sl = slice(hd * HEAD_DIM, (hd + 1) * HEAD_DIM)
        qh = _rms(q[:, sl], gq_ref[...]).astype(BF16)
        s = lax.dot_general(qh, kc_ref[0, :, sl], (((1,), (1,)), ((), ())), preferred_element_type=F32)
        p = jnp.exp(s - jnp.max(s, axis=1, keepdims=True))
        l = jnp.sum(p, axis=1, keepdims=True)
        heads.append(jnp.dot(p.astype(BF16), vc_ref[0, :, sl], preferred_element_type=F32) / l)
    o = jnp.concatenate(heads, axis=1).astype(BF16)
    x2 = x1 + jnp.dot(o, wo_ref[...], preferred_element_type=F32)
    x2_ref[...] = x2

    hf = _rms(x2, gf_ref[...])
    hf_hi = hf.astype(BF16)
    hf_hi32 = hf_hi.astype(F32)
    hf_lo = (hf - hf_hi32).astype(BF16)
    packed = _pack_bf16_pairs(hf_hi32)
    tm = packed.shape[0]
    nb = packed.shape[1] // LANE
    for j in range(nb):
        hp_ref[pl.ds(j, tm, stride=nb), :] = packed[:, j * LANE:(j + 1) * LANE]

    n_exp = N_GROUPS * EXPERTS_PER_GROUP
    lg = (jnp.dot(hf_hi, wrh_ref[...], preferred_element_type=F32)
          + jnp.dot(hf_hi, wrl_ref[...], preferred_element_type=F32)
          + jnp.dot(hf_lo, wrh_ref[...], preferred_element_type=F32)) + br_ref[...]
    lane = lax.broadcasted_iota(I32, lg.shape, 1)
    lanef = lane.astype(F32)
    ninf = -jnp.inf
    is_g = (lane >= n_exp) & (lane < n_exp + N_GROUPS)
    gl = jnp.where(is_g, lg, ninf)
    gmax = jnp.max(gl, axis=1, keepdims=True)
    gidx = jnp.min(jnp.where(gl == gmax, lanef, LANE_NONE), axis=1, keepdims=True) - n_exp
    g_gate = 1.0 / jnp.sum(jnp.exp(gl - gmax), axis=1, keepdims=True)
    in_grp = (lane < n_exp) & ((lane // EXPERTS_PER_GROUP).astype(F32) == gidx)
    el = jnp.where(in_grp, lg, ninf)
    m1 = jnp.max(el, axis=1, keepdims=True)
    i1 = jnp.min(jnp.where(el == m1, lanef, LANE_NONE), axis=1, keepdims=True)
    el2 = jnp.where(lanef == i1, ninf, el)
    m2 = jnp.max(el2, axis=1, keepdims=True)
    i2 = jnp.min(jnp.where(el2 == m2, lanef, LANE_NONE), axis=1, keepdims=True)
    p2 = jnp.exp(m2 - m1)
    den = 1.0 + p2
    eid_ref[...] = jnp.where(lane == 0, i1, jnp.where(lane == 1, i2, 0.0)).astype(I32)
    gate_ref[...] = jnp.where(lane == 0, g_gate / den, jnp.where(lane == 1, g_gate * p2 / den, 0.0))


def _cross_router(x1, g_cross, wq, gq, kc, vc, wo, g_ffn, wr_hi, wr_lo, br, seq, tm):
    n, d = x1.shape
    wc = wq.shape[1]
    m = kc.shape[1]
    tiles_per_batch = seq // tm
    const = lambda i: (0, 0)
    return pl.pallas_call(
        _cross_router_kernel,
        grid=(n // tm,),
        in_specs=[
            pl.BlockSpec((tm, d), lambda i: (i, 0)),
            pl.BlockSpec((1, d), const),
            pl.BlockSpec((d, wc), const),
            pl.BlockSpec((1, HEAD_DIM), const),
            pl.BlockSpec((1, m, wc), lambda i: (i // tiles_per_batch, 0, 0)),
            pl.BlockSpec((1, m, wc), lambda i: (i // tiles_per_batch, 0, 0)),
            pl.BlockSpec((wc, d), const),
            pl.BlockSpec((1, d), const),
            pl.BlockSpec((d, LANE), const),
            pl.BlockSpec((d, LANE), const),
            pl.BlockSpec((1, LANE), const),
        ],
        out_specs=[
            pl.BlockSpec((tm, d), lambda i: (i, 0)),
            pl.BlockSpec((tm * (d // 2 // LANE), LANE), lambda i: (i, 0)),
            pl.BlockSpec((tm, LANE), lambda i: (i, 0)),
            pl.BlockSpec((tm, LANE), lambda i: (i, 0)),
        ],
        out_shape=[
            jax.ShapeDtypeStruct((n, d), F32),
            jax.ShapeDtypeStruct((n * (d // 2 // LANE), LANE), U32),
            jax.ShapeDtypeStruct((n, LANE), I32),
            jax.ShapeDtypeStruct((n, LANE), F32),
        ],
        compiler_params=_params(("arbitrary",), 2 * _nbytes((tm, d), F32), _nbytes((tm, d // 2), U32),
                                2 * _nbytes((d, wc), BF16), 2 * _nbytes((d, LANE), BF16)),
        name="cross_router",
    )(x1, g_cross.reshape(1, d), wq, gq.reshape(1, HEAD_DIM), kc, vc, wo, g_ffn.reshape(1, d), wr_hi, wr_lo, br)


def _moe_up_kernel(nused_ref, te_ref, tblk_ref, tok_ref, tok_next_ref, hp_ref, wg_ref, wu_ref, h_ref,
                   x_scr, stage, g_acc, u_acc, sem, *, rt, nb, kd):
    t = pl.program_id(0)
    k = pl.program_id(1)
    n_used = nused_ref[0]
    half = nb * LANE
    himask = jnp.uint32(0xFFFF0000)

    def start_tile(tok):
        def body(i, c):
            for pri in range(2):
                r = 2 * i + pri
                pltpu.make_async_copy(hp_ref.at[pl.ds(tok[0, 0, r] * nb, nb)],
                                      stage.at[pl.ds(r * nb, nb)], sem).start(priority=pri)
            return c

        lax.fori_loop(0, rt // 2, body, 0, unroll=4)

    def put(col, val):
        x_scr[col // kd, :, col % kd:col % kd + LANE] = val.astype(BF16)

    @pl.when(t < n_used)
    def _():
        @pl.when(k == 0)
        def _():
            @pl.when(t == 0)
            def _():
                start_tile(tok_ref)

            pltpu.make_async_copy(hp_ref.at[pl.ds(0, rt * nb)], stage, sem).wait()
            for j in range(nb):
                w = stage[pl.ds(j, rt, stride=nb), :]
                put(j * LANE, pltpu.bitcast(w << 16, F32))
                put(half + j * LANE, pltpu.bitcast(w & himask, F32))

            @pl.when(t + 1 < n_used)
            def _():
                start_tile(tok_next_ref)

        x = x_scr[k]
        g = jnp.dot(x, wg_ref[0].astype(BF16), preferred_element_type=F32)
        u = jnp.dot(x, wu_ref[0].astype(BF16), preferred_element_type=F32)

        @pl.when(k == 0)
        def _():
            g_acc[...] = g
            u_acc[...] = u

        @pl.when(k > 0)
        def _():
            g_acc[...] += g
            u_acc[...] += u

        @pl.when(k == pl.num_programs(1) - 1)
        def _():
            gg = g_acc[...]
            h_ref[...] = (gg * jax.nn.sigmoid(gg) * u_acc[...]).astype(BF16)


def _moe_down_kernel(nused_ref, te_ref, tblk_ref, h_ref, wd_ref, y_ref, wd_buf, wsem):
    t = pl.program_id(0)
    n_used = nused_ref[0]
    slot = t % 2
    half = wd_buf.shape[1] // 2

    def weights(tile, s):
        e = te_ref[tile]
        return [(pltpu.make_async_copy(wd_ref.at[e, pl.ds(i * half, half)],
                                       wd_buf.at[s, pl.ds(i * half, half)], wsem.at[s]), i) for i in range(2)]

    @pl.when(t < n_used)
    def _():
        @pl.when(t == 0)
        def _():
            for cp, pri in weights(0, 0):
                cp.start(priority=pri)

        @pl.when(t + 1 < n_used)
        def _():
            for cp, pri in weights(t + 1, 1 - slot):
                cp.start(priority=pri)

        for cp, _ in weights(t, slot):
            cp.wait()
        y = jnp.dot(h_ref[...], wd_buf[slot].astype(BF16), preferred_element_type=F32)
        y_ref[...] = _pack_bf16_pairs(y.astype(BF16).astype(F32))


def _moe_experts(n_used, tile_e, tile_blk, slot_tok, hp, w_gate, w_up, w_down, n, n_tiles, rt, kd):
    n_exp, d, de = w_gate.shape
    nb = hp.shape[0] // n
    assert 2 * nb * LANE == d and kd % LANE == 0 and hp.shape[0] >= rt * nb
    p_rows = n_tiles * rt
    nk = d // kd
    tok3 = slot_tok.reshape(n_tiles, 1, rt)

    def inner(t, k, nused, last):
        return jnp.where(t < nused[0], k, last)

    h_mid = pl.pallas_call(
        functools.partial(_moe_up_kernel, rt=rt, nb=nb, kd=kd),
        grid_spec=pltpu.PrefetchScalarGridSpec(
            num_scalar_prefetch=3,
            grid=(n_tiles, nk),
            in_specs=[
                pl.BlockSpec((1, 1, rt), lambda t, k, nu, te, tb: (t, 0, 0), memory_space=pltpu.SMEM),
                pl.BlockSpec((1, 1, rt), lambda t, k, nu, te, tb: (jnp.minimum(t + 1, n_tiles - 1), 0, 0),
                             memory_space=pltpu.SMEM),
                pl.BlockSpec(memory_space=pl.ANY),
                pl.BlockSpec((1, kd, de), lambda t, k, nu, te, tb: (te[t], inner(t, k, nu, nk - 1), 0)),
                pl.BlockSpec((1, kd, de), lambda t, k, nu, te, tb: (te[t], inner(t, k, nu, nk - 1), 0)),
            ],
            out_specs=pl.BlockSpec((rt, de), lambda t, k, nu, te, tb: (tb[t], 0)),
            scratch_shapes=[pltpu.VMEM((nk, rt, kd), BF16), pltpu.VMEM((rt * nb, LANE), U32),
                            pltpu.VMEM((rt, de), F32), pltpu.VMEM((rt, de), F32), pltpu.SemaphoreType.DMA(())],
        ),
        out_shape=jax.ShapeDtypeStruct((p_rows, de), BF16),
        compiler_params=_params(("arbitrary", "arbitrary"), 2 * _nbytes((kd, de), F32), _nbytes((rt, de), F32),
                                _nbytes((rt * nb, LANE), U32) // 2, _nbytes((rt, d), BF16) // 2),
        name="moe_gate_up",
    )(n_used, tile_e, tile_blk, tok3, tok3, hp, w_gate, w_up)

    return pl.pallas_call(
        _moe_down_kernel,
        grid_spec=pltpu.PrefetchScalarGridSpec(
            num_scalar_prefetch=3,
            grid=(n_tiles,),
            in_specs=[
                pl.BlockSpec((rt, de), lambda t, nu, te, tb: (tb[t], 0)),
                pl.BlockSpec(memory_space=pl.ANY),
            ],
            out_specs=pl.BlockSpec((rt, d // 2), lambda t, nu, te, tb: (tb[t], 0)),
            scratch_shapes=[pltpu.VMEM((2, de, d), F32), pltpu.SemaphoreType.DMA((2,))],
        ),
        out_shape=jax.ShapeDtypeStruct((p_rows, d // 2), U32),
        compiler_params=_params(("arbitrary",), _nbytes((rt, de), BF16), _nbytes((de, d), F32),
                                _nbytes((rt, d // 2), U32), _nbytes((rt, d), F32) // 2),
        name="moe_down",
    )(n_used, tile_e, tile_blk, h_mid, w_down)


def _moe_combine_kernel(pos_ref, pos_next_ref, x_ref, g_ref, y_ref, o_ref, buf0, buf1, sem0, sem1, *, tm, top_k):
    i = pl.program_id(0)
    bufs, sems = (buf0, buf1), (sem0, sem1)

    def start_step(pos, slot):
        def body(r, c):
            for k in range(top_k):
                pltpu.make_async_copy(y_ref.at[pl.ds(pos[0, 0, top_k * r + k], 1)],
                                      bufs[slot].at[pl.ds(k * tm + r, 1)], sems[slot]).start(priority=k % 2)
            return c

        lax.fori_loop(0, tm, body, 0, unroll=8)

    for slot in range(2):
        @pl.when(i % 2 == slot)
        def _(slot=slot):
            if slot == 0:
                @pl.when(i == 0)
                def _():
                    start_step(pos_ref, 0)

            @pl.when(i + 1 < pl.num_programs(0))
            def _():
                start_step(pos_next_ref, 1 - slot)

            pltpu.make_async_copy(y_ref.at[pl.ds(0, top_k * tm)], bufs[slot], sems[slot]).wait()
            half = x_ref.shape[1] // 2
            lo = x_ref[:, :half]
            hi = x_ref[:, half:]
            for k in range(top_k):
                w = bufs[slot][k * tm:(k + 1) * tm, :]
                g = g_ref[:, k:k + 1]
                lo = lo + g * pltpu.bitcast(w << 16, F32)
                hi = hi + g * pltpu.bitcast(w & jnp.uint32(0xFFFF0000), F32)
            o_ref[:, :half] = lo
            o_ref[:, half:] = hi


def _moe_combine(dest, x2, gate, y, tm, top_k):
    n, d = x2.shape
    steps = n // tm
    assert y.shape[0] >= top_k * tm
    pos3 = dest.reshape(steps, 1, tm * top_k)
    stage = pltpu.VMEM((top_k * tm, d // 2), U32)
    return pl.pallas_call(
        functools.partial(_moe_combine_kernel, tm=tm, top_k=top_k),
        grid=(steps,),
        in_specs=[
            pl.BlockSpec((1, 1, tm * top_k), lambda i: (i, 0, 0), memory_space=pltpu.SMEM),
            pl.BlockSpec((1, 1, tm * top_k), lambda i: (jnp.minimum(i + 1, steps - 1), 0, 0),
                         memory_space=pltpu.SMEM),
            pl.BlockSpec((tm, d), lambda i: (i, 0)),
            pl.BlockSpec((tm, LANE), lambda i: (i, 0)),
            pl.BlockSpec(memory_space=pl.ANY),
        ],
        out_specs=pl.BlockSpec((tm, d), lambda i: (i, 0)),
        out_shape=jax.ShapeDtypeStruct((n, d), F32),
        scratch_shapes=[stage, stage, pltpu.SemaphoreType.DMA(()), pltpu.SemaphoreType.DMA(())],
        compiler_params=_params(("arbitrary",), 2 * _nbytes((tm, d), F32), _nbytes((top_k * tm, d // 2), U32)),
        name="moe_combine",
    )(pos3, pos3, x2, gate, y)


def _moe_schedule(eid, n_exp, rt, n_tiles):
    n, top_k = eid.shape
    a = n * top_k
    eflat = eid.reshape(a)
    onehot = (eflat[:, None] == jnp.arange(n_exp, dtype=I32)[None, :]).astype(I32)
    csum = jnp.cumsum(onehot, axis=0)
    rank = jnp.sum(csum * onehot, axis=1) - 1
    counts = csum[-1]
    nt = (counts + rt - 1) // rt
    tcum = jnp.cumsum(nt)
    tstart = tcum - nt
    n_used = tcum[-1]
    dest = tstart[eflat] * rt + rank
    tid = jnp.arange(n_tiles, dtype=I32)
    tile_blk = jnp.minimum(tid, n_used - 1)
    tile_e = jnp.minimum(jnp.searchsorted(tcum, tile_blk, side="right"), n_exp - 1).astype(I32)
    tok = jnp.arange(a, dtype=I32) // top_k
    slot_tok = jnp.zeros((n_tiles * rt,), I32).at[dest].set(tok, unique_indices=True, mode="promise_in_bounds")
    return n_used.reshape(1).astype(I32), tile_e, tile_blk.astype(I32), slot_tok, dest.astype(I32)


def _pick(n, pref):
    t = min(n, pref)
    assert n % t == 0
    return t


def kernel(x, mem, g_mix, w_in, g_q_dil, g_k_dil, g_out_dil, g_out_sb, w_out, g_cross, g_mem, w_q_cross, w_kv_cross, g_q_cross, g_k_cross, w_o_cross, g_ffn, w_group, b_group, w_router, b_router, w_exp_gate, w_exp_up, w_exp_down):
    batch, seq, d = x.shape
    n = batch * seq
    n_heads_mix = d // HEAD_DIM
    n_dil = n_heads_mix // 2
    n_sb = n_heads_mix - n_dil
    w_dil, w_sb = n_dil * HEAD_DIM, n_sb * HEAD_DIM
    n_exp = w_exp_gate.shape[1]
    top_k = 2
    scale = 1.0 / math.sqrt(HEAD_DIM)
    slopes = jnp.exp2(-ALIBI_MAX_BIAS * jnp.arange(1, n_dil + 1, dtype=F32) / n_dil)

    tm = _pick(n, 512)
    rt = -(-(9 * n * top_k) // (8 * n_exp * MOE_ROW_ALIGN)) * MOE_ROW_ALIGN
    n_tiles = -(-n * top_k // rt) + n_exp
    kd = max(d // 2, LANE)

    xf = x.reshape(n, d)
    for l in range(g_mix.shape[0]):
        h = _rmsnorm_cast(xf, g_mix[l], tm)
        w_in_b = w_in[l].astype(BF16)
        gain_dil = jnp.concatenate([jnp.tile(g_q_dil[l] * scale, n_dil), jnp.tile(g_k_dil[l], n_dil),
                                    jnp.ones((w_dil,), F32)])
        gain_sb = jnp.concatenate([jnp.full((w_sb,), scale * math.log2(math.e), F32), jnp.ones((2 * w_sb,), F32)])
        tn = _pick(w_dil, 1024)
        tmp = _pick(n, 1024)
        qkv_dil = _proj(h, w_in_b, gain_dil, 0, 3 * w_dil, 2 * w_dil, F32, tmp, tn, "in_proj_dil")
        qkv_sb = _proj(h, w_in_b, gain_sb, 3 * w_dil, 3 * w_sb, 0, BF16, tmp, tn, "in_proj_sb")
        o_dil = _dilated(qkv_dil, slopes, batch, seq, n_dil, _pick(seq, 2048))
        o_sb = _stick_breaking(qkv_sb, batch, seq, n_sb, _pick(seq, 2048), _pick(seq, 256))
        x1 = _outproj(o_dil, o_sb, g_out_dil[l], g_out_sb[l], w_out[l].astype(BF16), xf, _pick(n, 1024), _pick(d, 512))

        kc, vc = _mem_kv(mem, g_mem[l], w_kv_cross[l].astype(BF16), g_k_cross[l])
        wr = jnp.concatenate([w_router[l], w_group[l],
                              jnp.zeros((d, LANE - n_exp - N_GROUPS), F32)], axis=1)
        wr_hi = wr.astype(BF16)
        wr_lo = (wr - wr_hi.astype(F32)).astype(BF16)
        br = jnp.concatenate([b_router[l].reshape(-1), b_group[l],
                              jnp.zeros((LANE - n_exp - N_GROUPS,), F32)]).reshape(1, LANE)
        x2, hp, eid, gate = _cross_router(x1, g_cross[l], w_q_cross[l].astype(BF16), g_q_cross[l] * scale, kc, vc,
                                          w_o_cross[l].astype(BF16), g_ffn[l], wr_hi, wr_lo, br, seq,
                                          _pick(seq, 256))

        n_used, tile_e, tile_blk, slot_tok, dest = _moe_schedule(eid[:, :top_k], n_exp, rt, n_tiles)
        y = _moe_experts(n_used, tile_e, tile_blk, slot_tok, hp, w_exp_gate[l], w_exp_up[l], w_exp_down[l],
                         n, n_tiles, rt, kd)
        xf = _moe_combine(dest, x2, gate, y, _pick(n, 256), top_k)
    return xf.reshape(batch, seq, d)
```

```python
import functools
import math

import jax
import jax.numpy as jnp
from jax import lax
from jax.experimental import pallas as pl
from jax.experimental.pallas import tpu as pltpu

F32 = jnp.float32
BF16 = jnp.bfloat16
I32 = jnp.int32
U32 = jnp.uint32

LANE = 128
HEAD_DIM = 128
EPS = 1e-6
DIL_PAIRS = ((128, 1), (512, 4), (2048, 16))
DIL_STEPS = 128
DIL_UNROLL = 16
ALIBI_MAX_BIAS = 8.0
N_GROUPS = 8
EXPERTS_PER_GROUP = 8
MOE_ROW_ALIGN = 64
NEG_BIG = -1e30
VMEM_LIMIT_CAP = 60000 * 1024
VMEM_INTERNAL = 12 * 1024 * 1024
LANE_NONE = 1e9


def _vmem_limit(*block_bytes):
    return int(min(VMEM_LIMIT_CAP, 2 * sum(block_bytes) + VMEM_INTERNAL))


def _nbytes(shape, dtype):
    return math.prod(shape) * jnp.dtype(dtype).itemsize


def _params(sem, *block_bytes):
    return pltpu.CompilerParams(dimension_semantics=sem, vmem_limit_bytes=_vmem_limit(*block_bytes))


def _rms(x, g):
    return x * lax.rsqrt(jnp.mean(x * x, axis=-1, keepdims=True) + EPS) * g


def _rmsnorm_cast_kernel(x_ref, g_ref, o_ref):
    o_ref[...] = _rms(x_ref[...], g_ref[...]).astype(o_ref.dtype)


def _rmsnorm_cast(x, g, tm):
    n, d = x.shape
    return pl.pallas_call(
        _rmsnorm_cast_kernel,
        grid=(n // tm,),
        in_specs=[pl.BlockSpec((tm, d), lambda i: (i, 0)), pl.BlockSpec((1, d), lambda i: (0, 0))],
        out_specs=pl.BlockSpec((tm, d), lambda i: (i, 0)),
        out_shape=jax.ShapeDtypeStruct((n, d), BF16),
        compiler_params=_params(("arbitrary",), _nbytes((tm, d), F32), _nbytes((tm, d), BF16)),
        name="rmsnorm_cast",
    )(x, g.reshape(1, d))


def _proj_kernel(a_ref, w_ref, g_ref, o_ref, *, n_norm_tiles):
    j = pl.program_id(1)
    acc = jnp.dot(a_ref[...], w_ref[...], preferred_element_type=F32)
    tn = acc.shape[1]

    @pl.when(j < n_norm_tiles)
    def _():
        for h in range(tn // HEAD_DIM):
            sl = slice(h * HEAD_DIM, (h + 1) * HEAD_DIM)
            o_ref[:, sl] = _rms(acc[:, sl], g_ref[:, sl]).astype(o_ref.dtype)

    @pl.when(j >= n_norm_tiles)
    def _():
        o_ref[...] = (acc * g_ref[...]).astype(o_ref.dtype)


def _proj(a, w, gain, col0, ncols, norm_cols, out_dtype, tm, tn, name):
    n, k = a.shape
    assert col0 % tn == 0 and ncols % tn == 0 and norm_cols % tn == 0 and n % tm == 0
    jb = col0 // tn
    return pl.pallas_call(
        functools.partial(_proj_kernel, n_norm_tiles=norm_cols // tn),
        grid=(n // tm, ncols // tn),
        in_specs=[
            pl.BlockSpec((tm, k), lambda i, j: (i, 0)),
            pl.BlockSpec((k, tn), lambda i, j: (0, j + jb)),
            pl.BlockSpec((1, tn), lambda i, j: (0, j)),
        ],
        out_specs=pl.BlockSpec((tm, tn), lambda i, j: (i, j)),
        out_shape=jax.ShapeDtypeStruct((n, ncols), out_dtype),
        compiler_params=_params(("arbitrary", "arbitrary"), _nbytes((tm, k), BF16), _nbytes((k, tn), BF16),
                                _nbytes((tm, tn), F32)),
        name=name,
    )(a, w, gain.reshape(1, ncols))


def _dil_kernel(slope_ref, q_ref, k_ref, v_ref, o_ref, ob0, ob1, ob2, ls0, ls1, ls2, *, tq):
    h = pl.program_id(1)
    t0 = pl.program_id(2) * tq
    slope = slope_ref[h]
    J = DIL_STEPS
    qi = lax.broadcasted_iota(I32, (J, 2 * J), 0)
    ki = lax.broadcasted_iota(I32, (J, 2 * J), 1)
    dist = J + qi - ki
    valid = (dist >= 0) & (dist <= J)
    distf = dist.astype(F32)
    in_cur = ki >= J
    ones_v = jnp.ones((2 * J, HEAD_DIM), BF16)

    for (window, d), ob, ls in zip(DIL_PAIRS, (ob0, ob1, ob2), (ls0, ls1, ls2)):
        assert window // d == J and tq % (d * J) == 0
        bias = jnp.where(valid, (-slope * d) * distf, NEG_BIG)
        shift = d.bit_length() - 1

        def ld(ref, start, d=d):
            if d == 1:
                return ref[pl.ds(start, J), :]
            return ref[pl.ds(start, J, stride=d), :]

        def unit(u, carry, d=d, shift=shift, bias=bias, ob=ob, ls=ls, ld=ld):
            off = (u & (d - 1)) + (u >> shift) * (d * J)
            cur0 = t0 + off
            prev0 = cur0 - d * J
            has_prev = prev0 >= 0
            prev0 = jnp.maximum(prev0, 0)
            qs = ld(q_ref, off).astype(BF16)
            kcat = jnp.concatenate([ld(k_ref, prev0), ld(k_ref, cur0)], axis=0).astype(BF16)
            vcat = jnp.concatenate([ld(v_ref, prev0), ld(v_ref, cur0)], axis=0).astype(BF16)
            s = lax.dot_general(qs, kcat, (((1,), (1,)), ((), ())), preferred_element_type=F32) + bias
            s = jnp.where(jnp.logical_or(in_cur, has_prev), s, NEG_BIG)
            m = jnp.max(s, axis=1, keepdims=True)
            p = jnp.exp(s - m)
            ov = jnp.dot(p.astype(BF16), jnp.concatenate([vcat, ones_v], axis=1), preferred_element_type=F32)
            l = ov[:, HEAD_DIM:]
            o = ov[:, :HEAD_DIM] / l
            lse = m + jnp.log(l)
            if d == 1:
                ob[pl.ds(off, J), :] = o
                ls[pl.ds(off, J), :] = lse
            else:
                ob[pl.ds(off, J, stride=d), :] = o
                ls[pl.ds(off, J, stride=d), :] = lse
            return carry

        lax.fori_loop(0, tq // J, unit, 0, unroll=DIL_UNROLL)

    l0, l1, l2 = ls0[...], ls1[...], ls2[...]
    mx = jnp.maximum(jnp.maximum(l0, l1), l2)
    w0, w1, w2 = jnp.exp(l0 - mx), jnp.exp(l1 - mx), jnp.exp(l2 - mx)
    o_ref[...] = ((w0 * ob0[...] + w1 * ob1[...] + w2 * ob2[...]) / (w0 + w1 + w2)).astype(o_ref.dtype)


def _dilated(qkv, slopes, batch, seq, n_heads, tq):
    assert seq % tq == 0
    n = batch * seq
    nq = seq // tq
    blk = (tq, HEAD_DIM)
    full = (seq, HEAD_DIM)
    scr = [pltpu.VMEM(blk, F32)] * 6
    return pl.pallas_call(
        functools.partial(_dil_kernel, tq=tq),
        grid=(batch, n_heads, nq),
        in_specs=[
            pl.BlockSpec(memory_space=pltpu.SMEM),
            pl.BlockSpec(blk, lambda b, h, c: (b * nq + c, h)),
            pl.BlockSpec(full, lambda b, h, c: (b, n_heads + h)),
            pl.BlockSpec(full, lambda b, h, c: (b, 2 * n_heads + h)),
        ],
        out_specs=pl.BlockSpec(blk, lambda b, h, c: (b * nq + c, h)),
        out_shape=jax.ShapeDtypeStruct((n, n_heads * HEAD_DIM), BF16),
        scratch_shapes=scr,
        compiler_params=_params(("arbitrary",) * 3, 2 * _nbytes(full, F32), 5 * _nbytes(blk, F32)),
        name="dilated_attn",
    )(slopes, qkv, qkv, qkv)


def _sb_kernel(q_ref, k_ref, v_ref, o_ref, *, tq, sub):
    n_chain = tq // sub
    assert sub & (sub - 1) == 0
    kb0 = pl.program_id(2) * n_chain
    row = lax.broadcasted_iota(I32, (sub, sub), 0)
    col = lax.broadcasted_iota(I32, (sub, sub), 1)
    after = (row > col).astype(BF16)
    after2 = jnp.concatenate([after, after], axis=0)
    qrow = lax.broadcasted_iota(I32, (tq, sub), 0)
    causal = lax.broadcasted_iota(I32, (tq, sub), 1) < (qrow & (sub - 1))
    chain_of_row = lax.broadcasted_iota(I32, (tq, 1), 0) >> (sub.bit_length() - 1)
    sign = jnp.uint32(0x80000000)

    def sweep(kbs, drop, acc, diag, first_valid):
        starts = [pl.multiple_of(kb * sub, sub) for kb in kbs]
        z2 = jnp.concatenate(
            [lax.dot_general(q_ref[c * sub:(c + 1) * sub, :], k_ref[pl.ds(starts[c], sub), :],
                             (((1,), (1,)), ((), ())), preferred_element_type=F32) for c in range(n_chain)], axis=0)
        neg_abs = pltpu.bitcast(pltpu.bitcast(z2, U32) | sign, F32)
        sp2 = jnp.maximum(z2, 0.0) + jnp.log2(1.0 + jnp.exp2(neg_abs))
        log_beta2 = z2 - sp2
        if diag:
            sp2 = jnp.where(causal, sp2, 0.0)
        hi = sp2.astype(BF16)
        lo = (sp2 - hi.astype(F32)).astype(BF16)
        later = jnp.dot(jnp.concatenate([hi, lo], axis=1), after2, preferred_element_type=F32)
        a = jnp.exp2(log_beta2 - later)
        if diag:
            a = jnp.where(causal, a, 0.0)
        a = a.astype(BF16)
        pv = jnp.concatenate(
            [jnp.dot(a[c * sub:(c + 1) * sub, :], v_ref[pl.ds(starts[c], sub), :], preferred_element_type=F32)
             for c in range(n_chain)], axis=0)
        scale = jnp.exp2(-drop)
        new_drop = drop + later[:, 0:1] + sp2[:, 0:1]
        if first_valid is not None:
            valid = chain_of_row >= first_valid
            scale = jnp.where(valid, scale, 0.0)
            new_drop = jnp.where(valid, new_drop, drop)
        return new_drop, acc + scale * pv

    def alive(drop, first_valid):
        live = jnp.where(chain_of_row >= first_valid, jnp.exp2(-drop), 0.0)
        return (jnp.max(live) > 0.0).astype(I32)

    drop, acc = sweep([kb0 + c for c in range(n_chain)], jnp.zeros((tq, 1), F32),
                      jnp.zeros((tq, HEAD_DIM), F32), True, None)

    def cond(st):
        return jnp.logical_and(st[0] <= kb0 + n_chain - 1, st[1] > 0)

    def body(st):
        g, _, drop, acc = st
        drop, acc = sweep([jnp.maximum(kb0 + c - g, 0) for c in range(n_chain)], drop, acc, False, g - kb0)
        return g + 1, alive(drop, g + 1 - kb0), drop, acc

    _, _, _, acc = lax.while_loop(cond, body, (jnp.int32(1), alive(drop, 1 - kb0), drop, acc))
    o_ref[...] = acc.astype(o_ref.dtype)


def _stick_breaking(qkv, batch, seq, n_heads, tq, sub):
    assert seq % tq == 0 and tq % sub == 0
    n = batch * seq
    nq = seq // tq
    blk = (tq, HEAD_DIM)
    full = (seq, HEAD_DIM)
    return pl.pallas_call(
        functools.partial(_sb_kernel, tq=tq, sub=sub),
        grid=(batch, n_heads, nq),
        in_specs=[
            pl.BlockSpec(blk, lambda b, h, i: (b * nq + i, h)),
            pl.BlockSpec(full, lambda b, h, i: (b, n_heads + h)),
            pl.BlockSpec(full, lambda b, h, i: (b, 2 * n_heads + h)),
        ],
        out_specs=pl.BlockSpec(blk, lambda b, h, i: (b * nq + i, h)),
        out_shape=jax.ShapeDtypeStruct((n, n_heads * HEAD_DIM), BF16),
        compiler_params=_params(("arbitrary",) * 3, 2 * _nbytes(full, BF16), 2 * _nbytes(blk, F32),
                                4 * _nbytes((tq, sub), F32)),
        name="stick_breaking",
    )(qkv, qkv, qkv)


def _outproj_kernel(od_ref, os_ref, gd_ref, gs_ref, w_ref, x_ref, o_ref, m_scr):
    wd = od_ref.shape[1]

    @pl.when(pl.program_id(1) == 0)
    def _():
        m_scr[:, :wd] = _rms(od_ref[...].astype(F32), gd_ref[...]).astype(BF16)
        m_scr[:, wd:] = _rms(os_ref[...].astype(F32), gs_ref[...]).astype(BF16)

    o_ref[...] = x_ref[...] + jnp.dot(m_scr[...], w_ref[...], preferred_element_type=F32)


def _outproj(o_dil, o_sb, g_dil, g_sb, w, x, tm, tn):
    n, wd = o_dil.shape
    ws = o_sb.shape[1]
    d = x.shape[1]
    return pl.pallas_call(
        _outproj_kernel,
        grid=(n // tm, d // tn),
        in_specs=[
            pl.BlockSpec((tm, wd), lambda i, j: (i, 0)),
            pl.BlockSpec((tm, ws), lambda i, j: (i, 0)),
            pl.BlockSpec((1, wd), lambda i, j: (0, 0)),
            pl.BlockSpec((1, ws), lambda i, j: (0, 0)),
            pl.BlockSpec((wd + ws, tn), lambda i, j: (0, j)),
            pl.BlockSpec((tm, tn), lambda i, j: (i, j)),
        ],
        out_specs=pl.BlockSpec((tm, tn), lambda i, j: (i, j)),
        out_shape=jax.ShapeDtypeStruct((n, d), F32),
        scratch_shapes=[pltpu.VMEM((tm, wd + ws), BF16)],
        compiler_params=_params(("arbitrary", "arbitrary"), _nbytes((tm, wd + ws), BF16),
                                _nbytes((wd + ws, tn), BF16), 2 * _nbytes((tm, tn), F32),
                                _nbytes((tm, wd + ws), BF16) // 2),
        name="out_proj",
    )(o_dil, o_sb, g_dil.reshape(1, wd), g_sb.reshape(1, ws), w, x)


def _mem_kv_kernel(m_ref, gm_ref, w_ref, gk_ref, k_ref, v_ref):
    hm = _rms(m_ref[0], gm_ref[...]).astype(BF16)
    kv = jnp.dot(hm, w_ref[...], preferred_element_type=F32)
    wc = k_ref.shape[2]
    for h in range(wc // HEAD_DIM):
        sl = slice(h * HEAD_DIM, (h + 1) * HEAD_DIM)
        k_ref[0, :, sl] = _rms(kv[:, sl], gk_ref[...]).astype(BF16)
    v_ref[0] = kv[:, wc:].astype(BF16)


def _mem_kv(mem, g_mem, w_kv, g_k):
    b, m, d = mem.shape
    wc = w_kv.shape[1] // 2
    out = jax.ShapeDtypeStruct((b, m, wc), BF16)
    return pl.pallas_call(
        _mem_kv_kernel,
        grid=(b,),
        in_specs=[
            pl.BlockSpec((1, m, d), lambda i: (i, 0, 0)),
            pl.BlockSpec((1, d), lambda i: (0, 0)),
            pl.BlockSpec((d, 2 * wc), lambda i: (0, 0)),
            pl.BlockSpec((1, HEAD_DIM), lambda i: (0, 0)),
        ],
        out_specs=[pl.BlockSpec((1, m, wc), lambda i: (i, 0, 0))] * 2,
        out_shape=[out, out],
        compiler_params=_params(("arbitrary",), _nbytes((m, d), F32), _nbytes((d, 2 * wc), BF16)),
        name="mem_kv",
    )(mem, g_mem.reshape(1, d), w_kv, g_k.reshape(1, HEAD_DIM))


def _pack_bf16_pairs(xr):
    half = xr.shape[1] // 2
    lo = pltpu.bitcast(xr[:, :half], U32)
    hi = pltpu.bitcast(xr[:, half:], U32)
    return hi | (lo >> 16)


def _cross_router_kernel(x1_ref, gc_ref, wq_ref, gq_ref, kc_ref, vc_ref, wo_ref, gf_ref, wrh_ref, wrl_ref, br_ref,
                         x2_ref, hp_ref, eid_ref, gate_ref):
    x1 = x1_ref[...]
    h = _rms(x1, gc_ref[...]).astype(BF16)
    q = jnp.dot(h, wq_ref[...], preferred_element_type=F32)
    heads = []
    for hd in range(q.shape[1] // HEAD_DIM):
        sl = slice(hd * HEAD_DIM, (hd + 1) * HEAD_DIM)
        qh = _rms(q[:, sl], gq_ref[...]).astype(BF16)
        s = lax.dot_general(qh, kc_ref[0, :, sl], (((1,), (1,)), ((), ())), preferred_element_type=F32)
        p = jnp.exp(s - jnp.max(s, axis=1, keepdims=True))
        l = jnp.sum(p, axis=1, keepdims=True)
        heads.append(jnp.dot(p.astype(BF16), vc_ref[0, :, sl], preferred_element_type=F32) / l)
    o = jnp.concatenate(heads, axis=1).astype(BF16)
    x2 = x1 + jnp.dot(o, wo_ref[...], preferred_element_type=F32)
    x2_ref[...] = x2

    hf = _rms(x2, gf_ref[...])
    hf_hi = hf.astype(BF16)
    hf_hi32 = hf_hi.astype(F32)
    hf_lo = (hf - hf_hi32).astype(BF16)
    packed = _pack_bf16_pairs(hf_hi32)
    tm = packed.shape[0]
    nb = packed.shape[1] // LANE
    for j in range(nb):
        hp_ref[pl.ds(j, tm, stride=nb), :] = packed[:, j * LANE:(j + 1) * LANE]

    n_exp = N_GROUPS * EXPERTS_PER_GROUP
    lg = (jnp.dot(hf_hi, wrh_ref[...], preferred_element_type=F32)
          + jnp.dot(hf_hi, wrl_ref[...], preferred_element_type=F32)
          + jnp.dot(hf_lo, wrh_ref[...], preferred_element_type=F32)) + br_ref[...]
    lane = lax.broadcasted_iota(I32, lg.shape, 1)
    lanef = lane.astype(F32)
    ninf = -jnp.inf
    is_g = (lane >= n_exp) & (lane < n_exp + N_GROUPS)
    gl = jnp.where(is_g, lg, ninf)
    gmax = jnp.max(gl, axis=1, keepdims=True)
    gidx = jnp.min(jnp.where(gl == gmax, lanef, LANE_NONE), axis=1, keepdims=True) - n_exp
    g_gate = 1.0 / jnp.sum(jnp.exp(gl - gmax), axis=1, keepdims=True)
    in_grp = (lane < n_exp) & ((lane // EXPERTS_PER_GROUP).astype(F32) == gidx)
    el = jnp.where(in_grp, lg, ninf)
    m1 = jnp.max(el, axis=1, keepdims=True)
    i1 = jnp.min(jnp.where(el == m1, lanef, LANE_NONE), axis=1, keepdims=True)
    el2 = jnp.where(lanef == i1, ninf, el)
    m2 = jnp.max(el2, axis=1, keepdims=True)
    i2 = jnp.min(jnp.where(el2 == m2, lanef, LANE_NONE), axis=1, keepdims=True)
    p2 = jnp.exp(m2 - m1)
    den = 1.0 + p2
    eid_ref[...] = jnp.where(lane == 0, i1, jnp.where(lane == 1, i2, 0.0)).astype(I32)
    gate_ref[...] = jnp.where(lane == 0, g_gate / den, jnp.where(lane == 1, g_gate * p2 / den, 0.0))


def _cross_router(x1, g_cross, wq, gq, kc, vc, wo, g_ffn, wr_hi, wr_lo, br, seq, tm):
    n, d = x1.shape
    wc = wq.shape[1]
    m = kc.shape[1]
    tiles_per_batch = seq // tm
    const = lambda i: (0, 0)
    return pl.pallas_call(
        _cross_router_kernel,
        grid=(n // tm,),
        in_specs=[
            pl.BlockSpec((tm, d), lambda i: (i, 0)),
            pl.BlockSpec((1, d), const),
            pl.BlockSpec((d, wc), const),
            pl.BlockSpec((1, HEAD_DIM), const),
            pl.BlockSpec((1, m, wc), lambda i: (i // tiles_per_batch, 0, 0)),
            pl.BlockSpec((1, m, wc), lambda i: (i // tiles_per_batch, 0, 0)),
            pl.BlockSpec((wc, d), const),
            pl.BlockSpec((1, d), const),
            pl.BlockSpec((d, LANE), const),
            pl.BlockSpec((d, LANE), const),
            pl.BlockSpec((1, LANE), const),
        ],
        out_specs=[
            pl.BlockSpec((tm, d), lambda i: (i, 0)),
            pl.BlockSpec((tm * (d // 2 // LANE), LANE), lambda i: (i, 0)),
            pl.BlockSpec((tm, LANE), lambda i: (i, 0)),
            pl.BlockSpec((tm, LANE), lambda i: (i, 0)),
        ],
        out_shape=[
            jax.ShapeDtypeStruct((n, d), F32),
            jax.ShapeDtypeStruct((n * (d // 2 // LANE), LANE), U32),
            jax.ShapeDtypeStruct((n, LANE), I32),
            jax.ShapeDtypeStruct((n, LANE), F32),
        ],
        compiler_params=_params(("arbitrary",), 2 * _nbytes((tm, d), F32), _nbytes((tm, d // 2), U32),
                                2 * _nbytes((d, wc), BF16), 2 * _nbytes((d, LANE), BF16)),
        name="cross_router",
    )(x1, g_cross.reshape(1, d), wq, gq.reshape(1, HEAD_DIM), kc, vc, wo, g_ffn.reshape(1, d), wr_hi, wr_lo, br)


def _moe_up_kernel(nused_ref, te_ref, tblk_ref, tok_ref, tok_next_ref, hp_ref, wg_ref, wu_ref, h_ref,
                   x_scr, stage, g_acc, u_acc, sem, *, rt, nb, kd):
    t = pl.program_id(0)
    k = pl.program_id(1)
    n_used = nused_ref[0]
    half = nb * LANE
    himask = jnp.uint32(0xFFFF0000)

    def start_tile(tok):
        def body(i, c):
            for pri in range(2):
                r = 2 * i + pri
                pltpu.make_async_copy(hp_ref.at[pl.ds(tok[0, 0, r] * nb, nb)],
                                      stage.at[pl.ds(r * nb, nb)], sem).start(priority=pri)
            return c

        lax.fori_loop(0, rt // 2, body, 0, unroll=4)

    def put(col, val):
        x_scr[col // kd, :, col % kd:col % kd + LANE] = val.astype(BF16)

    @pl.when(t < n_used)
    def _():
        @pl.when(k == 0)
        def _():
            @pl.when(t == 0)
            def _():
                start_tile(tok_ref)

            pltpu.make_async_copy(hp_ref.at[pl.ds(0, rt * nb)], stage, sem).wait()
            for j in range(nb):
                w = stage[pl.ds(j, rt, stride=nb), :]
                put(j * LANE, pltpu.bitcast(w << 16, F32))
                put(half + j * LANE, pltpu.bitcast(w & himask, F32))

            @pl.when(t + 1 < n_used)
            def _():
                start_tile(tok_next_ref)

        x = x_scr[k]
        g = jnp.dot(x, wg_ref[0].astype(BF16), preferred_element_type=F32)
        u = jnp.dot(x, wu_ref[0].astype(BF16), preferred_element_type=F32)

        @pl.when(k == 0)
        def _():
            g_acc[...] = g
            u_acc[...] = u

        @pl.when(k > 0)
        def _():
            g_acc[...] += g
            u_acc[...] += u

        @pl.when(k == pl.num_programs(1) - 1)
        def _():
            gg = g_acc[...]
            h_ref[...] = (gg * jax.nn.sigmoid(gg) * u_acc[...]).astype(BF16)


def _moe_down_kernel(nused_ref, te_ref, tblk_ref, h_ref, wd_ref, y_ref):
    @pl.when(pl.program_id(0) < nused_ref[0])
    def _():
        y = jnp.dot(h_ref[...], wd_ref[0].astype(BF16), preferred_element_type=F32)
        y_ref[...] = _pack_bf16_pairs(y.astype(BF16).astype(F32))


def _moe_experts(n_used, tile_e, tile_blk, slot_tok, hp, w_gate, w_up, w_down, n, n_tiles, rt, kd):
    n_exp, d, de = w_gate.shape
    nb = hp.shape[0] // n
    assert 2 * nb * LANE == d and kd % LANE == 0 and hp.shape[0] >= rt * nb
    p_rows = n_tiles * rt
    nk = d // kd
    tok3 = slot_tok.reshape(n_tiles, 1, rt)

    def inner(t, k, nused, last):
        return jnp.where(t < nused[0], k, last)

    h_mid = pl.pallas_call(
        functools.partial(_moe_up_kernel, rt=rt, nb=nb, kd=kd),
        grid_spec=pltpu.PrefetchScalarGridSpec(
            num_scalar_prefetch=3,
            grid=(n_tiles, nk),
            in_specs=[
                pl.BlockSpec((1, 1, rt), lambda t, k, nu, te, tb: (t, 0, 0), memory_space=pltpu.SMEM),
                pl.BlockSpec((1, 1, rt), lambda t, k, nu, te, tb: (jnp.minimum(t + 1, n_tiles - 1), 0, 0),
                             memory_space=pltpu.SMEM),
                pl.BlockSpec(memory_space=pl.ANY),
                pl.BlockSpec((1, kd, de), lambda t, k, nu, te, tb: (te[t], inner(t, k, nu, nk - 1), 0)),
                pl.BlockSpec((1, kd, de), lambda t, k, nu, te, tb: (te[t], inner(t, k, nu, nk - 1), 0)),
            ],
            out_specs=pl.BlockSpec((rt, de), lambda t, k, nu, te, tb: (tb[t], 0)),
            scratch_shapes=[pltpu.VMEM((nk, rt, kd), BF16), pltpu.VMEM((rt * nb, LANE), U32),
                            pltpu.VMEM((rt, de), F32), pltpu.VMEM((rt, de), F32), pltpu.SemaphoreType.DMA(())],
        ),
        out_shape=jax.ShapeDtypeStruct((p_rows, de), BF16),
        compiler_params=_params(("arbitrary", "arbitrary"), 2 * _nbytes((kd, de), F32), _nbytes((rt, de), F32),
                                _nbytes((rt * nb, LANE), U32) // 2, _nbytes((rt, d), BF16) // 2),
        name="moe_gate_up",
    )(n_used, tile_e, tile_blk, tok3, tok3, hp, w_gate, w_up)

    return pl.pallas_call(
        _moe_down_kernel,
        grid_spec=pltpu.PrefetchScalarGridSpec(
            num_scalar_prefetch=3,
            grid=(n_tiles,),
            in_specs=[
                pl.BlockSpec((rt, de), lambda t, nu, te, tb: (tb[t], 0)),
                pl.BlockSpec((1, de, d), lambda t, nu, te, tb: (te[t], 0, 0)),
            ],
            out_specs=pl.BlockSpec((rt, d // 2), lambda t, nu, te, tb: (tb[t], 0)),
        ),
        out_shape=jax.ShapeDtypeStruct((p_rows, d // 2), U32),
        compiler_params=_params(("arbitrary",), _nbytes((rt, de), BF16), _nbytes((de, d), F32),
                                _nbytes((rt, d // 2), U32), _nbytes((rt, d), F32) // 2),
        name="moe_down",
    )(n_used, tile_e, tile_blk, h_mid, w_down)


def _moe_combine_kernel(pos_ref, pos_next_ref, x_ref, g_ref, y_ref, o_ref, buf0, buf1, sem0, sem1, *, tm, top_k):
    i = pl.program_id(0)
    bufs, sems = (buf0, buf1), (sem0, sem1)

    def start_step(pos, slot):
        def body(r, c):
            for k in range(top_k):
                pltpu.make_async_copy(y_ref.at[pl.ds(pos[0, 0, top_k * r + k], 1)],
                                      bufs[slot].at[pl.ds(k * tm + r, 1)], sems[slot]).start(priority=k % 2)
            return c

        lax.fori_loop(0, tm, body, 0, unroll=8)

    for slot in range(2):
        @pl.when(i % 2 == slot)
        def _(slot=slot):
            if slot == 0:
                @pl.when(i == 0)
                def _():
                    start_step(pos_ref, 0)

            @pl.when(i + 1 < pl.num_programs(0))
            def _():
                start_step(pos_next_ref, 1 - slot)

            pltpu.make_async_copy(y_ref.at[pl.ds(0, top_k * tm)], bufs[slot], sems[slot]).wait()
            half = x_ref.shape[1] // 2
            lo = x_ref[:, :half]
            hi = x_ref[:, half:]
            for k in range(top_k):
                w = bufs[slot][k * tm:(k + 1) * tm, :]
                g = g_ref[:, k:k + 1]
                lo = lo + g * pltpu.bitcast(w << 16, F32)
                hi = hi + g * pltpu.bitcast(w & jnp.uint32(0xFFFF0000), F32)
            o_ref[:, :half] = lo
            o_ref[:, half:] = hi


def _moe_combine(dest, x2, gate, y, tm, top_k):
    n, d = x2.shape
    steps = n // tm
    assert y.shape[0] >= top_k * tm
    pos3 = dest.reshape(steps, 1, tm * top_k)
    stage = pltpu.VMEM((top_k * tm, d // 2), U32)
    return pl.pallas_call(
        functools.partial(_moe_combine_kernel, tm=tm, top_k=top_k),
        grid=(steps,),
        in_specs=[
            pl.BlockSpec((1, 1, tm * top_k), lambda i: (i, 0, 0), memory_space=pltpu.SMEM),
            pl.BlockSpec((1, 1, tm * top_k), lambda i: (jnp.minimum(i + 1, steps - 1), 0, 0),
                         memory_space=pltpu.SMEM),
            pl.BlockSpec((tm, d), lambda i: (i, 0)),
            pl.BlockSpec((tm, LANE), lambda i: (i, 0)),
            pl.BlockSpec(memory_space=pl.ANY),
        ],
        out_specs=pl.BlockSpec((tm, d), lambda i: (i, 0)),
        out_shape=jax.ShapeDtypeStruct((n, d), F32),
        scratch_shapes=[stage, stage, pltpu.SemaphoreType.DMA(()), pltpu.SemaphoreType.DMA(())],
        compiler_params=_params(("arbitrary",), 2 * _nbytes((tm, d), F32), _nbytes((top_k * tm, d // 2), U32)),
        name="moe_combine",
    )(pos3, pos3, x2, gate, y)


def _moe_schedule(eid, n_exp, rt, n_tiles):
    n, top_k = eid.shape
    a = n * top_k
    eflat = eid.reshape(a)
    onehot = (eflat[:, None] == jnp.arange(n_exp, dtype=I32)[None, :]).astype(I32)
    csum = jnp.cumsum(onehot, axis=0)
    rank = jnp.sum(csum * onehot, axis=1) - 1
    counts = csum[-1]
    nt = (counts + rt - 1) // rt
    tcum = jnp.cumsum(nt)
    tstart = tcum - nt
    n_used = tcum[-1]
    dest = tstart[eflat] * rt + rank
    tid = jnp.arange(n_tiles, dtype=I32)
    tile_blk = jnp.minimum(tid, n_used - 1)
    tile_e = jnp.minimum(jnp.searchsorted(tcum, tile_blk, side="right"), n_exp - 1).astype(I32)
    tok = jnp.arange(a, dtype=I32) // top_k
    slot_tok = jnp.zeros((n_tiles * rt,), I32).at[dest].set(tok, unique_indices=True, mode="promise_in_bounds")
    return n_used.reshape(1).astype(I32), tile_e, tile_blk.astype(I32), slot_tok, dest.astype(I32)


def _pick(n, pref):
    t = min(n, pref)
    assert n % t == 0
    return t


def kernel(x, mem, g_mix, w_in, g_q_dil, g_k_dil, g_out_dil, g_out_sb, w_out, g_cross, g_mem, w_q_cross, w_kv_cross, g_q_cross, g_k_cross, w_o_cross, g_ffn, w_group, b_group, w_router, b_router, w_exp_gate, w_exp_up, w_exp_down):
    batch, seq, d = x.shape
    n = batch * seq
    n_heads_mix = d // HEAD_DIM
    n_dil = n_heads_mix // 2
    n_sb = n_heads_mix - n_dil
    w_dil, w_sb = n_dil * HEAD_DIM, n_sb * HEAD_DIM
    n_exp = w_exp_gate.shape[1]
    top_k = 2
    scale = 1.0 / math.sqrt(HEAD_DIM)
    slopes = jnp.exp2(-ALIBI_MAX_BIAS * jnp.arange(1, n_dil + 1, dtype=F32) / n_dil)

    tm = _pick(n, 512)
    rt = -(-(9 * n * top_k) // (8 * n_exp * MOE_ROW_ALIGN)) * MOE_ROW_ALIGN
    n_tiles = -(-n * top_k // rt) + n_exp
    kd = max(d // 2, LANE)

    xf = x.reshape(n, d)
    for l in range(g_mix.shape[0]):
        h = _rmsnorm_cast(xf, g_mix[l], tm)
        w_in_b = w_in[l].astype(BF16)
        gain_dil = jnp.concatenate([jnp.tile(g_q_dil[l] * scale, n_dil), jnp.tile(g_k_dil[l], n_dil),
                                    jnp.ones((w_dil,), F32)])
        gain_sb = jnp.concatenate([jnp.full((w_sb,), scale * math.log2(math.e), F32), jnp.ones((2 * w_sb,), F32)])
        tn = _pick(w_dil, 1024)
        tmp = _pick(n, 1024)
        qkv_dil = _proj(h, w_in_b, gain_dil, 0, 3 * w_dil, 2 * w_dil, F32, tmp, tn, "in_proj_dil")
        qkv_sb = _proj(h, w_in_b, gain_sb, 3 * w_dil, 3 * w_sb, 0, BF16, tmp, tn, "in_proj_sb")
        o_dil = _dilated(qkv_dil, slopes, batch, seq, n_dil, _pick(seq, 2048))
        o_sb = _stick_breaking(qkv_sb, batch, seq, n_sb, _pick(seq, 2048), _pick(seq, 256))
        x1 = _outproj(o_dil, o_sb, g_out_dil[l], g_out_sb[l], w_out[l].astype(BF16), xf, _pick(n, 1024), _pick(d, 512))

        kc, vc = _mem_kv(mem, g_mem[l], w_kv_cross[l].astype(BF16), g_k_cross[l])
        wr = jnp.concatenate([w_router[l], w_group[l],
                              jnp.zeros((d, LANE - n_exp - N_GROUPS), F32)], axis=1)
        wr_hi = wr.astype(BF16)
        wr_lo = (wr - wr_hi.astype(F32)).astype(BF16)
        br = jnp.concatenate([b_router[l].reshape(-1), b_group[l],
                              jnp.zeros((LANE - n_exp - N_GROUPS,), F32)]).reshape(1, LANE)
        x2, hp, eid, gate = _cross_router(x1, g_cross[l], w_q_cross[l].astype(BF16), g_q_cross[l] * scale, kc, vc,
                                          w_o_cross[l].astype(BF16), g_ffn[l], wr_hi, wr_lo, br, seq,
                                          _pick(seq, 256))

        n_used, tile_e, tile_blk, slot_tok, dest = _moe_schedule(eid[:, :top_k], n_exp, rt, n_tiles)
        y = _moe_experts(n_used, tile_e, tile_blk, slot_tok, hp, w_exp_gate[l], w_exp_up[l], w_exp_down[l],
                         n, n_tiles, rt, kd)
        xf = _moe_combine(dest, x2, gate, y, _pick(n, 256), top_k)
    return xf.reshape(batch, seq, d)
```
